```python
import math
import jax
import jax.numpy as jnp
from jax import lax
import numpy as np

D_MODEL = 2048
BATCH = 1
SEQ = 16384
DEPTH = 1

CTX_LEN = 256
GRID_W = 64
N_MOD = 9
D_FF = 5632
RET_HEADS = 8
RET_DK = 128
RET_DV = 256
RET_CHUNK = 128
RET_Q = RET_HEADS * RET_DK
RET_V = RET_HEADS * RET_DV
HY_WIDTH = 2048
HY_BANDS = 16
HY_EMB = 1 + 2 * HY_BANDS
HY_FF = 64
HY_MIN_DECAY = math.log(1e-2) / 0.3
HY_MAX_DECAY = math.log(1e-2) / 1.5
ROPE_BASE = 10000.0
EPS = 1e-6

Q_OFF = 0
K_OFF = Q_OFF + RET_Q
V_OFF = K_OFF + RET_Q
G_OFF = V_OFF + RET_V
HY_OFF = G_OFF + RET_V
GATE_OFF = HY_OFF + 3 * HY_WIDTH
IN_COLS = GATE_OFF + 2 * D_MODEL

kernel_name = 'hybrid_retention_hyena_dit_block'


def rmsnorm(x, g):
    xf = x.astype(jnp.float32)
    y = xf * lax.rsqrt(jnp.mean(xf * xf, axis=-1, keepdims=True) + EPS)
    return (y * g.astype(jnp.float32)).astype(x.dtype)


def modulate(h, shift, scale):
    return h * (1.0 + scale) + shift


def adaln_mods(cvec, w, b):
    m = jax.nn.silu(cvec) @ w + b
    return m.reshape(-1, 1, N_MOD, D_MODEL)


def swiglu(h, w_up, w_down):
    a, g = jnp.split(h @ w_up, 2, axis=-1)
    return (jax.nn.silu(a) * g) @ w_down


def ffn_half(x, mods, slot, g, w_up, w_down):
    h = modulate(rmsnorm(x, g), mods[:, :, 3 * slot], mods[:, :, 3 * slot + 1])
    return x + 0.5 * mods[:, :, 3 * slot + 2] * swiglu(h, w_up, w_down)


def rope_angles(pos, dim):
    inv = ROPE_BASE ** (-jnp.arange(0, dim, 2, dtype=jnp.float32) / dim)
    return pos.astype(jnp.float32)[:, None] * inv[None, :]


def apply_rope(x, ang):
    x1, x2 = jnp.split(x, 2, axis=-1)
    cos = jnp.cos(ang).astype(x.dtype)
    sin = jnp.sin(ang).astype(x.dtype)
    return jnp.concatenate([x1 * cos - x2 * sin, x1 * sin + x2 * cos], axis=-1)


def rope_2d(x, row, col):
    half = x.shape[-1] // 2
    xr, xc = jnp.split(x, 2, axis=-1)
    return jnp.concatenate([apply_rope(xr, rope_angles(row, half)),
                            apply_rope(xc, rope_angles(col, half))], axis=-1)


def split_heads(a, d):
    b, t, _ = a.shape
    return a.reshape(b, t, RET_HEADS, d).transpose(0, 2, 1, 3)


def retention_scan(q, k, v, log_gamma, state0):
    b, h, t, dk = q.shape
    dv = v.shape[-1]
    n = t // RET_CHUNK
    qc = q.reshape(b, h, n, RET_CHUNK, dk)
    kc = k.reshape(b, h, n, RET_CHUNK, dk)
    vc = v.reshape(b, h, n, RET_CHUNK, dv)
    lg = log_gamma.astype(jnp.float32)[:, None]
    idx = jnp.arange(RET_CHUNK, dtype=jnp.float32)
    diff = idx[:, None] - idx[None, :]
    decay = jnp.where(diff >= 0, jnp.exp(lg[:, :, None] * jnp.maximum(diff, 0.0)), 0.0)
    xi = jnp.exp(lg * (idx + 1.0))
    zeta = jnp.exp(lg * (RET_CHUNK - 1.0 - idx))
    chunk_decay = jnp.exp(lg * RET_CHUNK)[None, :, :, None]
    scores = jnp.einsum('bhnqd,bhnkd->bhnqk', qc, kc) * decay[None, :, None]
    inner = jnp.einsum('bhnqk,bhnkv->bhnqv', scores, vc)
    kv = jnp.einsum('bhnkd,hk,bhnkv->nbhdv', kc, zeta, vc)

    def step(state, kv_i):
        return state * chunk_decay + kv_i, state

    final, prev = lax.scan(step, state0, kv)
    cross = jnp.einsum('bhnqd,nbhdv,hq->bhnqv', qc, prev, xi)
    return (inner + cross).reshape(b, h, t, dv), final


def context_states(pkv, log_gamma):
    k = split_heads(pkv[..., :RET_Q], RET_DK)
    v = split_heads(pkv[..., RET_Q:], RET_DV)
    lc = k.shape[2]
    m = jnp.arange(lc, dtype=jnp.float32)
    lg = log_gamma.astype(jnp.float32)[:, :, None]
    w_f = jnp.exp(lg[0] * (lc - 1.0 - m))
    w_b = jnp.exp(lg[1] * m)
    s_f = jnp.einsum('bhtd,ht,bhtv->bhdv', k, w_f, v)
    s_b = jnp.einsum('bhtd,ht,bhtv->bhdv', k, w_b, v)
    return s_f, s_b


def head_group_norm(y, g, b):
    y = y.astype(jnp.float32)
    mu = jnp.mean(y, axis=-1, keepdims=True)
    var = jnp.mean(jnp.square(y - mu), axis=-1, keepdims=True)
    yn = (y - mu) * lax.rsqrt(var + EPS)
    bsz, h, t, dv = y.shape
    yn = yn.transpose(0, 2, 1, 3).reshape(bsz, t, h * dv)
    return yn * g + b


def retention_mix(p, row, col, log_gamma, st_f, st_b, gn_g, gn_b, w_o):
    q = split_heads(p[..., Q_OFF:K_OFF], RET_DK) * (RET_DK ** -0.5)
    k = split_heads(p[..., K_OFF:V_OFF], RET_DK)
    v = split_heads(p[..., V_OFF:G_OFF], RET_DV)
    if row is not None:
        q = rope_2d(q, row, col)
        k = rope_2d(k, row, col)
    y_f, fin_f = retention_scan(q, k, v, log_gamma[0], st_f)
    y_b, fin_b = retention_scan(jnp.flip(q, 2), jnp.flip(k, 2), jnp.flip(v, 2), log_gamma[1], st_b)
    y = head_group_norm(y_f + jnp.flip(y_b, 2), gn_g, gn_b).astype(p.dtype)
    y = jax.nn.silu(p[..., G_OFF:HY_OFF]) * y
    return y @ w_o, fin_f, fin_b


def short_conv3(u, w, b):
    up = jnp.pad(u, ((0, 0), (1, 1), (0, 0)))
    return up[:, :-2] * w[0] + up[:, 1:-1] * w[1] + up[:, 2:] * w[2] + b


def hyena_filter(length, w1, b1, w2, b2, w3, b3, w4, freq, decay_rate):
    f32 = jnp.float32
    t = jnp.linspace(0.0, 1.0, length, dtype=f32)[:, None]
    w = 2.0 * math.pi * jnp.arange(length, dtype=f32)[:, None] / length
    f = jnp.linspace(1e-4, HY_BANDS - 1.0, HY_BANDS, dtype=f32)[None, :]
    feat = jnp.concatenate([t, jnp.cos(f * w), -jnp.sin(f * w)], axis=-1)
    freq = freq.astype(f32)
    h = jnp.sin(freq * (feat @ w1.astype(f32) + b1.astype(f32)))
    h = jnp.sin(freq * (h @ w2.astype(f32) + b2.astype(f32)))
    h = jnp.sin(freq * (h @ w3.astype(f32) + b3.astype(f32)))
    taps = h @ w4.astype(f32)
    window = jnp.exp(-t * jnp.abs(decay_rate.astype(f32))[None, :])
    h_fwd, h_bwd = jnp.split(taps, 2, axis=-1)
    h_fwd = h_fwd * window
    h_bwd = h_bwd * window
    kern = jnp.concatenate([h_fwd, jnp.zeros_like(h_fwd[:1]), jnp.flip(h_bwd[1:], 0)], axis=0)
    return kern * lax.rsqrt(jnp.sum(kern * kern, axis=0, keepdims=True) + EPS)


def fft_long_conv(z, kern):
    length = z.shape[1]
    zf = jnp.fft.rfft(z.astype(jnp.float32), n=2 * length, axis=1)
    kf = jnp.fft.rfft(kern, n=2 * length, axis=0)
    y = jnp.fft.irfft(zf * kf[None], n=2 * length, axis=1)[:, :length]
    return y.astype(z.dtype)


def hyena_mix(p, conv_w, conv_b, w1, b1, w2, b2, w3, b3, w4, freq, decay_rate, skip, w_o):
    u = short_conv3(p[..., HY_OFF:GATE_OFF], conv_w, conv_b)
    x0, x1, v = jnp.split(u, 3, axis=-1)
    kern = hyena_filter(p.shape[1], w1, b1, w2, b2, w3, b3, w4, freq, decay_rate)
    z = x1 * v
    z = fft_long_conv(z, kern) + z * skip
    return (x0 * z) @ w_o


def merge_branches(p, ret_out, hy_out, w_out):
    g_ret, g_hy = jnp.split(jax.nn.sigmoid(p[..., GATE_OFF:]), 2, axis=-1)
    return (g_ret * ret_out + g_hy * hy_out) @ w_out


def setup_inputs(seed: int = 0) -> dict:
    key = jax.random.key(seed)
    ks = jax.random.split(key, 32)
    f32 = jnp.float32

    def nrm(k, shape, scale):
        return jax.random.normal(k, shape, f32) * scale

    x = nrm(ks[0], (BATCH, SEQ, D_MODEL), 1.0)
    c = nrm(ks[1], (BATCH, D_MODEL), 1.0)
    ctx = nrm(ks[2], (BATCH, CTX_LEN, D_MODEL), 1.0)
    c_ctx = nrm(ks[3], (D_MODEL,), 1.0)
    w_ada = nrm(ks[4], (DEPTH, D_MODEL, N_MOD * D_MODEL), 0.5 * D_MODEL ** -0.5)
    b_ada = nrm(ks[5], (DEPTH, N_MOD * D_MODEL), 0.02)
    norm_g = 1.0 + nrm(ks[6], (DEPTH, 3, D_MODEL), 0.1)
    ffn_up = nrm(ks[7], (DEPTH, 2, D_MODEL, 2 * D_FF), D_MODEL ** -0.5)
    ffn_down = nrm(ks[8], (DEPTH, 2, D_FF, D_MODEL), D_FF ** -0.5)
    w_in = nrm(ks[9], (DEPTH, D_MODEL, IN_COLS), D_MODEL ** -0.5)
    expo = -5.0 - jnp.arange(RET_HEADS, dtype=f32) + nrm(ks[10], (DEPTH, 2, RET_HEADS), 0.1)
    ret_log_gamma = jnp.log1p(-jnp.exp2(expo))
    ret_gn_g = 1.0 + nrm(ks[11], (DEPTH, RET_V), 0.1)
    ret_gn_b = nrm(ks[12], (DEPTH, RET_V), 0.02)
    w_ret_o = nrm(ks[13], (DEPTH, RET_V, D_MODEL), RET_V ** -0.5)
    hy_conv_w = nrm(ks[14], (DEPTH, 3, 3 * HY_WIDTH), 3.0 ** -0.5)
    hy_conv_b = nrm(ks[15], (DEPTH, 3 * HY_WIDTH), 0.02)
    hy_ff_w1 = nrm(ks[16], (DEPTH, HY_EMB, HY_FF), HY_EMB ** -0.5)
    hy_ff_b1 = nrm(ks[17], (DEPTH, HY_FF), 0.02)
    hy_ff_w2 = nrm(ks[18], (DEPTH, HY_FF, HY_FF), HY_FF ** -0.5)
    hy_ff_b2 = nrm(ks[19], (DEPTH, HY_FF), 0.02)
    hy_ff_w3 = nrm(ks[20], (DEPTH, HY_FF, HY_FF), HY_FF ** -0.5)
    hy_ff_b3 = nrm(ks[21], (DEPTH, HY_FF), 0.02)
    hy_ff_w4 = nrm(ks[22], (DEPTH, HY_FF, 2 * HY_WIDTH), HY_FF ** -0.5)
    hy_sin_freq = 1.0 + nrm(ks[23], (DEPTH, HY_FF), 0.1)
    hy_decay = (jnp.linspace(HY_MIN_DECAY, HY_MAX_DECAY, HY_WIDTH, dtype=f32)[None, :]
                + nrm(ks[24], (DEPTH, HY_WIDTH), 0.1))
    hy_bias = nrm(ks[25], (DEPTH, HY_WIDTH), 1.0)
    w_hy_o = nrm(ks[26], (DEPTH, HY_WIDTH, D_MODEL), HY_WIDTH ** -0.5)
    w_out = nrm(ks[27], (DEPTH, D_MODEL, D_MODEL), D_MODEL ** -0.5)
    final_norm_g = 1.0 + nrm(ks[28], (D_MODEL,), 0.1)
    return {'x': x, 'c': c, 'ctx': ctx, 'c_ctx': c_ctx, 'w_ada': w_ada, 'b_ada': b_ada,
            'norm_g': norm_g, 'ffn_up': ffn_up, 'ffn_down': ffn_down, 'w_in': w_in,
            'ret_log_gamma': ret_log_gamma, 'ret_gn_g': ret_gn_g, 'ret_gn_b': ret_gn_b,
            'w_ret_o': w_ret_o, 'hy_conv_w': hy_conv_w, 'hy_conv_b': hy_conv_b,
            'hy_ff_w1': hy_ff_w1, 'hy_ff_b1': hy_ff_b1, 'hy_ff_w2': hy_ff_w2, 'hy_ff_b2': hy_ff_b2,
            'hy_ff_w3': hy_ff_w3, 'hy_ff_b3': hy_ff_b3, 'hy_ff_w4': hy_ff_w4,
            'hy_sin_freq': hy_sin_freq, 'hy_decay': hy_decay, 'hy_bias': hy_bias,
            'w_hy_o': w_hy_o, 'w_out': w_out, 'final_norm_g': final_norm_g}


def reference(x, c, ctx, c_ctx, w_ada, b_ada, norm_g, ffn_up, ffn_down, w_in,
              ret_log_gamma, ret_gn_g, ret_gn_b, w_ret_o, hy_conv_w, hy_conv_b,
              hy_ff_w1, hy_ff_b1, hy_ff_w2, hy_ff_b2, hy_ff_w3, hy_ff_b3, hy_ff_w4,
              hy_sin_freq, hy_decay, hy_bias, w_hy_o, w_out, final_norm_g):
    seq_len = x.shape[1]
    rows = seq_len // GRID_W
    row = jnp.repeat(jnp.arange(rows, dtype=jnp.int32), GRID_W)
    col = jnp.tile(jnp.arange(GRID_W, dtype=jnp.int32), rows)

    for l in range(DEPTH):
        last = l == DEPTH - 1
        hy = (hy_conv_w[l], hy_conv_b[l], hy_ff_w1[l], hy_ff_b1[l], hy_ff_w2[l], hy_ff_b2[l],
              hy_ff_w3[l], hy_ff_b3[l], hy_ff_w4[l], hy_sin_freq[l], hy_decay[l], hy_bias[l],
              w_hy_o[l])
        mx = adaln_mods(c, w_ada[l], b_ada[l])
        mc = adaln_mods(c_ctx, w_ada[l], b_ada[l])

        x = ffn_half(x, mx, 0, norm_g[l, 0], ffn_up[l, 0], ffn_down[l, 0])
        ctx = ffn_half(ctx, mc, 0, norm_g[l, 0], ffn_up[l, 0], ffn_down[l, 0])

        hx = modulate(rmsnorm(x, norm_g[l, 1]), mx[:, :, 3], mx[:, :, 4])
        hc = modulate(rmsnorm(ctx, norm_g[l, 1]), mc[:, :, 3], mc[:, :, 4])
        if last:
            st_f, st_b = context_states(hc @ w_in[l][:, K_OFF:G_OFF], ret_log_gamma[l])
        else:
            pc = hc @ w_in[l]
            zero = jnp.zeros((ctx.shape[0], RET_HEADS, RET_DK, RET_DV), jnp.float32)
            ret_c, st_f, st_b = retention_mix(pc, None, None, ret_log_gamma[l], zero, zero,
                                              ret_gn_g[l], ret_gn_b[l], w_ret_o[l])
            hy_c = hyena_mix(pc, *hy)
            ctx = ctx + mc[:, :, 5] * merge_branches(pc, ret_c, hy_c, w_out[l])
            ctx = ffn_half(ctx, mc, 2, norm_g[l, 2], ffn_up[l, 1], ffn_down[l, 1])
        px = hx @ w_in[l]
        ret_x, _, _ = retention_mix(px, row, col, ret_log_gamma[l], st_f, st_b,
                                    ret_gn_g[l], ret_gn_b[l], w_ret_o[l])
        hy_x = hyena_mix(px, *hy)
        x = x + mx[:, :, 5] * merge_branches(px, ret_x, hy_x, w_out[l])

        x = ffn_half(x, mx, 2, norm_g[l, 2], ffn_up[l, 1], ffn_down[l, 1])

    return rmsnorm(x, final_norm_g)
```

```python
import functools
import math

import jax
import jax.numpy as jnp
from jax import lax
from jax.experimental import pallas as pl
from jax.experimental.pallas import tpu as pltpu

F32 = jnp.float32
BF16 = jnp.bfloat16

N_MOD = 9
GRID_W = 64
RET_HEADS = 8
RET_DK = 128
RET_DV = 256
RET_CHUNK = 128
RET_Q = RET_HEADS * RET_DK
RET_V = RET_HEADS * RET_DV
HY_BANDS = 16
ROPE_BASE = 10000.0
EPS = 1e-6

DFT_N1 = 128
VMEM_LIMIT = 56 * 1024 * 1024


def _cparams(*sem):
    return pltpu.CompilerParams(dimension_semantics=sem, vmem_limit_bytes=VMEM_LIMIT)


def _dot(a, b):
    return jnp.dot(a, b, preferred_element_type=F32)


def _dot_t0(a, b):
    return lax.dot_general(a, b, (((0,), (0,)), ((), ())), preferred_element_type=F32)


def _silu(v):
    return v * jax.nn.sigmoid(v)


def _norm_mod(x, g, shift, scale):
    y = x * lax.rsqrt(jnp.mean(x * x, axis=-1, keepdims=True) + EPS) * g
    return y * (1.0 + scale) + shift


def _adaln_kernel(c_ref, w_ref, b_ref, o_ref):
    s = _silu(c_ref[...]).astype(BF16)
    o_ref[...] = _dot(s, w_ref[...].astype(BF16)) + b_ref[...]


def _adaln(cv, w, b, tn=1024):
    rows, d = cv.shape
    n = w.shape[1]
    return pl.pallas_call(
        _adaln_kernel,
        grid=(n // tn,),
        in_specs=[pl.BlockSpec((rows, d), lambda j: (0, 0)),
                  pl.BlockSpec((d, tn), lambda j: (0, j)),
                  pl.BlockSpec((1, tn), lambda j: (0, j))],
        out_specs=pl.BlockSpec((rows, tn), lambda j: (0, j)),
        out_shape=jax.ShapeDtypeStruct((rows, n), F32),
        compiler_params=_cparams("arbitrary"),
        name="adaln",
    )(cv, w, b.reshape(1, n))


def _ffn_kernel(x_ref, mod_ref, g_ref, wa_ref, wg_ref, wd_ref, *rest, final):
    if final:
        fg_ref, o_ref, h_scr, acc_scr = rest
    else:
        o_ref, h_scr, acc_scr = rest
    j = pl.program_id(1)

    @pl.when(j == 0)
    def _():
        h = _norm_mod(x_ref[...], g_ref[...], mod_ref[0:1, :], mod_ref[1:2, :])
        h_scr[...] = h.astype(BF16)
        acc_scr[...] = jnp.zeros_like(acc_scr)

    h = h_scr[...]
    a = _dot(h, wa_ref[...])
    g = _dot(h, wg_ref[...])
    acc_scr[...] += _dot((_silu(a) * g).astype(BF16), wd_ref[...])

    @pl.when(j == pl.num_programs(1) - 1)
    def _():
        out = x_ref[...] + (0.5 * mod_ref[2:3, :]) * acc_scr[...]
        if final:
            out = out * lax.rsqrt(jnp.mean(out * out, axis=-1, keepdims=True) + EPS) * fg_ref[...]
        o_ref[...] = out


def _ffn_half(x, mod, g, w_up, w_down, final_g=None, tm=512, tf=512):
    t, d = x.shape
    dff = w_down.shape[0]
    tm = min(tm, t)
    nf = dff // tf
    final = final_g is not None
    in_specs = [pl.BlockSpec((tm, d), lambda i, j: (i, 0)),
                pl.BlockSpec((8, d), lambda i, j: (0, 0)),
                pl.BlockSpec((1, d), lambda i, j: (0, 0)),
                pl.BlockSpec((d, tf), lambda i, j: (0, j)),
                pl.BlockSpec((d, tf), lambda i, j: (0, nf + j)),
                pl.BlockSpec((tf, d), lambda i, j: (j, 0))]
    args = [x, mod, g.reshape(1, d), w_up, w_up, w_down]
    if final:
        in_specs.append(pl.BlockSpec((1, d), lambda i, j: (0, 0)))
        args.append(final_g.reshape(1, d))
    return pl.pallas_call(
        functools.partial(_ffn_kernel, final=final),
        grid=(t // tm, nf),
        in_specs=in_specs,
        out_specs=pl.BlockSpec((tm, d), lambda i, j: (i, 0)),
        out_shape=jax.ShapeDtypeStruct((t, d), F32),
        scratch_shapes=[pltpu.VMEM((tm, d), BF16), pltpu.VMEM((tm, d), F32)],
        compiler_params=_cparams("parallel", "arbitrary"),
        name="ffn_final" if final else "ffn_half",
    )(*args)


def _inproj_kernel(x_ref, mod_ref, g_ref, w_ref, o_ref, h_scr):
    @pl.when(pl.program_id(1) == 0)
    def _():
        h = _norm_mod(x_ref[...], g_ref[...], mod_ref[0:1, :], mod_ref[1:2, :])
        h_scr[...] = h.astype(BF16)

    o_ref[...] = _dot(h_scr[...], w_ref[...]).astype(BF16)


def _in_proj(x, mod, g, w, col0, ncols, tm=1024, tn=1024):
    t, d = x.shape
    tm = min(tm, t)
    off = col0 // tn
    return pl.pallas_call(
        _inproj_kernel,
        grid=(t // tm, ncols // tn),
        in_specs=[pl.BlockSpec((tm, d), lambda i, j: (i, 0)),
                  pl.BlockSpec((8, d), lambda i, j: (0, 0)),
                  pl.BlockSpec((1, d), lambda i, j: (0, 0)),
                  pl.BlockSpec((d, tn), lambda i, j: (0, j + off))],
        out_specs=pl.BlockSpec((tm, tn), lambda i, j: (i, j)),
        out_shape=jax.ShapeDtypeStruct((t, ncols), BF16),
        scratch_shapes=[pltpu.VMEM((tm, d), BF16)],
        compiler_params=_cparams("parallel", "arbitrary"),
        name="in_proj",
    )(x, mod, g.reshape(1, d), w)


def _ctx_state_kernel(lg_ref, k_ref, v_ref, o_ref):
    h = pl.program_id(0)
    lc = k_ref.shape[0]
    m = lax.broadcasted_iota(jnp.int32, (lc, 1), 0).astype(F32)
    k = k_ref[...].astype(F32)
    v = v_ref[...]
    w_f = jnp.exp(lg_ref[0, h] * (lc - 1.0 - m))
    w_b = jnp.exp(lg_ref[1, h] * m)
    o_ref[0, 0] = _dot_t0((k * w_f).astype(BF16), v)
    o_ref[1, 0] = _dot_t0((k * w_b).astype(BF16), v)


def _ctx_states(pkv, lg):
    lc = pkv.shape[0]
    return pl.pallas_call(
        _ctx_state_kernel,
        grid=(RET_HEADS,),
        in_specs=[pl.BlockSpec(memory_space=pltpu.SMEM),
                  pl.BlockSpec((lc, RET_DK), lambda h: (0, h)),
                  pl.BlockSpec((lc, RET_DV), lambda h: (0, RET_Q // RET_DV + h))],
        out_specs=pl.BlockSpec((2, 1, RET_DK, RET_DV), lambda h: (0, h, 0, 0)),
        out_shape=jax.ShapeDtypeStruct((2, RET_HEADS, RET_DK, RET_DV), F32),
        compiler_params=_cparams("arbitrary"),
        name="ctx_states",
    )(lg, pkv, pkv)


def _rope(x, cos, sin):
    lane = lax.broadcasted_iota(jnp.int32, x.shape, 1)
    partner = jnp.where((lane % 64) < 32, pltpu.roll(x, 96, 1), pltpu.roll(x, 32, 1))
    return x * cos + partner * sin


def _ret_kernel(lg_ref, q_ref, k_ref, v_ref, g_ref, cos_ref, sin_ref, st0_ref, gng_ref, gnb_ref,
                o_ref, sb_all, sf_scr, sb_scr, *, nb, bc):
    c_len = RET_CHUNK
    h = pl.program_id(0)
    i = pl.program_id(1)
    lgf = lg_ref[0, h]
    lgb = lg_ref[1, h]
    pos = lax.broadcasted_iota(jnp.int32, (c_len, 1), 0).astype(F32)
    ones_row = jnp.ones((1, RET_DV), F32)

    @pl.when(i == 0)
    def _():
        sf_scr[...] = st0_ref[0, 0]
        sb_scr[...] = st0_ref[1, 0]

    @pl.when(i < nb)
    def _():
        rb = nb - 1 - i
        zeta_b = jnp.exp(lgb * pos)
        cd_b = jnp.exp((lgb * c_len) * ones_row)
        for c in reversed(range(bc)):
            rows = slice(c * c_len, (c + 1) * c_len)
            k = _rope(k_ref[rows, :].astype(F32), cos_ref[rows, :], sin_ref[rows, :])
            sb_all[rb * bc + c] = sb_scr[...].astype(BF16)
            sb_scr[...] = sb_scr[...] * cd_b + _dot_t0((k * zeta_b).astype(BF16), v_ref[rows, :])

    @pl.when(i >= nb)
    def _():
        rb = i - nb
        a = lax.broadcasted_iota(jnp.int32, (c_len, c_len), 0)
        b = lax.broadcasted_iota(jnp.int32, (c_len, c_len), 1)
        d = (a - b).astype(F32)
        decay = jnp.where(d > 0, jnp.exp(lgf * jnp.maximum(d, 0.0)),
                          jnp.where(d < 0, jnp.exp(lgb * jnp.maximum(-d, 0.0)), 2.0))
        xi_f = jnp.exp(lgf * (pos + 1.0))
        xi_b = jnp.exp(lgb * (c_len - pos))
        zeta_f = jnp.exp(lgf * (c_len - 1.0 - pos))
        cd_f = jnp.exp((lgf * c_len) * ones_row)
        scale = RET_DK ** -0.5
        for c in range(bc):
            rows = slice(c * c_len, (c + 1) * c_len)
            cos = cos_ref[rows, :]
            sin = sin_ref[rows, :]
            q = _rope(q_ref[rows, :].astype(F32), cos, sin) * scale
            k = _rope(k_ref[rows, :].astype(F32), cos, sin)
            v = v_ref[rows, :]
            s = lax.dot_general(q.astype(BF16), k.astype(BF16), (((1,), (1,)), ((), ())),
                                preferred_element_type=F32) * decay
            y = (_dot(s.astype(BF16), v)
                 + _dot((q * xi_f).astype(BF16), sf_scr[...].astype(BF16))
                 + _dot((q * xi_b).astype(BF16), sb_all[rb * bc + c]))
            sf_scr[...] = sf_scr[...] * cd_f + _dot_t0((k * zeta_f).astype(BF16), v)
            mu = jnp.mean(y, axis=-1, keepdims=True)
            yc = y - mu
            var = jnp.mean(yc * yc, axis=-1, keepdims=True)
            yn = yc * lax.rsqrt(var + EPS) * gng_ref[...] + gnb_ref[...]
            o_ref[rows, :] = (_silu(g_ref[rows, :].astype(F32)) * yn).astype(BF16)


def _retention(p, cos_t, sin_t, st0, lg, gn_g, gn_b, q_off, k_off, v_off, g_off, bc=4):
    t = p.shape[0]
    rows = bc * RET_CHUNK
    nb = t // rows

    def kv_rb(i):
        return jnp.where(i < nb, nb - 1 - i, i - nb)

    def q_rb(i):
        return jnp.maximum(i - nb, 0)

    return pl.pallas_call(
        functools.partial(_ret_kernel, nb=nb, bc=bc),
        grid=(RET_HEADS, 2 * nb),
        in_specs=[pl.BlockSpec(memory_space=pltpu.SMEM),
                  pl.BlockSpec((rows, RET_DK), lambda h, i: (q_rb(i), q_off // RET_DK + h)),
                  pl.BlockSpec((rows, RET_DK), lambda h, i: (kv_rb(i), k_off // RET_DK + h)),
                  pl.BlockSpec((rows, RET_DV), lambda h, i: (kv_rb(i), v_off // RET_DV + h)),
                  pl.BlockSpec((rows, RET_DV), lambda h, i: (q_rb(i), g_off // RET_DV + h)),
                  pl.BlockSpec((rows, RET_DK), lambda h, i: (kv_rb(i), 0)),
                  pl.BlockSpec((rows, RET_DK), lambda h, i: (kv_rb(i), 0)),
                  pl.BlockSpec((2, 1, RET_DK, RET_DV), lambda h, i: (0, h, 0, 0)),
                  pl.BlockSpec((1, RET_DV), lambda h, i: (0, h)),
                  pl.BlockSpec((1, RET_DV), lambda h, i: (0, h))],
        out_specs=pl.BlockSpec((rows, RET_DV), lambda h, i: (q_rb(i), h)),
        out_shape=jax.ShapeDtypeStruct((t, RET_V), BF16),
        scratch_shapes=[pltpu.VMEM((t // RET_CHUNK, RET_DK, RET_DV), BF16),
                        pltpu.VMEM((RET_DK, RET_DV), F32),
                        pltpu.VMEM((RET_DK, RET_DV), F32)],
        compiler_params=_cparams("arbitrary", "arbitrary"),
        name="retention",
    )(lg, p, p, p, p, cos_t, sin_t, st0, gn_g.reshape(1, RET_V), gn_b.reshape(1, RET_V))


def _rope_tables(t):
    half = RET_DK // 2
    pos = jnp.arange(t, dtype=jnp.int32)
    inv = ROPE_BASE ** (-jnp.arange(0, half, 2, dtype=F32) / half)
    ang_r = (pos // GRID_W).astype(F32)[:, None] * inv[None, :]
    ang_c = (pos % GRID_W).astype(F32)[:, None] * inv[None, :]
    cos_t = jnp.concatenate([jnp.cos(ang_r)] * 2 + [jnp.cos(ang_c)] * 2, axis=-1)
    sin_t = jnp.concatenate([-jnp.sin(ang_r), jnp.sin(ang_r), -jnp.sin(ang_c), jnp.sin(ang_c)], axis=-1)
    return cos_t, sin_t


def _hy_pre_kernel(m0, m1, m2, p0, p1, p2, n0, n1, n2, w_ref, b_ref, x0_ref, z_ref):
    i = pl.program_id(0)
    tm, ch = m0.shape
    row = lax.broadcasted_iota(jnp.int32, (tm, 1), 0)
    has_prev = (i > 0).astype(F32)
    has_next = (i < pl.num_programs(0) - 1).astype(F32)
    halo = p0.shape[0]

    def conv(main, prev, nxt, part):
        cols = slice(part * ch, (part + 1) * ch)
        u = main[...].astype(F32)
        before = jnp.where(row == 0, prev[halo - 1:halo, :].astype(F32) * has_prev, pltpu.roll(u, 1, 0))
        after = jnp.where(row == tm - 1, nxt[0:1, :].astype(F32) * has_next, pltpu.roll(u, tm - 1, 0))
        return (before * w_ref[0:1, cols] + u * w_ref[1:2, cols] + after * w_ref[2:3, cols]
                + b_ref[0:1, cols])

    x0_ref[...] = conv(m0, p0, n0, 0).astype(BF16)
    z_ref[...] = (conv(m1, p1, n1, 1) * conv(m2, p2, n2, 2)).astype(BF16)


def _hy_pre(p, conv_w, conv_b, hy_off, ch, tm=256, halo=16):
    t = p.shape[0]
    tm = min(tm, t)
    nh = tm // halo
    last = t // halo - 1
    cb0 = hy_off // ch

    def main(part):
        return pl.BlockSpec((tm, ch), lambda i: (i, cb0 + part))

    def prev(part):
        return pl.BlockSpec((halo, ch), lambda i: (jnp.maximum(i * nh - 1, 0), cb0 + part))

    def nxt(part):
        return pl.BlockSpec((halo, ch), lambda i: (jnp.minimum((i + 1) * nh, last), cb0 + part))

    out_spec = pl.BlockSpec((tm, ch), lambda i: (i, 0))
    return pl.pallas_call(
        _hy_pre_kernel,
        grid=(t // tm,),
        in_specs=[main(0), main(1), main(2), prev(0), prev(1), prev(2), nxt(0), nxt(1), nxt(2),
                  pl.BlockSpec((3, 3 * ch), lambda i: (0, 0)),
                  pl.BlockSpec((1, 3 * ch), lambda i: (0, 0))],
        out_specs=[out_spec, out_spec],
        out_shape=[jax.ShapeDtypeStruct((t, ch), BF16)] * 2,
        compiler_params=_cparams("parallel"),
        name="hy_conv3",
    )(p, p, p, p, p, p, p, p, p, conv_w, conv_b.reshape(1, 3 * ch))


def _filt_ffn_kernel(f_ref, w1, b1, w2, b2, w3, b3, fr, o_ref):
    hp = lax.Precision.HIGHEST
    freq = fr[:, 0:1]
    h = jnp.sin(freq * (jnp.dot(w1[...], f_ref[...], precision=hp, preferred_element_type=F32) + b1[:, 0:1]))
    h = jnp.sin(freq * (jnp.dot(w2[...], h, precision=hp, preferred_element_type=F32) + b2[:, 0:1]))
    o_ref[...] = jnp.sin(freq * (jnp.dot(w3[...], h, precision=hp, preferred_element_type=F32) + b3[:, 0:1]))


def _filter_ffn(feat_t, w1, b1, w2, b2, w3, b3, freq, tl=2048):
    e, length = feat_t.shape
    ff = w1.shape[1]
    tl = min(tl, length)

    def col(v):
        return jnp.broadcast_to(v.reshape(ff, 1), (ff, 128))

    full = lambda shape: pl.BlockSpec(shape, lambda i: (0, 0))
    return pl.pallas_call(
        _filt_ffn_kernel,
        grid=(length // tl,),
        in_specs=[pl.BlockSpec((e, tl), lambda i: (0, i)),
                  full((ff, e)), full((ff, 128)), full((ff, ff)), full((ff, 128)),
                  full((ff, ff)), full((ff, 128)), full((ff, 128))],
        out_specs=pl.BlockSpec((ff, tl), lambda i: (0, i)),
        out_shape=jax.ShapeDtypeStruct((ff, length), F32),
        compiler_params=_cparams("parallel"),
        name="filter_ffn",
    )(feat_t, w1.T, col(b1), w2.T, col(b2), w3.T, col(b3), col(freq))


def _filt_taps_kernel(h_ref, t_ref, w4_ref, dec_ref, hp_ref, ssq_ref, *, fwd_tiles):
    i = pl.program_id(0)
    taps = _dot(h_ref[...].astype(BF16), w4_ref[...])
    window = jnp.exp(-t_ref[:, 0:1] * jnp.abs(dec_ref[...]))
    sign = jnp.where(i < fwd_tiles, 1.0, -1.0)
    hrow = taps * window * sign

    @pl.when(i == 0)
    def _():
        ssq_ref[...] = jnp.zeros_like(ssq_ref)

    ssq_ref[...] += jnp.sum(hrow * hrow, axis=0, keepdims=True)
    hp_ref[...] = hrow.astype(BF16)


def _filter_taps(h_ext, t_ext, w4, decay, tr=512):
    n, ff = h_ext.shape
    ch = decay.shape[0]
    tr = min(tr, n // 2)
    fwd_tiles = (n // 2) // tr
    return pl.pallas_call(
        functools.partial(_filt_taps_kernel, fwd_tiles=fwd_tiles),
        grid=(n // tr,),
        in_specs=[pl.BlockSpec((tr, ff), lambda i: (i, 0)),
                  pl.BlockSpec((tr, 1), lambda i: (i, 0)),
                  pl.BlockSpec((ff, ch), lambda i: (0, (i >= fwd_tiles).astype(jnp.int32))),
                  pl.BlockSpec((1, ch), lambda i: (0, 0))],
        out_specs=[pl.BlockSpec((tr, ch), lambda i: (i, 0)),
                   pl.BlockSpec((1, ch), lambda i: (0, 0))],
        out_shape=[jax.ShapeDtypeStruct((n, ch), BF16), jax.ShapeDtypeStruct((1, ch), F32)],
        compiler_params=_cparams("arbitrary"),
        name="filter_taps",
    )(h_ext, t_ext, w4, decay.reshape(1, ch))


def _dft_a_kernel(x_ref, e_ref, br_ref, bi_ref):
    r = _dot(e_ref[...], x_ref[...])
    n = r.shape[0] // 2
    br_ref[...] = r[:n].astype(BF16)
    bi_ref[...] = r[n:].astype(BF16)


def _dft_stage_a(x, e_tab):
    n1, m2, k2 = e_tab.shape
    c = x.shape[1]
    out = jax.ShapeDtypeStruct((m2 // 2, n1 * c), BF16)
    out_spec = pl.BlockSpec((m2 // 2, c), lambda i: (0, i))
    b_r, b_i = pl.pallas_call(
        _dft_a_kernel,
        grid=(n1,),
        in_specs=[pl.BlockSpec((k2, c), lambda i: (0, i)),
                  pl.BlockSpec((None, m2, k2), lambda i: (i, 0, 0))],
        out_specs=[out_spec, out_spec],
        out_shape=[out, out],
        compiler_params=_cparams("parallel"),
        name="dft_stage_a",
    )(x.reshape(k2, n1 * c), e_tab)
    return b_r.reshape(m2 // 2, n1, c), b_i.reshape(m2 // 2, n1, c)


def _dft_mid_kernel(zr, zi, hr, hi, f_ref, finv_ref, cr_ref, ci_ref):
    n = zr.shape[0]
    xs = _dot(f_ref[...], jnp.concatenate([zr[...], zi[...]], axis=0))
    ks = _dot(f_ref[...], jnp.concatenate([hr[...], hi[...]], axis=0))
    xr, xi, kr, ki = xs[:n], xs[n:], ks[:n], ks[n:]
    y = jnp.concatenate([xr * kr - xi * ki, xr * ki + xi * kr], axis=0).astype(BF16)
    cs = _dot(finv_ref[...], y)
    cr_ref[...] = cs[:n].astype(BF16)
    ci_ref[...] = cs[n:].astype(BF16)


def _dft_mid(bzr, bzi, bhr, bhi, f_fwd, f_inv):
    n2h, n1, c = bzr.shape
    blk = pl.BlockSpec((None, n1, c), lambda i: (i, 0, 0))
    mat = pl.BlockSpec((2 * n1, 2 * n1), lambda i: (0, 0))
    out = jax.ShapeDtypeStruct((n2h, n1, c), BF16)
    return pl.pallas_call(
        _dft_mid_kernel,
        grid=(n2h,),
        in_specs=[blk, blk, blk, blk, mat, mat],
        out_specs=[blk, blk],
        out_shape=[out, out],
        compiler_params=_cparams("parallel"),
        name="dft_mid",
    )(bzr, bzi, bhr, bhi, f_fwd, f_inv)


def _dft_ainv_kernel(cr, ci, e_ref, x0_ref, z_ref, skip_ref, sc_ref, o_ref):
    y = _dot(e_ref[...], jnp.concatenate([cr[...], ci[...]], axis=0)) * sc_ref[...]
    zz = y + z_ref[...].astype(F32) * skip_ref[...]
    o_ref[...] = (x0_ref[...].astype(F32) * zz).astype(BF16)


def _dft_stage_a_inv(c_r, c_i, e_inv, x0, z, skip, scale):
    n2h, n1, c = c_r.shape
    t = x0.shape[0]
    cblk = pl.BlockSpec((n2h, c), lambda i: (0, i))
    row = pl.BlockSpec((1, c), lambda i: (0, 0))
    out = pl.pallas_call(
        _dft_ainv_kernel,
        grid=(n1,),
        in_specs=[cblk, cblk, pl.BlockSpec((None, n2h, 2 * n2h), lambda i: (i, 0, 0)), cblk, cblk, row, row],
        out_specs=cblk,
        out_shape=jax.ShapeDtypeStruct((n2h, n1 * c), BF16),
        compiler_params=_cparams("parallel"),
        name="dft_stage_a_inv",
    )(c_r.reshape(n2h, n1 * c), c_i.reshape(n2h, n1 * c), e_inv, x0.reshape(n2h, n1 * c),
      z.reshape(n2h, n1 * c), skip.reshape(1, c), scale)
    return out.reshape(t, c)


def _dft_tables(n1, n2h):
    n2 = 2 * n2h
    n = n1 * n2
    a = jnp.arange(n1, dtype=jnp.int32)
    b = jnp.arange(n2, dtype=jnp.int32)
    odd = 2 * jnp.arange(n2h, dtype=jnp.int32) + 1
    alpha = ((a[:, None] * odd[None, :]) % (2 * n)).astype(F32) * (math.pi / n)
    beta = ((odd[:, None] * b[None, :]) % (2 * n2)).astype(F32) * (math.pi / n2)
    ar, ai = jnp.cos(alpha)[:, :, None], -jnp.sin(alpha)[:, :, None]
    br, bi = jnp.cos(beta)[None], -jnp.sin(beta)[None]
    er = ar * br - ai * bi
    ei = ar * bi + ai * br
    e_full = jnp.concatenate([er, ei], axis=1).astype(BF16)
    e_half = e_full[:, :, :n2h]
    e_inv = jnp.concatenate([jnp.swapaxes(er[:, :, :n2h], 1, 2), jnp.swapaxes(ei[:, :, :n2h], 1, 2)],
                            axis=2).astype(BF16)
    phi = ((a[:, None] * a[None, :]) % n1).astype(F32) * (2.0 * math.pi / n1)
    fr, fi = jnp.cos(phi), -jnp.sin(phi)
    f_fwd = jnp.concatenate([jnp.concatenate([fr, -fi], axis=1),
                             jnp.concatenate([fi, fr], axis=1)], axis=0).astype(BF16)
    f_inv = jnp.concatenate([jnp.concatenate([fr, fi], axis=1),
                             jnp.concatenate([-fi, fr], axis=1)], axis=0).astype(BF16)
    return e_half, e_full, e_inv, f_fwd, f_inv


def _hyena_features(length):
    t = jnp.linspace(0.0, 1.0, length, dtype=F32)[:, None]
    w = 2.0 * math.pi * jnp.arange(length, dtype=F32)[:, None] / length
    f = jnp.linspace(1e-4, HY_BANDS - 1.0, HY_BANDS, dtype=F32)[None, :]
    return t, jnp.concatenate([t, jnp.cos(f * w), -jnp.sin(f * w)], axis=-1)


def _merge_kernel(x_ref, yr_ref, hy_ref, gr_ref, gh_ref, gate_ref, wr_ref, wh_ref, wo_ref, o_ref):
    ret_out = _dot(yr_ref[...], wr_ref[...])
    hy_out = _dot(hy_ref[...], wh_ref[...])
    m = (jax.nn.sigmoid(gr_ref[...].astype(F32)) * ret_out
         + jax.nn.sigmoid(gh_ref[...].astype(F32)) * hy_out)
    o_ref[...] = x_ref[...] + gate_ref[...] * _dot(m.astype(BF16), wo_ref[...])


def _merge(x, y_ret, hy_in, p, gate_off, gate, w_ret_o, w_hy_o, w_out, tm=256):
    t, d = x.shape
    tm = min(tm, t)
    gb = gate_off // d
    tile = lambda: pl.BlockSpec((tm, d), lambda i: (i, 0))
    weight = lambda w: pl.BlockSpec(w.shape, lambda i: (0, 0), pipeline_mode=pl.Buffered(1))
    return pl.pallas_call(
        _merge_kernel,
        grid=(t // tm,),
        in_specs=[tile(), tile(), tile(),
                  pl.BlockSpec((tm, d), lambda i: (i, gb)),
                  pl.BlockSpec((tm, d), lambda i: (i, gb + 1)),
                  pl.BlockSpec((1, d), lambda i: (0, 0)),
                  weight(w_ret_o), weight(w_hy_o), weight(w_out)],
        out_specs=tile(),
        out_shape=jax.ShapeDtypeStruct((t, d), F32),
        compiler_params=_cparams("parallel"),
        name="merge_out",
    )(x, y_ret, hy_in, p, p, gate.reshape(1, d), w_ret_o, w_hy_o, w_out)


def _mod_rows(mods, first):
    return jnp.pad(mods[first:first + 3], ((0, 5), (0, 0)))


def kernel(x, c, ctx, c_ctx, w_ada, b_ada, norm_g, ffn_up, ffn_down, w_in, ret_log_gamma, ret_gn_g,
           ret_gn_b, w_ret_o, hy_conv_w, hy_conv_b, hy_ff_w1, hy_ff_b1, hy_ff_w2, hy_ff_b2, hy_ff_w3,
           hy_ff_b3, hy_ff_w4, hy_sin_freq, hy_decay, hy_bias, w_hy_o, w_out, final_norm_g):
    batch, t, d = x.shape
    assert batch == 1 and w_ada.shape[0] == 1, "single sample, single layer"
    hy_w = hy_decay.shape[1]
    q_off, k_off = 0, RET_Q
    v_off = k_off + RET_Q
    g_off = v_off + RET_V
    hy_off = g_off + RET_V
    gate_off = hy_off + 3 * hy_w
    assert w_in.shape[2] == gate_off + 2 * d and hy_w == d

    xs, cs = x[0], ctx[0]
    up = ffn_up[0].astype(BF16)
    down = ffn_down[0].astype(BF16)
    w_in_b = w_in[0].astype(BF16)
    lg = ret_log_gamma[0].astype(F32)

    cv = jnp.concatenate([c, c_ctx[None, :], jnp.zeros((6, d), F32)], axis=0)
    mods = _adaln(cv, w_ada[0], b_ada[0]).reshape(8, N_MOD, d)
    mx, mc = mods[0], mods[1]

    xs = _ffn_half(xs, _mod_rows(mx, 0), norm_g[0, 0], up[0], down[0])
    cs = _ffn_half(cs, _mod_rows(mc, 0), norm_g[0, 0], up[0], down[0])

    pkv = _in_proj(cs, _mod_rows(mc, 3), norm_g[0, 1], w_in_b, k_off, g_off - k_off)
    st0 = _ctx_states(pkv, lg)

    p = _in_proj(xs, _mod_rows(mx, 3), norm_g[0, 1], w_in_b, 0, w_in_b.shape[1])

    cos_t, sin_t = _rope_tables(t)
    y_ret = _retention(p, cos_t, sin_t, st0, lg, ret_gn_g[0], ret_gn_b[0], q_off, k_off, v_off, g_off)

    x0c, z = _hy_pre(p, hy_conv_w[0], hy_conv_b[0], hy_off, hy_w)
    t_lin, feat = _hyena_features(t)
    e_pad = (-feat.shape[1]) % 8
    feat_t = jnp.pad(feat.T, ((0, e_pad), (0, 0)))
    w1 = jnp.pad(hy_ff_w1[0], ((0, e_pad), (0, 0)))
    h3 = _filter_ffn(feat_t, w1, hy_ff_b1[0], hy_ff_w2[0], hy_ff_b2[0], hy_ff_w3[0], hy_ff_b3[0],
                     hy_sin_freq[0]).T
    zero_row = jnp.zeros((1, h3.shape[1]), F32)
    h_ext = jnp.concatenate([h3, zero_row, jnp.flip(h3[1:], axis=0)], axis=0)
    t_ext = jnp.concatenate([t_lin, jnp.zeros((1, 1), F32), jnp.flip(t_lin[1:], axis=0)], axis=0)
    h_taps, ssq = _filter_taps(h_ext, t_ext, hy_ff_w4[0].astype(BF16), hy_decay[0])

    n1 = DFT_N1
    n2h = t // n1
    e_half, e_full, e_inv, f_fwd, f_inv = _dft_tables(n1, n2h)
    bzr, bzi = _dft_stage_a(z, e_half)
    bhr, bhi = _dft_stage_a(h_taps, e_full)
    c_r, c_i = _dft_mid(bzr, bzi, bhr, bhi, f_fwd, f_inv)
    scale = lax.rsqrt(ssq + EPS) * (2.0 / (2 * t))
    hy_in = _dft_stage_a_inv(c_r, c_i, e_inv, x0c, z, hy_bias[0], scale)

    xs = _merge(xs, y_ret, hy_in, p, gate_off, mx[5], w_ret_o[0].astype(BF16), w_hy_o[0].astype(BF16),
                w_out[0].astype(BF16))

    out = _ffn_half(xs, _mod_rows(mx, 6), norm_g[0, 2], up[1], down[1], final_g=final_norm_g)
    return out[None]
```

```python
import functools
import math

import jax
import jax.numpy as jnp
from jax import lax
from jax.experimental import pallas as pl
from jax.experimental.pallas import tpu as pltpu

F32 = jnp.float32
BF16 = jnp.bfloat16
U32 = jnp.uint32

N_MOD = 9
GRID_W = 64
RET_HEADS = 8
RET_DK = 128
RET_DV = 256
RET_CHUNK = 128
RET_Q = RET_HEADS * RET_DK
RET_V = RET_HEADS * RET_DV
HY_BANDS = 16
ROPE_BASE = 10000.0
EPS = 1e-6

DFT_N1 = 128
SUBLANES = 8
LANES = 128
DFT_SLABS = 4
VMEM_BYTES = 64 * 1024 * 1024
VMEM_LIMIT = VMEM_BYTES - 8 * 1024 * 1024
VMEM_LIMIT_FFN = VMEM_BYTES - 4 * 1024 * 1024


def _cparams(*sem, vmem=VMEM_LIMIT):
    return pltpu.CompilerParams(dimension_semantics=sem, vmem_limit_bytes=vmem)


def _dot(a, b):
    return jnp.dot(a, b, preferred_element_type=F32)


def _dot_t0(a, b):
    return lax.dot_general(a, b, (((0,), (0,)), ((), ())), preferred_element_type=F32)


def _silu(v):
    return v * jax.nn.sigmoid(v)


def _norm_mod(x, g, shift, scale):
    y = x * lax.rsqrt(jnp.mean(x * x, axis=-1, keepdims=True) + EPS) * g
    return y * (1.0 + scale) + shift


def _pack2(hi, lo):
    hb = lax.bitcast_convert_type(hi.astype(BF16).astype(F32), U32)
    lb = lax.bitcast_convert_type(lo.astype(BF16).astype(F32), U32)
    return hb | (lb >> 16)


def _unpack2(w):
    hi = lax.bitcast_convert_type(w & jnp.uint32(0xFFFF0000), F32)
    lo = lax.bitcast_convert_type(w << 16, F32)
    return hi, lo


def _adaln_kernel(c_ref, w_ref, b_ref, o_ref):
    s = _silu(c_ref[...]).astype(BF16)
    o_ref[...] = _dot(s, w_ref[...].astype(BF16)) + b_ref[...]


def _adaln(cv, w, b, tn=1024):
    rows, d = cv.shape
    n = w.shape[1]
    return pl.pallas_call(
        _adaln_kernel,
        grid=(n // tn,),
        in_specs=[pl.BlockSpec((rows, d), lambda j: (0, 0)),
                  pl.BlockSpec((d, tn), lambda j: (0, j)),
                  pl.BlockSpec((1, tn), lambda j: (0, j))],
        out_specs=pl.BlockSpec((rows, tn), lambda j: (0, j)),
        out_shape=jax.ShapeDtypeStruct((rows, n), F32),
        compiler_params=_cparams("arbitrary"),
        name="adaln",
    )(cv, w, b.reshape(1, n))


def _ffn_kernel(x_ref, mod_ref, g_ref, wa_ref, wg_ref, wd_ref, *rest, final):
    if final:
        fg_ref, o_ref, h_scr = rest
    else:
        o_ref, h_scr = rest
    j = pl.program_id(1)

    @pl.when(j == 0)
    def _():
        h = _norm_mod(x_ref[...], g_ref[...], mod_ref[0:1, :], mod_ref[1:2, :])
        h_scr[...] = h.astype(BF16)
        o_ref[...] = jnp.zeros_like(o_ref)

    h = h_scr[...]
    a = _dot(h, wa_ref[...])
    g = _dot(h, wg_ref[...])
    o_ref[...] += _dot((_silu(a) * g).astype(BF16), wd_ref[...])

    @pl.when(j == pl.num_programs(1) - 1)
    def _():
        out = x_ref[...] + (0.5 * mod_ref[2:3, :]) * o_ref[...]
        if final:
            out = out * lax.rsqrt(jnp.mean(out * out, axis=-1, keepdims=True) + EPS) * fg_ref[...]
        o_ref[...] = out


def _ffn_half(x, mod, g, w_up, w_down, final_g=None, tm=1024, tf=512):
    t, d = x.shape
    dff = w_down.shape[0]
    tm = min(tm, t)
    nf = dff // tf
    final = final_g is not None
    in_specs = [pl.BlockSpec((tm, d), lambda i, j: (i, 0), pipeline_mode=pl.Buffered(1)),
                pl.BlockSpec((8, d), lambda i, j: (0, 0)),
                pl.BlockSpec((1, d), lambda i, j: (0, 0)),
                pl.BlockSpec((d, tf), lambda i, j: (0, j)),
                pl.BlockSpec((d, tf), lambda i, j: (0, nf + j)),
                pl.BlockSpec((tf, d), lambda i, j: (j, 0))]
    args = [x, mod, g.reshape(1, d), w_up, w_up, w_down]
    if final:
        in_specs.append(pl.BlockSpec((1, d), lambda i, j: (0, 0)))
        args.append(final_g.reshape(1, d))
    return pl.pallas_call(
        functools.partial(_ffn_kernel, final=final),
        grid=(t // tm, nf),
        in_specs=in_specs,
        out_specs=pl.BlockSpec((tm, d), lambda i, j: (i, 0)),
        out_shape=jax.ShapeDtypeStruct((t, d), F32),
        scratch_shapes=[pltpu.VMEM((tm, d), BF16)],
        compiler_params=_cparams("parallel", "arbitrary", vmem=VMEM_LIMIT_FFN),
        name="ffn_final" if final else "ffn_half",
    )(*args)


def _rope(x, cos, sin):
    lane = lax.broadcasted_iota(jnp.int32, x.shape, 1)
    partner = jnp.where((lane % 64) < 32, pltpu.roll(x, 96, 1), pltpu.roll(x, 32, 1))
    return x * cos + partner * sin


def _inproj_kernel(x_ref, mod_ref, g_ref, w_ref, *rest, rope_tiles):
    if rope_tiles:
        cos_ref, sin_ref, o_ref, h_scr = rest
    else:
        o_ref, h_scr = rest
    j = pl.program_id(1)

    @pl.when(j == 0)
    def _():
        h = _norm_mod(x_ref[...], g_ref[...], mod_ref[0:1, :], mod_ref[1:2, :])
        h_scr[...] = h.astype(BF16)

    if rope_tiles:
        @pl.when(j < rope_tiles)
        def _():
            acc = _dot(h_scr[...], w_ref[...])
            scale = jnp.where(j == 0, RET_DK ** -0.5, 1.0)
            cos = cos_ref[...] * scale
            sin = sin_ref[...] * scale
            for hb in range(acc.shape[1] // RET_DK):
                cols = slice(hb * RET_DK, (hb + 1) * RET_DK)
                o_ref[:, cols] = _rope(acc[:, cols], cos, sin).astype(BF16)

        @pl.when(j >= rope_tiles)
        def _():
            o_ref[...] = _dot(h_scr[...], w_ref[...]).astype(BF16)
    else:
        o_ref[...] = _dot(h_scr[...], w_ref[...]).astype(BF16)


def _in_proj(x, mod, g, w, col0, ncols, rope=None, tm=1024, tn=1024):
    t, d = x.shape
    tm = min(tm, t)
    off = col0 // tn
    in_specs = [pl.BlockSpec((tm, d), lambda i, j: (i, 0)),
                pl.BlockSpec((8, d), lambda i, j: (0, 0)),
                pl.BlockSpec((1, d), lambda i, j: (0, 0)),
                pl.BlockSpec((d, tn), lambda i, j: (0, j + off))]
    args = [x, mod, g.reshape(1, d), w]
    rope_tiles = 0
    if rope is not None:
        cos_t, sin_t, qk_cols = rope
        assert tn == RET_Q and qk_cols == 2 * RET_Q and col0 == 0
        rope_tiles = qk_cols // tn
        in_specs += [pl.BlockSpec((tm, RET_DK), lambda i, j: (i, 0))] * 2
        args += [cos_t, sin_t]
    return pl.pallas_call(
        functools.partial(_inproj_kernel, rope_tiles=rope_tiles),
        grid=(t // tm, ncols // tn),
        in_specs=in_specs,
        out_specs=pl.BlockSpec((tm, tn), lambda i, j: (i, j)),
        out_shape=jax.ShapeDtypeStruct((t, ncols), BF16),
        scratch_shapes=[pltpu.VMEM((tm, d), BF16)],
        compiler_params=_cparams("parallel", "arbitrary"),
        name="in_proj",
    )(*args)


def _rope_tables(t):
    half = RET_DK // 2
    pos = jnp.arange(t, dtype=jnp.int32)
    inv = ROPE_BASE ** (-jnp.arange(0, half, 2, dtype=F32) / half)
    ang_r = (pos // GRID_W).astype(F32)[:, None] * inv[None, :]
    ang_c = (pos % GRID_W).astype(F32)[:, None] * inv[None, :]
    cos_t = jnp.concatenate([jnp.cos(ang_r)] * 2 + [jnp.cos(ang_c)] * 2, axis=-1)
    sin_t = jnp.concatenate([-jnp.sin(ang_r), jnp.sin(ang_r), -jnp.sin(ang_c), jnp.sin(ang_c)], axis=-1)
    return cos_t, sin_t


def _ctx_state_kernel(lg_ref, k_ref, v_ref, o_ref):
    h = pl.program_id(0)
    lc = k_ref.shape[0]
    m = lax.broadcasted_iota(jnp.int32, (lc, 1), 0).astype(F32)
    k = k_ref[...].astype(F32)
    v = v_ref[...]
    w_f = jnp.exp(lg_ref[0, h] * (lc - 1.0 - m))
    w_b = jnp.exp(lg_ref[1, h] * m)
    o_ref[0, 0] = _dot_t0((k * w_f).astype(BF16), v)
    o_ref[1, 0] = _dot_t0((k * w_b).astype(BF16), v)


def _ctx_states(pkv, lg):
    lc = pkv.shape[0]
    return pl.pallas_call(
        _ctx_state_kernel,
        grid=(RET_HEADS,),
        in_specs=[pl.BlockSpec(memory_space=pltpu.SMEM),
                  pl.BlockSpec((lc, RET_DK), lambda h: (0, h)),
                  pl.BlockSpec((lc, RET_DV), lambda h: (0, RET_Q // RET_DV + h))],
        out_specs=pl.BlockSpec((2, 1, RET_DK, RET_DV), lambda h: (0, h, 0, 0)),
        out_shape=jax.ShapeDtypeStruct((2, RET_HEADS, RET_DK, RET_DV), F32),
        compiler_params=_cparams("arbitrary"),
        name="ctx_states",
    )(lg, pkv, pkv)


def _ret_kernel(lg_ref, q_ref, k_ref, v_ref, g_ref, st0_ref, gng_ref, gnb_ref,
                o_ref, sb_all, sf_scr, sb_scr, *, nb, bc):
    c_len = RET_CHUNK
    h = pl.program_id(0)
    i = pl.program_id(1)
    lgf = lg_ref[0, h]
    lgb = lg_ref[1, h]
    pos = lax.broadcasted_iota(jnp.int32, (c_len, 1), 0).astype(F32)
    ones_row = jnp.ones((1, RET_DV), F32)

    @pl.when(i == 0)
    def _():
        sf_scr[...] = st0_ref[0, 0]
        sb_scr[...] = st0_ref[1, 0]

    @pl.when(i < nb)
    def _():
        rb = nb - 1 - i
        zeta_b = jnp.exp(lgb * pos)
        cd_b = jnp.exp((lgb * c_len) * ones_row)

        s_b = sb_scr[...]
        for c in reversed(range(bc)):
            rows = slice(c * c_len, (c + 1) * c_len)
            sb_all[rb * bc + c] = s_b.astype(BF16)
            kz = (k_ref[rows, :].astype(F32) * zeta_b).astype(BF16)
            s_b = s_b * cd_b + _dot_t0(kz, v_ref[rows, :])
        sb_scr[...] = s_b

    @pl.when(i >= nb)
    def _():
        rb = i - nb
        a = lax.broadcasted_iota(jnp.int32, (c_len, c_len), 0)
        b = lax.broadcasted_iota(jnp.int32, (c_len, c_len), 1)
        d = (a - b).astype(F32)
        decay = jnp.where(d > 0, jnp.exp(lgf * jnp.maximum(d, 0.0)),
                          jnp.where(d < 0, jnp.exp(lgb * jnp.maximum(-d, 0.0)), 2.0))
        xi_f = jnp.exp(lgf * (pos + 1.0))
        xi_b = jnp.exp(lgb * (c_len - pos))
        zeta_f = jnp.exp(lgf * (c_len - 1.0 - pos))
        cd_f = jnp.exp((lgf * c_len) * ones_row)

        s_f = sf_scr[...]
        for c in range(bc):
            rows = slice(c * c_len, (c + 1) * c_len)
            q = q_ref[rows, :]
            k = k_ref[rows, :]
            v = v_ref[rows, :]
            s = lax.dot_general(q, k, (((1,), (1,)), ((), ())), preferred_element_type=F32) * decay
            y = (_dot(s.astype(BF16), v)
                 + xi_f * _dot(q, s_f.astype(BF16))
                 + xi_b * _dot(q, sb_all[rb * bc + c]))
            kz = (k.astype(F32) * zeta_f).astype(BF16)
            s_f = s_f * cd_f + _dot_t0(kz, v)
            mu = jnp.mean(y, axis=-1, keepdims=True)
            yc = y - mu
            var = jnp.mean(yc * yc, axis=-1, keepdims=True)
            yn = yc * lax.rsqrt(var + EPS) * gng_ref[...] + gnb_ref[...]
            o_ref[rows, :] = (_silu(g_ref[rows, :].astype(F32)) * yn).astype(BF16)
        sf_scr[...] = s_f


def _retention(p, st0, lg, gn_g, gn_b, q_off, k_off, v_off, g_off, bc=16):
    t = p.shape[0]
    bc = min(bc, t // RET_CHUNK)
    rows = bc * RET_CHUNK
    nb = t // rows

    def kv_rb(i):
        return jnp.where(i < nb, nb - 1 - i, i - nb)

    def q_rb(i):
        return jnp.maximum(i - nb, 0)

    return pl.pallas_call(
        functools.partial(_ret_kernel, nb=nb, bc=bc),
        grid=(RET_HEADS, 2 * nb),
        in_specs=[pl.BlockSpec(memory_space=pltpu.SMEM),
                  pl.BlockSpec((rows, RET_DK), lambda h, i: (q_rb(i), q_off // RET_DK + h)),
                  pl.BlockSpec((rows, RET_DK), lambda h, i: (kv_rb(i), k_off // RET_DK + h)),
                  pl.BlockSpec((rows, RET_DV), lambda h, i: (kv_rb(i), v_off // RET_DV + h)),
                  pl.BlockSpec((rows, RET_DV), lambda h, i: (q_rb(i), g_off // RET_DV + h)),
                  pl.BlockSpec((2, 1, RET_DK, RET_DV), lambda h, i: (0, h, 0, 0)),
                  pl.BlockSpec((1, RET_DV), lambda h, i: (0, h)),
                  pl.BlockSpec((1, RET_DV), lambda h, i: (0, h))],
        out_specs=pl.BlockSpec((rows, RET_DV), lambda h, i: (q_rb(i), h)),
        out_shape=jax.ShapeDtypeStruct((t, RET_V), BF16),
        scratch_shapes=[pltpu.VMEM((t // RET_CHUNK, RET_DK, RET_DV), BF16),
                        pltpu.VMEM((RET_DK, RET_DV), F32),
                        pltpu.VMEM((RET_DK, RET_DV), F32)],
        compiler_params=_cparams("arbitrary", "arbitrary"),
        name="retention",
    )(lg, p, p, p, p, st0, gn_g.reshape(1, RET_V), gn_b.reshape(1, RET_V))


def _hy_pre_kernel(m0, m1, m2, p0, p1, p2, n0, n1, n2, w_ref, b_ref, zx_ref):
    i = pl.program_id(0)
    tm, ch = m0.shape
    row = lax.broadcasted_iota(jnp.int32, (tm, 1), 0)
    has_prev = (i > 0).astype(F32)
    has_next = (i < pl.num_programs(0) - 1).astype(F32)
    halo = p0.shape[0]

    def conv(main, prev, nxt, part):
        cols = slice(part * ch, (part + 1) * ch)
        u = main[...].astype(F32)
        before = jnp.where(row == 0, prev[halo - 1:halo, :].astype(F32) * has_prev, pltpu.roll(u, 1, 0))
        after = jnp.where(row == tm - 1, nxt[0:1, :].astype(F32) * has_next, pltpu.roll(u, tm - 1, 0))
        return (before * w_ref[0:1, cols] + u * w_ref[1:2, cols] + after * w_ref[2:3, cols]
                + b_ref[0:1, cols])

    x0 = conv(m0, p0, n0, 0)
    z = conv(m1, p1, n1, 1) * conv(m2, p2, n2, 2)
    zx_ref[...] = _pack2(z, x0)


def _hy_pre(p, conv_w, conv_b, hy_off, ch, tm=256, halo=16):
    t = p.shape[0]
    tm = min(tm, t)
    nh = tm // halo
    last = t // halo - 1
    cb0 = hy_off // ch

    def main(part):
        return pl.BlockSpec((tm, ch), lambda i: (i, cb0 + part))

    def prev(part):
        return pl.BlockSpec((halo, ch), lambda i: (jnp.maximum(i * nh - 1, 0), cb0 + part))

    def nxt(part):
        return pl.BlockSpec((halo, ch), lambda i: (jnp.minimum((i + 1) * nh, last), cb0 + part))

    return pl.pallas_call(
        _hy_pre_kernel,
        grid=(t // tm,),
        in_specs=[main(0), main(1), main(2), prev(0), prev(1), prev(2), nxt(0), nxt(1), nxt(2),
                  pl.BlockSpec((3, 3 * ch), lambda i: (0, 0)),
                  pl.BlockSpec((1, 3 * ch), lambda i: (0, 0))],
        out_specs=pl.BlockSpec((tm, ch), lambda i: (i, 0)),
        out_shape=jax.ShapeDtypeStruct((t, ch), U32),
        compiler_params=_cparams("parallel"),
        name="hy_conv3",
    )(p, p, p, p, p, p, p, p, p, conv_w, conv_b.reshape(1, 3 * ch))


def _filt_ffn_kernel(f_ref, w1, b1, w2, b2, w3, b3, fr, o_ref):
    hp = lax.Precision.HIGHEST
    freq = fr[:, 0:1]
    h = jnp.sin(freq * (jnp.dot(w1[...], f_ref[...], precision=hp, preferred_element_type=F32) + b1[:, 0:1]))
    h = jnp.sin(freq * (jnp.dot(w2[...], h, precision=hp, preferred_element_type=F32) + b2[:, 0:1]))
    o_ref[...] = jnp.sin(freq * (jnp.dot(w3[...], h, precision=hp, preferred_element_type=F32) + b3[:, 0:1]))


def _filter_ffn(feat_t, w1, b1, w2, b2, w3, b3, freq, tl=2048):
    e, length = feat_t.shape
    ff = w1.shape[1]
    tl = min(tl, length)

    def col(v):
        return jnp.broadcast_to(v.reshape(ff, 1), (ff, 128))

    full = lambda shape: pl.BlockSpec(shape, lambda i: (0, 0))
    return pl.pallas_call(
        _filt_ffn_kernel,
        grid=(length // tl,),
        in_specs=[pl.BlockSpec((e, tl), lambda i: (0, i)),
                  full((ff, e)), full((ff, 128)), full((ff, ff)), full((ff, 128)),
                  full((ff, ff)), full((ff, 128)), full((ff, 128))],
        out_specs=pl.BlockSpec((ff, tl), lambda i: (0, i)),
        out_shape=jax.ShapeDtypeStruct((ff, length), F32),
        compiler_params=_cparams("parallel"),
        name="filter_ffn",
    )(feat_t, w1.T, col(b1), w2.T, col(b2), w3.T, col(b3), col(freq))


def _filt_taps_kernel(hf_ref, hb_ref, tf_ref, tb_ref, wf_ref, wb_ref, dec_ref, hp_ref, ssq_ref):
    dec = jnp.abs(dec_ref[...])
    fwd = _dot(hf_ref[...].astype(BF16), wf_ref[...]) * jnp.exp(-tf_ref[:, 0:1] * dec)
    bwd = -_dot(hb_ref[...].astype(BF16), wb_ref[...]) * jnp.exp(-tb_ref[:, 0:1] * dec)

    @pl.when(pl.program_id(0) == 0)
    def _():
        ssq_ref[...] = jnp.zeros_like(ssq_ref)

    ssq_ref[...] += jnp.sum(fwd * fwd + bwd * bwd, axis=0, keepdims=True)
    hp_ref[...] = _pack2(fwd, bwd)


def _filter_taps(h_ext, t_ext, w4, decay, tr=512):
    n, ff = h_ext.shape
    length = n // 2
    ch = decay.shape[0]
    tr = min(tr, length)
    nt = length // tr
    return pl.pallas_call(
        _filt_taps_kernel,
        grid=(nt,),
        in_specs=[pl.BlockSpec((tr, ff), lambda i: (i, 0)),
                  pl.BlockSpec((tr, ff), lambda i: (i + nt, 0)),
                  pl.BlockSpec((tr, 1), lambda i: (i, 0)),
                  pl.BlockSpec((tr, 1), lambda i: (i + nt, 0)),
                  pl.BlockSpec((ff, ch), lambda i: (0, 0)),
                  pl.BlockSpec((ff, ch), lambda i: (0, 1)),
                  pl.BlockSpec((1, ch), lambda i: (0, 0))],
        out_specs=[pl.BlockSpec((tr, ch), lambda i: (i, 0)),
                   pl.BlockSpec((1, ch), lambda i: (0, 0))],
        out_shape=[jax.ShapeDtypeStruct((length, ch), U32), jax.ShapeDtypeStruct((1, ch), F32)],
        compiler_params=_cparams("arbitrary"),
        name="filter_taps",
    )(h_ext, h_ext, t_ext, t_ext, w4, w4, decay.reshape(1, ch))


def _slab_specs(rows, first_lane_tile):
    return [pl.BlockSpec((rows, None, SUBLANES, LANES),
                         lambda g, j, s=s: (0, g, 0, first_lane_tile(j) + s)) for s in range(DFT_SLABS)]


def _flatten_slabs(slabs, flats):
    for s, f in zip(slabs, flats):
        f[...] = s[...].reshape(f.shape)


def _gather_rows(flats, a):
    return jnp.concatenate([f[pl.ds(a, f.shape[0] // SUBLANES, stride=SUBLANES), :] for f in flats], axis=1)


def _flat_scratch(rows, count):
    return [pltpu.VMEM((rows * SUBLANES, LANES), U32)] * count


def _dft_a_kernel(*refs, both_halves):
    slabs, (e_ref, o_ref), flats = refs[:DFT_SLABS], refs[DFT_SLABS:DFT_SLABS + 2], refs[DFT_SLABS + 2:]
    _flatten_slabs(slabs, flats)
    for a in range(SUBLANES):
        hi, lo = _unpack2(_gather_rows(flats, a))
        x = jnp.concatenate([hi, lo], axis=0) if both_halves else hi
        r = _dot(e_ref[a], x.astype(BF16))
        n = r.shape[0] // 2
        o_ref[a] = _pack2(r[:n], r[n:])


def _dft_stage_a(x, e_tab, both_halves):
    n1, m2, kdim = e_tab.shape
    length, c = x.shape
    k2 = length // n1
    assert kdim == (2 * k2 if both_halves else k2)
    width = DFT_SLABS * LANES
    return pl.pallas_call(
        functools.partial(_dft_a_kernel, both_halves=both_halves),
        grid=(n1 // SUBLANES, c // width),
        in_specs=_slab_specs(k2, lambda j: DFT_SLABS * j)
        + [pl.BlockSpec((SUBLANES, m2, kdim), lambda g, j: (g, 0, 0))],
        out_specs=pl.BlockSpec((SUBLANES, m2 // 2, width), lambda g, j: (g, 0, j)),
        out_shape=jax.ShapeDtypeStruct((n1, m2 // 2, c), U32),
        scratch_shapes=_flat_scratch(k2, DFT_SLABS),
        compiler_params=_cparams("parallel", "arbitrary"),
        name="dft_stage_a",
    )(*([x.reshape(k2, n1 // SUBLANES, SUBLANES, c)] * DFT_SLABS), e_tab)


def _dft_mid_kernel(*refs):
    zs, hs = refs[:DFT_SLABS], refs[DFT_SLABS:2 * DFT_SLABS]
    f_ref, finv_ref, o_ref = refs[2 * DFT_SLABS:2 * DFT_SLABS + 3]
    flats = refs[2 * DFT_SLABS + 3:]
    _flatten_slabs(zs + hs, flats)
    n = zs[0].shape[0]
    for a in range(SUBLANES):
        zr, zi = _unpack2(_gather_rows(flats[:DFT_SLABS], a))
        hr, hi = _unpack2(_gather_rows(flats[DFT_SLABS:], a))
        xs = _dot(f_ref[...], jnp.concatenate([zr, zi], axis=0).astype(BF16))
        ks = _dot(f_ref[...], jnp.concatenate([hr, hi], axis=0).astype(BF16))
        xr, xi, kr, ki = xs[:n], xs[n:], ks[:n], ks[n:]
        y = jnp.concatenate([xr * kr - xi * ki, xr * ki + xi * kr], axis=0).astype(BF16)
        cs = _dot(finv_ref[...], y)
        o_ref[a] = _pack2(cs[:n], cs[n:])


def _dft_mid(bz, bh, f_fwd, f_inv):
    n1, n2h, c = bz.shape
    width = DFT_SLABS * LANES
    mat = pl.BlockSpec((2 * n1, 2 * n1), lambda g, j: (0, 0))
    view = lambda b: [b.reshape(n1, n2h // SUBLANES, SUBLANES, c)] * DFT_SLABS
    slabs = _slab_specs(n1, lambda j: DFT_SLABS * j)
    return pl.pallas_call(
        _dft_mid_kernel,
        grid=(n2h // SUBLANES, c // width),
        in_specs=slabs + slabs + [mat, mat],
        out_specs=pl.BlockSpec((SUBLANES, n1, width), lambda g, j: (g, 0, j)),
        out_shape=jax.ShapeDtypeStruct((n2h, n1, c), U32),
        scratch_shapes=_flat_scratch(n1, 2 * DFT_SLABS),
        compiler_params=_cparams("parallel", "arbitrary"),
        name="dft_mid",
    )(*view(bz), *view(bh), f_fwd, f_inv)


def _dft_ainv_kernel(*refs):
    cs, zxs = refs[:DFT_SLABS], refs[DFT_SLABS:2 * DFT_SLABS]
    e_ref, skip_ref, sc_ref, o_ref = refs[2 * DFT_SLABS:2 * DFT_SLABS + 4]
    flats = refs[2 * DFT_SLABS + 4:]
    _flatten_slabs(cs + zxs, flats)
    for a in range(SUBLANES):
        c_r, c_i = _unpack2(_gather_rows(flats[:DFT_SLABS], a))
        y = _dot(e_ref[a], jnp.concatenate([c_r, c_i], axis=0).astype(BF16)) * sc_ref[...]
        z, x0 = _unpack2(_gather_rows(flats[DFT_SLABS:], a))
        o_ref[:, a, :] = x0 * (y + z * skip_ref[...])


def _dft_stage_a_inv(cc, e_inv, zx, skip, scale):
    n2h, n1, c = cc.shape
    length = zx.shape[0]
    k2 = length // n1
    width = DFT_SLABS * LANES
    row = pl.BlockSpec((1, width), lambda g, j: (0, j))
    out = pl.pallas_call(
        _dft_ainv_kernel,
        grid=(n1 // SUBLANES, c // width),
        in_specs=_slab_specs(n2h, lambda j: DFT_SLABS * j) + _slab_specs(k2, lambda j: DFT_SLABS * j)
        + [pl.BlockSpec((SUBLANES, k2, 2 * n2h), lambda g, j: (g, 0, 0)), row, row],
        out_specs=pl.BlockSpec((k2, None, SUBLANES, width), lambda g, j: (0, g, 0, j)),
        out_shape=jax.ShapeDtypeStruct((k2, n1 // SUBLANES, SUBLANES, c), F32),
        scratch_shapes=_flat_scratch(n2h, DFT_SLABS) + _flat_scratch(k2, DFT_SLABS),
        compiler_params=_cparams("parallel", "arbitrary"),
        name="dft_stage_a_inv",
    )(*([cc.reshape(n2h, n1 // SUBLANES, SUBLANES, c)] * DFT_SLABS),
      *([zx.reshape(k2, n1 // SUBLANES, SUBLANES, c)] * DFT_SLABS),
      e_inv, skip.reshape(1, c), scale)
    return out.reshape(length, c)


def _dft_tables(n1, n2h):
    n2 = 2 * n2h
    n = n1 * n2
    a = jnp.arange(n1, dtype=jnp.int32)
    b = jnp.arange(n2, dtype=jnp.int32)
    odd = 2 * jnp.arange(n2h, dtype=jnp.int32) + 1
    alpha = ((a[:, None] * odd[None, :]) % (2 * n)).astype(F32) * (math.pi / n)
    beta = ((odd[:, None] * b[None, :]) % (2 * n2)).astype(F32) * (math.pi / n2)
    ar, ai = jnp.cos(alpha)[:, :, None], -jnp.sin(alpha)[:, :, None]
    br, bi = jnp.cos(beta)[None], -jnp.sin(beta)[None]
    er = ar * br - ai * bi
    ei = ar * bi + ai * br
    e_full = jnp.concatenate([er, ei], axis=1).astype(BF16)
    e_half = e_full[:, :, :n2h]
    e_inv = jnp.concatenate([jnp.swapaxes(er[:, :, :n2h], 1, 2), jnp.swapaxes(ei[:, :, :n2h], 1, 2)],
                            axis=2).astype(BF16)
    phi = ((a[:, None] * a[None, :]) % n1).astype(F32) * (2.0 * math.pi / n1)
    fr, fi = jnp.cos(phi), -jnp.sin(phi)
    f_fwd = jnp.concatenate([jnp.concatenate([fr, -fi], axis=1),
                             jnp.concatenate([fi, fr], axis=1)], axis=0).astype(BF16)
    f_inv = jnp.concatenate([jnp.concatenate([fr, fi], axis=1),
                             jnp.concatenate([-fi, fr], axis=1)], axis=0).astype(BF16)
    return e_half, e_full, e_inv, f_fwd, f_inv


def _hyena_features(length):
    t = jnp.linspace(0.0, 1.0, length, dtype=F32)[:, None]
    w = 2.0 * math.pi * jnp.arange(length, dtype=F32)[:, None] / length
    f = jnp.linspace(1e-4, HY_BANDS - 1.0, HY_BANDS, dtype=F32)[None, :]
    return t, jnp.concatenate([t, jnp.cos(f * w), -jnp.sin(f * w)], axis=-1)


def _merge_kernel(x_ref, yr_ref, hy_ref, gr_ref, gh_ref, gate_ref, wr_ref, wh_ref, wo_ref, o_ref):
    ret_out = _dot(yr_ref[...], wr_ref[...])
    hy_out = _dot(hy_ref[...].astype(BF16), wh_ref[...])
    m = (jax.nn.sigmoid(gr_ref[...].astype(F32)) * ret_out
         + jax.nn.sigmoid(gh_ref[...].astype(F32)) * hy_out)
    o_ref[...] = x_ref[...] + gate_ref[...] * _dot(m.astype(BF16), wo_ref[...])


def _merge(x, y_ret, hy_in, p, gate_off, gate, w_ret_o, w_hy_o, w_out, tm=256):
    t, d = x.shape
    tm = min(tm, t)
    gb = gate_off // d
    tile = lambda: pl.BlockSpec((tm, d), lambda i: (i, 0))
    weight = lambda w: pl.BlockSpec(w.shape, lambda i: (0, 0), pipeline_mode=pl.Buffered(1))
    return pl.pallas_call(
        _merge_kernel,
        grid=(t // tm,),
        in_specs=[tile(), tile(), tile(),
                  pl.BlockSpec((tm, d), lambda i: (i, gb)),
                  pl.BlockSpec((tm, d), lambda i: (i, gb + 1)),
                  pl.BlockSpec((1, d), lambda i: (0, 0)),
                  weight(w_ret_o), weight(w_hy_o), weight(w_out)],
        out_specs=tile(),
        out_shape=jax.ShapeDtypeStruct((t, d), F32),
        compiler_params=_cparams("parallel"),
        name="merge_out",
    )(x, y_ret, hy_in, p, p, gate.reshape(1, d), w_ret_o, w_hy_o, w_out)


def _mod_rows(mods, first):
    return jnp.pad(mods[first:first + 3], ((0, 5), (0, 0)))


def kernel(x, c, ctx, c_ctx, w_ada, b_ada, norm_g, ffn_up, ffn_down, w_in, ret_log_gamma, ret_gn_g,
           ret_gn_b, w_ret_o, hy_conv_w, hy_conv_b, hy_ff_w1, hy_ff_b1, hy_ff_w2, hy_ff_b2, hy_ff_w3,
           hy_ff_b3, hy_ff_w4, hy_sin_freq, hy_decay, hy_bias, w_hy_o, w_out, final_norm_g):
    batch, t, d = x.shape
    assert batch == 1 and w_ada.shape[0] == 1, "single sample, single layer"
    hy_w = hy_decay.shape[1]
    q_off, k_off = 0, RET_Q
    v_off = k_off + RET_Q
    g_off = v_off + RET_V
    hy_off = g_off + RET_V
    gate_off = hy_off + 3 * hy_w
    assert w_in.shape[2] == gate_off + 2 * d and hy_w == d

    xs, cs = x[0], ctx[0]
    up0, up1 = ffn_up[0, 0].astype(BF16), ffn_up[0, 1].astype(BF16)
    down0, down1 = ffn_down[0, 0].astype(BF16), ffn_down[0, 1].astype(BF16)
    w_in_b = w_in[0].astype(BF16)
    lg = ret_log_gamma[0].astype(F32)

    cv = jnp.concatenate([c, c_ctx[None, :], jnp.zeros((6, d), F32)], axis=0)
    mods = _adaln(cv, w_ada[0], b_ada[0]).reshape(8, N_MOD, d)
    mx, mc = mods[0], mods[1]

    xs = _ffn_half(xs, _mod_rows(mx, 0), norm_g[0, 0], up0, down0)
    cs = _ffn_half(cs, _mod_rows(mc, 0), norm_g[0, 0], up0, down0)

    pkv = _in_proj(cs, _mod_rows(mc, 3), norm_g[0, 1], w_in_b, k_off, g_off - k_off)
    st0 = _ctx_states(pkv, lg)

    cos_t, sin_t = _rope_tables(t)
    p = _in_proj(xs, _mod_rows(mx, 3), norm_g[0, 1], w_in_b, 0, w_in_b.shape[1],
                 rope=(cos_t, sin_t, v_off))

    y_ret = _retention(p, st0, lg, ret_gn_g[0], ret_gn_b[0], q_off, k_off, v_off, g_off)

    zx = _hy_pre(p, hy_conv_w[0], hy_conv_b[0], hy_off, hy_w)
    t_lin, feat = _hyena_features(t)
    e_pad = (-feat.shape[1]) % 8
    feat_t = jnp.pad(feat.T, ((0, e_pad), (0, 0)))
    w1 = jnp.pad(hy_ff_w1[0], ((0, e_pad), (0, 0)))
    h3 = _filter_ffn(feat_t, w1, hy_ff_b1[0], hy_ff_w2[0], hy_ff_b2[0], hy_ff_w3[0], hy_ff_b3[0],
                     hy_sin_freq[0]).T
    zero_row = jnp.zeros((1, h3.shape[1]), F32)
    h_ext = jnp.concatenate([h3, zero_row, jnp.flip(h3[1:], axis=0)], axis=0)
    t_ext = jnp.concatenate([t_lin, jnp.zeros((1, 1), F32), jnp.flip(t_lin[1:], axis=0)], axis=0)
    h_taps, ssq = _filter_taps(h_ext, t_ext, hy_ff_w4[0].astype(BF16), hy_decay[0])

    n1 = DFT_N1
    e_half, e_full, e_inv, f_fwd, f_inv = _dft_tables(n1, t // n1)
    bz = _dft_stage_a(zx, e_half, both_halves=False)
    bh = _dft_stage_a(h_taps, e_full, both_halves=True)
    cc = _dft_mid(bz, bh, f_fwd, f_inv)
    scale = lax.rsqrt(ssq + EPS) * (2.0 / (2 * t))
    hy_in = _dft_stage_a_inv(cc, e_inv, zx, hy_bias[0], scale)

    xs = _merge(xs, y_ret, hy_in, p, gate_off, mx[5], w_ret_o[0].astype(BF16), w_hy_o[0].astype(BF16),
                w_out[0].astype(BF16))

    out = _ffn_half(xs, _mod_rows(mx, 6), norm_g[0, 2], up1, down1, final_g=final_norm_g)
    return out[None]
```

```python
import functools
import math

import jax
import jax.numpy as jnp
from jax import lax
from jax.experimental import pallas as pl
from jax.experimental.pallas import tpu as pltpu

F32 = jnp.float32
BF16 = jnp.bfloat16
U32 = jnp.uint32

N_MOD = 9
GRID_W = 64
RET_HEADS = 8
RET_DK = 128
RET_DV = 256
RET_CHUNK = 128
RET_Q = RET_HEADS * RET_DK
RET_V = RET_HEADS * RET_DV
HY_BANDS = 16
ROPE_BASE = 10000.0
EPS = 1e-6

DFT_N1 = 128
SUBLANES = 8
LANES = 128
DFT_SLABS = 4
DFT_WIDTH = 1024
VMEM_BYTES = 64 * 1024 * 1024
VMEM_LIMIT = VMEM_BYTES - 8 * 1024 * 1024
VMEM_LIMIT_FFN = VMEM_BYTES - 4 * 1024 * 1024


def _cparams(*sem, vmem=VMEM_LIMIT):
    return pltpu.CompilerParams(dimension_semantics=sem, vmem_limit_bytes=vmem)


def _dot(a, b):
    return jnp.dot(a, b, preferred_element_type=F32)


def _dot_t0(a, b):
    return lax.dot_general(a, b, (((0,), (0,)), ((), ())), preferred_element_type=F32)


def _silu(v):
    return v * jax.nn.sigmoid(v)


def _norm_mod(x, g, shift, scale):
    y = x * lax.rsqrt(jnp.mean(x * x, axis=-1, keepdims=True) + EPS) * g
    return y * (1.0 + scale) + shift


def _pack2(hi, lo):
    hb = lax.bitcast_convert_type(hi.astype(BF16).astype(F32), U32)
    lb = lax.bitcast_convert_type(lo.astype(BF16).astype(F32), U32)
    return hb | (lb >> 16)


def _unpack2(w):
    hi = lax.bitcast_convert_type(w & jnp.uint32(0xFFFF0000), F32)
    lo = lax.bitcast_convert_type(w << 16, F32)
    return hi, lo


def _adaln_kernel(c_ref, w_ref, b_ref, o_ref):
    s = _silu(c_ref[...]).astype(BF16)
    o_ref[...] = _dot(s, w_ref[...].astype(BF16)) + b_ref[...]


def _adaln(cv, w, b, tn=1024):
    rows, d = cv.shape
    n = w.shape[1]
    return pl.pallas_call(
        _adaln_kernel,
        grid=(n // tn,),
        in_specs=[pl.BlockSpec((rows, d), lambda j: (0, 0)),
                  pl.BlockSpec((d, tn), lambda j: (0, j)),
                  pl.BlockSpec((1, tn), lambda j: (0, j))],
        out_specs=pl.BlockSpec((rows, tn), lambda j: (0, j)),
        out_shape=jax.ShapeDtypeStruct((rows, n), F32),
        compiler_params=_cparams("arbitrary"),
        name="adaln",
    )(cv, w, b.reshape(1, n))


def _ffn_kernel(x_ref, mod_ref, g_ref, wa_ref, wg_ref, wd_ref, *rest, final):
    if final:
        fg_ref, o_ref, h_scr = rest
    else:
        o_ref, h_scr = rest
    j = pl.program_id(1)

    @pl.when(j == 0)
    def _():
        h = _norm_mod(x_ref[...], g_ref[...], mod_ref[0:1, :], mod_ref[1:2, :])
        h_scr[...] = h.astype(BF16)
        o_ref[...] = jnp.zeros_like(o_ref)

    h = h_scr[...]
    a = _dot(h, wa_ref[...])
    g = _dot(h, wg_ref[...])
    o_ref[...] += _dot((_silu(a) * g).astype(BF16), wd_ref[...])

    @pl.when(j == pl.num_programs(1) - 1)
    def _():
        out = x_ref[...] + (0.5 * mod_ref[2:3, :]) * o_ref[...]
        if final:
            out = out * lax.rsqrt(jnp.mean(out * out, axis=-1, keepdims=True) + EPS) * fg_ref[...]
        o_ref[...] = out


def _ffn_half(x, mod, g, w_up, w_down, layer, final_g=None, tm=1024, tf=512):
    t, d = x.shape
    dff = w_down.shape[2]
    tm = min(tm, t)
    nf = dff // tf
    final = final_g is not None
    in_specs = [pl.BlockSpec((tm, d), lambda i, j: (i, 0), pipeline_mode=pl.Buffered(1)),
                pl.BlockSpec((8, d), lambda i, j: (0, 0)),
                pl.BlockSpec((1, d), lambda i, j: (0, 0)),
                pl.BlockSpec((None, None, d, tf), lambda i, j: (0, layer, 0, j)),
                pl.BlockSpec((None, None, d, tf), lambda i, j: (0, layer, 0, nf + j)),
                pl.BlockSpec((None, None, tf, d), lambda i, j: (0, layer, j, 0))]
    args = [x, mod, g.reshape(1, d), w_up, w_up, w_down]
    if final:
        in_specs.append(pl.BlockSpec((1, d), lambda i, j: (0, 0)))
        args.append(final_g.reshape(1, d))
    return pl.pallas_call(
        functools.partial(_ffn_kernel, final=final),
        grid=(t // tm, nf),
        in_specs=in_specs,
        out_specs=pl.BlockSpec((tm, d), lambda i, j: (i, 0)),
        out_shape=jax.ShapeDtypeStruct((t, d), F32),
        scratch_shapes=[pltpu.VMEM((tm, d), BF16)],
        compiler_params=_cparams("parallel", "arbitrary", vmem=VMEM_LIMIT_FFN),
        name="ffn_final" if final else "ffn_half",
    )(*args)


def _rope(x, cos, sin):
    lane = lax.broadcasted_iota(jnp.int32, x.shape, 1)
    partner = jnp.where((lane % 64) < 32, pltpu.roll(x, 96, 1), pltpu.roll(x, 32, 1))
    return x * cos + partner * sin


def _inproj_kernel(x_ref, mod_ref, g_ref, w_ref, *rest, rope_tiles):
    if rope_tiles:
        cos_ref, sin_ref, o_ref, h_scr = rest
    else:
        o_ref, h_scr = rest
    j = pl.program_id(1)

    @pl.when(j == 0)
    def _():
        h = _norm_mod(x_ref[...], g_ref[...], mod_ref[0:1, :], mod_ref[1:2, :])
        h_scr[...] = h.astype(BF16)

    if rope_tiles:
        @pl.when(j < rope_tiles)
        def _():
            acc = _dot(h_scr[...], w_ref[...])
            scale = jnp.where(j == 0, RET_DK ** -0.5, 1.0)
            cos = cos_ref[...] * scale
            sin = sin_ref[...] * scale
            for hb in range(acc.shape[1] // RET_DK):
                cols = slice(hb * RET_DK, (hb + 1) * RET_DK)
                o_ref[:, cols] = _rope(acc[:, cols], cos, sin).astype(BF16)

        @pl.when(j >= rope_tiles)
        def _():
            o_ref[...] = _dot(h_scr[...], w_ref[...]).astype(BF16)
    else:
        o_ref[...] = _dot(h_scr[...], w_ref[...]).astype(BF16)


def _in_proj(x, mod, g, w, col0, ncols, rope=None, tm=1024, tn=1024):
    t, d = x.shape
    tm = min(tm, t)
    off = col0 // tn
    in_specs = [pl.BlockSpec((tm, d), lambda i, j: (i, 0)),
                pl.BlockSpec((8, d), lambda i, j: (0, 0)),
                pl.BlockSpec((1, d), lambda i, j: (0, 0)),
                pl.BlockSpec((None, d, tn), lambda i, j: (0, 0, j + off))]
    args = [x, mod, g.reshape(1, d), w]
    rope_tiles = 0
    if rope is not None:
        cos_t, sin_t, qk_cols = rope
        assert tn == RET_Q and qk_cols == 2 * RET_Q and col0 == 0
        rope_tiles = qk_cols // tn
        in_specs += [pl.BlockSpec((tm, RET_DK), lambda i, j: (i, 0))] * 2
        args += [cos_t, sin_t]
    return pl.pallas_call(
        functools.partial(_inproj_kernel, rope_tiles=rope_tiles),
        grid=(t // tm, ncols // tn),
        in_specs=in_specs,
        out_specs=pl.BlockSpec((tm, tn), lambda i, j: (i, j)),
        out_shape=jax.ShapeDtypeStruct((t, ncols), BF16),
        scratch_shapes=[pltpu.VMEM((tm, d), BF16)],
        compiler_params=_cparams("parallel", "arbitrary"),
        name="in_proj",
    )(*args)


def _rope_tables(t):
    half = RET_DK // 2
    n_rows = t // GRID_W
    inv = ROPE_BASE ** (-jnp.arange(0, half, 2, dtype=F32) / half)
    ang_r = jnp.arange(n_rows, dtype=jnp.int32).astype(F32)[:, None] * inv[None, :]
    ang_c = jnp.arange(GRID_W, dtype=jnp.int32).astype(F32)[:, None] * inv[None, :]
    by_row = lambda v: jnp.repeat(v, GRID_W, axis=0)
    by_col = lambda v: jnp.tile(v, (n_rows, 1))
    cos_r, sin_r, cos_c, sin_c = jnp.cos(ang_r), jnp.sin(ang_r), jnp.cos(ang_c), jnp.sin(ang_c)
    cos_t = jnp.concatenate([by_row(cos_r)] * 2 + [by_col(cos_c)] * 2, axis=-1)
    sin_t = jnp.concatenate([by_row(-sin_r), by_row(sin_r), by_col(-sin_c), by_col(sin_c)], axis=-1)
    return cos_t, sin_t


def _ctx_state_kernel(lg_ref, k_ref, v_ref, o_ref):
    h = pl.program_id(0)
    lc = k_ref.shape[0]
    m = lax.broadcasted_iota(jnp.int32, (lc, 1), 0).astype(F32)
    k = k_ref[...].astype(F32)
    v = v_ref[...]
    w_f = jnp.exp(lg_ref[0, h] * (lc - 1.0 - m))
    w_b = jnp.exp(lg_ref[1, h] * m)
    o_ref[0, 0] = _dot_t0((k * w_f).astype(BF16), v)
    o_ref[1, 0] = _dot_t0((k * w_b).astype(BF16), v)


def _ctx_states(pkv, lg):
    lc = pkv.shape[0]
    return pl.pallas_call(
        _ctx_state_kernel,
        grid=(RET_HEADS,),
        in_specs=[pl.BlockSpec(memory_space=pltpu.SMEM),
                  pl.BlockSpec((lc, RET_DK), lambda h: (0, h)),
                  pl.BlockSpec((lc, RET_DV), lambda h: (0, RET_Q // RET_DV + h))],
        out_specs=pl.BlockSpec((2, 1, RET_DK, RET_DV), lambda h: (0, h, 0, 0)),
        out_shape=jax.ShapeDtypeStruct((2, RET_HEADS, RET_DK, RET_DV), F32),
        compiler_params=_cparams("arbitrary"),
        name="ctx_states",
    )(lg, pkv, pkv)


def _ret_kernel(lg_ref, q_ref, k_ref, v_ref, g_ref, st0_ref, gng_ref, gnb_ref,
                o_ref, sb_all, sf_scr, sb_scr, *, nb, bc):
    c_len = RET_CHUNK
    h = pl.program_id(0)
    i = pl.program_id(1)
    lgf = lg_ref[0, h]
    lgb = lg_ref[1, h]
    pos = lax.broadcasted_iota(jnp.int32, (c_len, 1), 0).astype(F32)
    ones_row = jnp.ones((1, RET_DV), F32)

    @pl.when(i == 0)
    def _():
        sf_scr[...] = st0_ref[0, 0]
        sb_scr[...] = st0_ref[1, 0]

    @pl.when(i < nb)
    def _():
        rb = nb - 1 - i
        zeta_b = jnp.exp(lgb * pos)
        cd_b = jnp.exp((lgb * c_len) * ones_row)

        s_b = sb_scr[...]
        for c in reversed(range(bc)):
            rows = slice(c * c_len, (c + 1) * c_len)
            sb_all[rb * bc + c] = s_b.astype(BF16)
            kz = (k_ref[rows, :].astype(F32) * zeta_b).astype(BF16)
            s_b = s_b * cd_b + _dot_t0(kz, v_ref[rows, :])
        sb_scr[...] = s_b

    @pl.when(i >= nb)
    def _():
        rb = i - nb
        a = lax.broadcasted_iota(jnp.int32, (c_len, c_len), 0)
        b = lax.broadcasted_iota(jnp.int32, (c_len, c_len), 1)
        d = (a - b).astype(F32)
        decay = jnp.where(d > 0, jnp.exp(lgf * jnp.maximum(d, 0.0)),
                          jnp.where(d < 0, jnp.exp(lgb * jnp.maximum(-d, 0.0)), 2.0))
        xi_f = jnp.exp(lgf * (pos + 1.0))
        xi_b = jnp.exp(lgb * (c_len - pos))
        zeta_f = jnp.exp(lgf * (c_len - 1.0 - pos))
        cd_f = jnp.exp((lgf * c_len) * ones_row)

        s_f = sf_scr[...]
        for c in range(bc):
            rows = slice(c * c_len, (c + 1) * c_len)
            q = q_ref[rows, :]
            k = k_ref[rows, :]
            v = v_ref[rows, :]
            s = lax.dot_general(q, k, (((1,), (1,)), ((), ())), preferred_element_type=F32) * decay
            y = (_dot(s.astype(BF16), v)
                 + xi_f * _dot(q, s_f.astype(BF16))
                 + xi_b * _dot(q, sb_all[rb * bc + c]))
            kz = (k.astype(F32) * zeta_f).astype(BF16)
            s_f = s_f * cd_f + _dot_t0(kz, v)
            mu = jnp.mean(y, axis=-1, keepdims=True)
            yc = y - mu
            var = jnp.mean(yc * yc, axis=-1, keepdims=True)
            yn = yc * lax.rsqrt(var + EPS) * gng_ref[...] + gnb_ref[...]
            o_ref[rows, :] = (_silu(g_ref[rows, :].astype(F32)) * yn).astype(BF16)
        sf_scr[...] = s_f


def _retention(p, st0, lg, gn_g, gn_b, q_off, k_off, v_off, g_off, bc=16):
    t = p.shape[0]
    bc = min(bc, t // RET_CHUNK)
    rows = bc * RET_CHUNK
    nb = t // rows

    def kv_rb(i):
        return jnp.where(i < nb, nb - 1 - i, i - nb)

    def q_rb(i):
        return jnp.maximum(i - nb, 0)

    return pl.pallas_call(
        functools.partial(_ret_kernel, nb=nb, bc=bc),
        grid=(RET_HEADS, 2 * nb),
        in_specs=[pl.BlockSpec(memory_space=pltpu.SMEM),
                  pl.BlockSpec((rows, RET_DK), lambda h, i: (q_rb(i), q_off // RET_DK + h)),
                  pl.BlockSpec((rows, RET_DK), lambda h, i: (kv_rb(i), k_off // RET_DK + h)),
                  pl.BlockSpec((rows, RET_DV), lambda h, i: (kv_rb(i), v_off // RET_DV + h)),
                  pl.BlockSpec((rows, RET_DV), lambda h, i: (q_rb(i), g_off // RET_DV + h)),
                  pl.BlockSpec((2, 1, RET_DK, RET_DV), lambda h, i: (0, h, 0, 0)),
                  pl.BlockSpec((1, RET_DV), lambda h, i: (0, h)),
                  pl.BlockSpec((1, RET_DV), lambda h, i: (0, h))],
        out_specs=pl.BlockSpec((rows, RET_DV), lambda h, i: (q_rb(i), h)),
        out_shape=jax.ShapeDtypeStruct((t, RET_V), BF16),
        scratch_shapes=[pltpu.VMEM((t // RET_CHUNK, RET_DK, RET_DV), BF16),
                        pltpu.VMEM((RET_DK, RET_DV), F32),
                        pltpu.VMEM((RET_DK, RET_DV), F32)],
        compiler_params=_cparams("arbitrary", "arbitrary"),
        name="retention",
    )(lg, p, p, p, p, st0, gn_g.reshape(1, RET_V), gn_b.reshape(1, RET_V))


def _hy_pre_kernel(m0, m1, m2, p0, p1, p2, n0, n1, n2, w_ref, b_ref, zx_ref):
    i = pl.program_id(0)
    tm, ch = m0.shape
    row = lax.broadcasted_iota(jnp.int32, (tm, 1), 0)
    has_prev = (i > 0).astype(F32)
    has_next = (i < pl.num_programs(0) - 1).astype(F32)
    halo = p0.shape[0]

    def conv(main, prev, nxt, part):
        cols = slice(part * ch, (part + 1) * ch)
        u = main[...].astype(F32)
        before = jnp.where(row == 0, prev[halo - 1:halo, :].astype(F32) * has_prev, pltpu.roll(u, 1, 0))
        after = jnp.where(row == tm - 1, nxt[0:1, :].astype(F32) * has_next, pltpu.roll(u, tm - 1, 0))
        return (before * w_ref[0:1, cols] + u * w_ref[1:2, cols] + after * w_ref[2:3, cols]
                + b_ref[0:1, cols])

    x0 = conv(m0, p0, n0, 0)
    z = conv(m1, p1, n1, 1) * conv(m2, p2, n2, 2)
    zx_ref[...] = _pack2(z, x0)


def _hy_pre(p, conv_w, conv_b, hy_off, ch, tm=256, halo=16):
    t = p.shape[0]
    tm = min(tm, t)
    nh = tm // halo
    last = t // halo - 1
    cb0 = hy_off // ch

    def main(part):
        return pl.BlockSpec((tm, ch), lambda i: (i, cb0 + part))

    def prev(part):
        return pl.BlockSpec((halo, ch), lambda i: (jnp.maximum(i * nh - 1, 0), cb0 + part))

    def nxt(part):
        return pl.BlockSpec((halo, ch), lambda i: (jnp.minimum((i + 1) * nh, last), cb0 + part))

    return pl.pallas_call(
        _hy_pre_kernel,
        grid=(t // tm,),
        in_specs=[main(0), main(1), main(2), prev(0), prev(1), prev(2), nxt(0), nxt(1), nxt(2),
                  pl.BlockSpec((3, 3 * ch), lambda i: (0, 0)),
                  pl.BlockSpec((1, 3 * ch), lambda i: (0, 0))],
        out_specs=pl.BlockSpec((tm, ch), lambda i: (i, 0)),
        out_shape=jax.ShapeDtypeStruct((t, ch), U32),
        compiler_params=_cparams("parallel"),
        name="hy_conv3",
    )(p, p, p, p, p, p, p, p, p, conv_w, conv_b.reshape(1, 3 * ch))


def _filt_ffn_kernel(f_ref, w1, b1, w2, b2, w3, b3, fr, o_ref):
    hp = lax.Precision.HIGHEST
    freq = fr[:, 0:1]
    h = jnp.sin(freq * (jnp.dot(w1[...], f_ref[...], precision=hp, preferred_element_type=F32) + b1[:, 0:1]))
    h = jnp.sin(freq * (jnp.dot(w2[...], h, precision=hp, preferred_element_type=F32) + b2[:, 0:1]))
    o_ref[...] = jnp.sin(freq * (jnp.dot(w3[...], h, precision=hp, preferred_element_type=F32) + b3[:, 0:1]))


def _filter_ffn(feat_t, w1, b1, w2, b2, w3, b3, freq, tl=2048):
    e, length = feat_t.shape
    ff = w1.shape[1]
    tl = min(tl, length)

    def col(v):
        return jnp.broadcast_to(v.reshape(ff, 1), (ff, 128))

    full = lambda shape: pl.BlockSpec(shape, lambda i: (0, 0))
    return pl.pallas_call(
        _filt_ffn_kernel,
        grid=(length // tl,),
        in_specs=[pl.BlockSpec((e, tl), lambda i: (0, i)),
                  full((ff, e)), full((ff, 128)), full((ff, ff)), full((ff, 128)),
                  full((ff, ff)), full((ff, 128)), full((ff, 128))],
        out_specs=pl.BlockSpec((ff, tl), lambda i: (0, i)),
        out_shape=jax.ShapeDtypeStruct((ff, length), F32),
        compiler_params=_cparams("parallel"),
        name="filter_ffn",
    )(feat_t, w1.T, col(b1), w2.T, col(b2), w3.T, col(b3), col(freq))


def _filt_taps_kernel(hf_ref, hb_ref, tf_ref, tb_ref, wf_ref, wb_ref, dec_ref, hp_ref, ssq_ref):
    i = pl.program_id(0)
    dec = jnp.abs(dec_ref[...])
    fwd = _dot_t0(hf_ref[...].astype(BF16), wf_ref[...]) * jnp.exp(-tf_ref[:, 0:1] * dec)
    bwd = -_dot_t0(hb_ref[...].astype(BF16), wb_ref[...]) * jnp.exp(-tb_ref[:, 0:1] * dec)
    row = lax.broadcasted_iota(jnp.int32, (bwd.shape[0], 1), 0)
    bwd = jnp.where((row == 0) & (i == 0), 0.0, bwd)

    @pl.when(i == 0)
    def _():
        ssq_ref[...] = jnp.zeros_like(ssq_ref)

    ssq_ref[...] += jnp.sum(fwd * fwd + bwd * bwd, axis=0, keepdims=True)
    hp_ref[...] = _pack2(fwd, bwd)


def _filter_taps(h_ext, t_ext, w4, decay, tr=512):
    ff, n = h_ext.shape
    length = n // 2
    ch = decay.shape[0]
    tr = min(tr, length)
    nt = length // tr
    return pl.pallas_call(
        _filt_taps_kernel,
        grid=(nt,),
        in_specs=[pl.BlockSpec((ff, tr), lambda i: (0, i)),
                  pl.BlockSpec((ff, tr), lambda i: (0, i + nt)),
                  pl.BlockSpec((tr, 1), lambda i: (i, 0)),
                  pl.BlockSpec((tr, 1), lambda i: (i + nt, 0)),
                  pl.BlockSpec((ff, ch), lambda i: (0, 0)),
                  pl.BlockSpec((ff, ch), lambda i: (0, 1)),
                  pl.BlockSpec((1, ch), lambda i: (0, 0))],
        out_specs=[pl.BlockSpec((tr, ch), lambda i: (i, 0)),
                   pl.BlockSpec((1, ch), lambda i: (0, 0))],
        out_shape=[jax.ShapeDtypeStruct((length, ch), U32), jax.ShapeDtypeStruct((1, ch), F32)],
        compiler_params=_cparams("arbitrary"),
        name="filter_taps",
    )(h_ext, h_ext, t_ext, t_ext, w4, w4, decay.reshape(1, ch))


def _wide_spec(rows):
    return pl.BlockSpec((rows, None, SUBLANES, DFT_WIDTH), lambda g, j: (0, g, 0, j))


def _flat_scratch(rows, arrays=1):
    return [pltpu.VMEM((rows * SUBLANES, LANES), U32)] * (arrays * DFT_WIDTH // LANES)


def _flatten(block, flats):
    for s, f in enumerate(flats):
        f[...] = block[:, :, s * LANES:(s + 1) * LANES].reshape(f.shape)


def _gather_rows(flats, a):
    return jnp.concatenate([f[pl.ds(a, f.shape[0] // SUBLANES, stride=SUBLANES), :] for f in flats], axis=1)


def _lane_groups(flats):
    per = DFT_SLABS
    return [(slice(i * per * LANES, (i + 1) * per * LANES), flats[i * per:(i + 1) * per])
            for i in range(len(flats) // per)]


def _dft_a_kernel(x_ref, e_ref, o_ref, *flats, both_halves):
    _flatten(x_ref, flats)
    for lanes, fl in _lane_groups(flats):
        for a in range(SUBLANES):
            hi, lo = _unpack2(_gather_rows(fl, a))
            x = jnp.concatenate([hi, lo], axis=0) if both_halves else hi
            r = _dot(e_ref[a], x.astype(BF16))
            n = r.shape[0] // 2
            o_ref[a, :, lanes] = _pack2(r[:n], r[n:])


def _dft_stage_a(x, e_tab, both_halves):
    n1, m2, kdim = e_tab.shape
    length, c = x.shape
    k2 = length // n1
    assert kdim == (2 * k2 if both_halves else k2)
    return pl.pallas_call(
        functools.partial(_dft_a_kernel, both_halves=both_halves),
        grid=(n1 // SUBLANES, c // DFT_WIDTH),
        in_specs=[_wide_spec(k2), pl.BlockSpec((SUBLANES, m2, kdim), lambda g, j: (g, 0, 0))],
        out_specs=pl.BlockSpec((SUBLANES, m2 // 2, DFT_WIDTH), lambda g, j: (g, 0, j)),
        out_shape=jax.ShapeDtypeStruct((n1, m2 // 2, c), U32),
        scratch_shapes=_flat_scratch(k2),
        compiler_params=_cparams("parallel", "arbitrary"),
        name="dft_stage_a",
    )(x.reshape(k2, n1 // SUBLANES, SUBLANES, c), e_tab)


def _dft_mid_kernel(z_ref, h_ref, f_ref, finv_ref, o_ref, *flats):
    half = len(flats) // 2
    _flatten(z_ref, flats[:half])
    _flatten(h_ref, flats[half:])
    n = z_ref.shape[0]
    for (lanes, fz), (_, fh) in zip(_lane_groups(flats[:half]), _lane_groups(flats[half:])):
        for a in range(SUBLANES):
            zr, zi = _unpack2(_gather_rows(fz, a))
            hr, hi = _unpack2(_gather_rows(fh, a))
            xs = _dot(f_ref[...], jnp.concatenate([zr, zi], axis=0).astype(BF16))
            ks = _dot(f_ref[...], jnp.concatenate([hr, hi], axis=0).astype(BF16))
            xr, xi, kr, ki = xs[:n], xs[n:], ks[:n], ks[n:]
            y = jnp.concatenate([xr * kr - xi * ki, xr * ki + xi * kr], axis=0).astype(BF16)
            cs = _dot(finv_ref[...], y)
            o_ref[a, :, lanes] = _pack2(cs[:n], cs[n:])


def _dft_mid(bz, bh, f_fwd, f_inv):
    n1, n2h, c = bz.shape
    mat = pl.BlockSpec((2 * n1, 2 * n1), lambda g, j: (0, 0))
    view = lambda b: b.reshape(n1, n2h // SUBLANES, SUBLANES, c)
    return pl.pallas_call(
        _dft_mid_kernel,
        grid=(n2h // SUBLANES, c // DFT_WIDTH),
        in_specs=[_wide_spec(n1), _wide_spec(n1), mat, mat],
        out_specs=pl.BlockSpec((SUBLANES, n1, DFT_WIDTH), lambda g, j: (g, 0, j)),
        out_shape=jax.ShapeDtypeStruct((n2h, n1, c), U32),
        scratch_shapes=_flat_scratch(n1, arrays=2),
        compiler_params=_cparams("parallel", "arbitrary"),
        name="dft_mid",
    )(view(bz), view(bh), f_fwd, f_inv)


def _dft_ainv_kernel(c_ref, zx_ref, e_ref, skip_ref, sc_ref, o_ref, *flats):
    half = len(flats) // 2
    _flatten(c_ref, flats[:half])
    _flatten(zx_ref, flats[half:])
    for (lanes, fc), (_, fzx) in zip(_lane_groups(flats[:half]), _lane_groups(flats[half:])):
        for a in range(SUBLANES):
            c_r, c_i = _unpack2(_gather_rows(fc, a))
            y = _dot_t0(e_ref[a], jnp.concatenate([c_r, c_i], axis=0).astype(BF16)) * sc_ref[:, lanes]
            z, x0 = _unpack2(_gather_rows(fzx, a))
            o_ref[:, a, lanes] = x0 * (y + z * skip_ref[:, lanes])


def _dft_stage_a_inv(cc, e_tab, zx, skip, scale):
    n2h, n1, c = cc.shape
    length = zx.shape[0]
    k2 = length // n1
    assert e_tab.shape == (n1, 2 * n2h, k2)
    row = pl.BlockSpec((1, DFT_WIDTH), lambda g, j: (0, j))
    out = pl.pallas_call(
        _dft_ainv_kernel,
        grid=(n1 // SUBLANES, c // DFT_WIDTH),
        in_specs=[_wide_spec(n2h), _wide_spec(k2),
                  pl.BlockSpec((SUBLANES, 2 * n2h, k2), lambda g, j: (g, 0, 0)), row, row],
        out_specs=_wide_spec(k2),
        out_shape=jax.ShapeDtypeStruct((k2, n1 // SUBLANES, SUBLANES, c), F32),
        scratch_shapes=_flat_scratch(n2h) + _flat_scratch(k2),
        compiler_params=_cparams("parallel", "arbitrary"),
        name="dft_stage_a_inv",
    )(cc.reshape(n2h, n1 // SUBLANES, SUBLANES, c), zx.reshape(k2, n1 // SUBLANES, SUBLANES, c),
      e_tab, skip.reshape(1, c), scale)
    return out.reshape(length, c)


def _dft_tables(n1, n2h):
    n2 = 2 * n2h
    n = n1 * n2
    a = jnp.arange(n1, dtype=jnp.int32)
    b = jnp.arange(n2, dtype=jnp.int32)
    odd = 2 * jnp.arange(n2h, dtype=jnp.int32) + 1
    alpha = ((a[:, None] * odd[None, :]) % (2 * n)).astype(F32) * (math.pi / n)
    beta = ((odd[:, None] * b[None, :]) % (2 * n2)).astype(F32) * (math.pi / n2)
    ar, ai = jnp.cos(alpha)[:, :, None], -jnp.sin(alpha)[:, :, None]
    br, bi = jnp.cos(beta)[None], -jnp.sin(beta)[None]
    er = ar * br - ai * bi
    ei = ar * bi + ai * br
    e_full = jnp.concatenate([er, ei], axis=1).astype(BF16)
    e_half = e_full[:, :, :n2h]
    phi = ((a[:, None] * a[None, :]) % n1).astype(F32) * (2.0 * math.pi / n1)
    fr, fi = jnp.cos(phi), -jnp.sin(phi)
    f_fwd = jnp.concatenate([jnp.concatenate([fr, -fi], axis=1),
                             jnp.concatenate([fi, fr], axis=1)], axis=0).astype(BF16)
    f_inv = jnp.concatenate([jnp.concatenate([fr, fi], axis=1),
                             jnp.concatenate([-fi, fr], axis=1)], axis=0).astype(BF16)
    return e_half, e_full, f_fwd, f_inv


def _hyena_features(length):
    r = jnp.arange(2 * length, dtype=jnp.int32)
    pos = jnp.where(r < length, r, 2 * length - r)
    t = jnp.linspace(0.0, 1.0, length, dtype=F32)[jnp.minimum(pos, length - 1)]
    w = 2.0 * math.pi * pos.astype(F32)[None, :] / length
    f = jnp.linspace(1e-4, HY_BANDS - 1.0, HY_BANDS, dtype=F32)[:, None]
    return t[:, None], jnp.concatenate([t[None, :], jnp.cos(f * w), -jnp.sin(f * w)], axis=0)


def _merge_kernel(x_ref, yr_ref, hy_ref, gr_ref, gh_ref, gate_ref, wr_ref, wh_ref, wo_ref, o_ref):
    ret_out = _dot(yr_ref[...], wr_ref[...])
    hy_out = _dot(hy_ref[...].astype(BF16), wh_ref[...])
    m = (jax.nn.sigmoid(gr_ref[...].astype(F32)) * ret_out
         + jax.nn.sigmoid(gh_ref[...].astype(F32)) * hy_out)
    o_ref[...] = x_ref[...] + gate_ref[...] * _dot(m.astype(BF16), wo_ref[...])


def _merge(x, y_ret, hy_in, p, gate_off, gate, w_ret_o, w_hy_o, w_out, tm=256):
    t, d = x.shape
    tm = min(tm, t)
    gb = gate_off // d
    tile = lambda: pl.BlockSpec((tm, d), lambda i: (i, 0))
    weight = lambda w: pl.BlockSpec((None,) + w.shape[1:], lambda i: (0, 0, 0), pipeline_mode=pl.Buffered(1))
    return pl.pallas_call(
        _merge_kernel,
        grid=(t // tm,),
        in_specs=[tile(), tile(), tile(),
                  pl.BlockSpec((tm, d), lambda i: (i, gb)),
                  pl.BlockSpec((tm, d), lambda i: (i, gb + 1)),
                  pl.BlockSpec((1, d), lambda i: (0, 0)),
                  weight(w_ret_o), weight(w_hy_o), weight(w_out)],
        out_specs=tile(),
        out_shape=jax.ShapeDtypeStruct((t, d), F32),
        compiler_params=_cparams("parallel"),
        name="merge_out",
    )(x, y_ret, hy_in, p, p, gate.reshape(1, d), w_ret_o, w_hy_o, w_out)


def _mod_rows(mods, first):
    return jnp.pad(mods[first:first + 3], ((0, 5), (0, 0)))


def kernel(x, c, ctx, c_ctx, w_ada, b_ada, norm_g, ffn_up, ffn_down, w_in, ret_log_gamma, ret_gn_g,
           ret_gn_b, w_ret_o, hy_conv_w, hy_conv_b, hy_ff_w1, hy_ff_b1, hy_ff_w2, hy_ff_b2, hy_ff_w3,
           hy_ff_b3, hy_ff_w4, hy_sin_freq, hy_decay, hy_bias, w_hy_o, w_out, final_norm_g):
    batch, t, d = x.shape
    assert batch == 1 and w_ada.shape[0] == 1, "single sample, single layer"
    hy_w = hy_decay.shape[1]
    q_off, k_off = 0, RET_Q
    v_off = k_off + RET_Q
    g_off = v_off + RET_V
    hy_off = g_off + RET_V
    gate_off = hy_off + 3 * hy_w
    assert w_in.shape[2] == gate_off + 2 * d and hy_w == d

    xs, cs = x[0], ctx[0]
    up, down, w_in_b = ffn_up.astype(BF16), ffn_down.astype(BF16), w_in.astype(BF16)
    lg = ret_log_gamma[0].astype(F32)

    cv = jnp.concatenate([c, c_ctx[None, :], jnp.zeros((6, d), F32)], axis=0)
    mods = _adaln(cv, w_ada[0], b_ada[0]).reshape(8, N_MOD, d)
    mx, mc = mods[0], mods[1]

    xs = _ffn_half(xs, _mod_rows(mx, 0), norm_g[0, 0], up, down, 0)
    cs = _ffn_half(cs, _mod_rows(mc, 0), norm_g[0, 0], up, down, 0)

    pkv = _in_proj(cs, _mod_rows(mc, 3), norm_g[0, 1], w_in_b, k_off, g_off - k_off)
    st0 = _ctx_states(pkv, lg)

    cos_t, sin_t = _rope_tables(t)
    p = _in_proj(xs, _mod_rows(mx, 3), norm_g[0, 1], w_in_b, 0, w_in_b.shape[2],
                 rope=(cos_t, sin_t, v_off))

    y_ret = _retention(p, st0, lg, ret_gn_g[0], ret_gn_b[0], q_off, k_off, v_off, g_off)

    zx = _hy_pre(p, hy_conv_w[0], hy_conv_b[0], hy_off, hy_w)
    t_ext, feat_t = _hyena_features(t)
    e_pad = (-feat_t.shape[0]) % 8
    feat_t = jnp.pad(feat_t, ((0, e_pad), (0, 0)))
    w1 = jnp.pad(hy_ff_w1[0], ((0, e_pad), (0, 0)))
    h_ext = _filter_ffn(feat_t, w1, hy_ff_b1[0], hy_ff_w2[0], hy_ff_b2[0], hy_ff_w3[0], hy_ff_b3[0],
                        hy_sin_freq[0])
    h_taps, ssq = _filter_taps(h_ext, t_ext, hy_ff_w4[0].astype(BF16), hy_decay[0])

    n1 = DFT_N1
    e_half, e_full, f_fwd, f_inv = _dft_tables(n1, t // n1)
    bz = _dft_stage_a(zx, e_half, both_halves=False)
    bh = _dft_stage_a(h_taps, e_full, both_halves=True)
    cc = _dft_mid(bz, bh, f_fwd, f_inv)
    scale = lax.rsqrt(ssq + EPS) * (2.0 / (2 * t))
    hy_in = _dft_stage_a_inv(cc, e_half, zx, hy_bias[0], scale)

    xs = _merge(xs, y_ret, hy_in, p, gate_off, mx[5], w_ret_o.astype(BF16), w_hy_o.astype(BF16),
                w_out.astype(BF16))

    out = _ffn_half(xs, _mod_rows(mx, 6), norm_g[0, 2], up, down, 1, final_g=final_norm_g)
    return out[None]
```

```python
import functools
import math

import jax
import jax.numpy as jnp
from jax import lax
from jax.experimental import pallas as pl
from jax.experimental.pallas import tpu as pltpu

F32 = jnp.float32
BF16 = jnp.bfloat16
U32 = jnp.uint32

N_MOD = 9
GRID_W = 64
RET_HEADS = 8
RET_DK = 128
RET_DV = 256
RET_CHUNK = 128
RET_Q = RET_HEADS * RET_DK
RET_V = RET_HEADS * RET_DV
HY_BANDS = 16
ROPE_BASE = 10000.0
EPS = 1e-6

TAPS_COLS = 256
DFT_N1 = 128
SUBLANES = 8
LANES = 128
DFT_SLABS = 4
DFT_WIDTH = 1024
VMEM_BYTES = 64 * 1024 * 1024
VMEM_LIMIT = VMEM_BYTES - 8 * 1024 * 1024
VMEM_LIMIT_FFN = VMEM_BYTES - 4 * 1024 * 1024


def _cparams(*sem, vmem=VMEM_LIMIT):
    return pltpu.CompilerParams(dimension_semantics=sem, vmem_limit_bytes=vmem)


def _dot(a, b):
    return jnp.dot(a, b, preferred_element_type=F32)


def _dot_t0(a, b):
    return lax.dot_general(a, b, (((0,), (0,)), ((), ())), preferred_element_type=F32)


def _silu(v):
    return v * jax.nn.sigmoid(v)


def _norm_mod(x, g, shift, scale):
    return x * lax.rsqrt(jnp.mean(x * x, axis=-1, keepdims=True) + EPS) * (g * (1.0 + scale)) + shift


def _pack2(hi, lo):
    hb = lax.bitcast_convert_type(hi.astype(BF16).astype(F32), U32)
    lb = lax.bitcast_convert_type(lo.astype(BF16).astype(F32), U32)
    return hb | (lb >> 16)


def _unpack2(w):
    hi = lax.bitcast_convert_type(w & jnp.uint32(0xFFFF0000), F32)
    lo = lax.bitcast_convert_type(w << 16, F32)
    return hi, lo


def _adaln_kernel(c_ref, w_ref, b_ref, o_ref):
    s = _silu(c_ref[...]).astype(BF16)
    o_ref[...] = _dot(s, w_ref[...].astype(BF16)) + b_ref[...]


def _adaln(cv, w, b, tn=1024):
    rows, d = cv.shape
    n = w.shape[1]
    return pl.pallas_call(
        _adaln_kernel,
        grid=(n // tn,),
        in_specs=[pl.BlockSpec((rows, d), lambda j: (0, 0)),
                  pl.BlockSpec((d, tn), lambda j: (0, j)),
                  pl.BlockSpec((1, tn), lambda j: (0, j))],
        out_specs=pl.BlockSpec((rows, tn), lambda j: (0, j)),
        out_shape=jax.ShapeDtypeStruct((rows, n), F32),
        compiler_params=_cparams("arbitrary"),
        name="adaln",
    )(cv, w, b.reshape(1, n))


def _ffn_kernel(x_ref, mod_ref, g_ref, wa_ref, wg_ref, wd_ref, *rest, final):
    if final:
        fg_ref, o_ref, h_scr = rest
    else:
        o_ref, h_scr = rest
    j = pl.program_id(1)

    @pl.when(j == 0)
    def _():
        h = _norm_mod(x_ref[...], g_ref[...], mod_ref[0:1, :], mod_ref[1:2, :])
        h_scr[...] = h.astype(BF16)

    h = h_scr[...]
    a = _dot(h, wa_ref[...])
    g = _dot(h, wg_ref[...])
    act = (_silu(a) * g).astype(BF16)

    @pl.when(j == 0)
    def _():
        o_ref[...] = _dot(act, wd_ref[...])

    @pl.when(j > 0)
    def _():
        o_ref[...] += _dot(act, wd_ref[...])

    @pl.when(j == pl.num_programs(1) - 1)
    def _():
        out = x_ref[...] + (0.5 * mod_ref[2:3, :]) * o_ref[...]
        if final:
            out = out * lax.rsqrt(jnp.mean(out * out, axis=-1, keepdims=True) + EPS) * fg_ref[...]
        o_ref[...] = out


def _ffn_half(x, mod, g, w_up, w_down, layer, final_g=None, tm=1024, tf=512):
    t, d = x.shape
    dff = w_down.shape[2]
    tm = min(tm, t)
    nf = dff // tf
    final = final_g is not None
    in_specs = [pl.BlockSpec((tm, d), lambda i, j: (i, 0), pipeline_mode=pl.Buffered(1)),
                pl.BlockSpec((8, d), lambda i, j: (0, 0)),
                pl.BlockSpec((1, d), lambda i, j: (0, 0)),
                pl.BlockSpec((None, None, d, tf), lambda i, j: (0, layer, 0, j)),
                pl.BlockSpec((None, None, d, tf), lambda i, j: (0, layer, 0, nf + j)),
                pl.BlockSpec((None, None, tf, d), lambda i, j: (0, layer, j, 0))]
    args = [x, mod, g.reshape(1, d), w_up, w_up, w_down]
    if final:
        in_specs.append(pl.BlockSpec((1, d), lambda i, j: (0, 0)))
        args.append(final_g.reshape(1, d))
    return pl.pallas_call(
        functools.partial(_ffn_kernel, final=final),
        grid=(t // tm, nf),
        in_specs=in_specs,
        out_specs=pl.BlockSpec((tm, d), lambda i, j: (i, 0)),
        out_shape=jax.ShapeDtypeStruct((t, d), F32),
        scratch_shapes=[pltpu.VMEM((tm, d), BF16)],
        compiler_params=_cparams("parallel", "arbitrary", vmem=VMEM_LIMIT_FFN),
        name="ffn_final" if final else "ffn_half",
    )(*args)


def _rope(x, cos, sin):
    lane = lax.broadcasted_iota(jnp.int32, x.shape, 1)
    partner = jnp.where((lane % 64) < 32, pltpu.roll(x, 96, 1), pltpu.roll(x, 32, 1))
    return x * cos + partner * sin


def _inproj_kernel(x_ref, mod_ref, g_ref, w_ref, *rest, rope_tiles):
    if rope_tiles:
        cos_ref, sin_ref, o_ref, h_scr = rest
    else:
        o_ref, h_scr = rest
    j = pl.program_id(1)

    @pl.when(j == 0)
    def _():
        h = _norm_mod(x_ref[...], g_ref[...], mod_ref[0:1, :], mod_ref[1:2, :])
        h_scr[...] = h.astype(BF16)

    if rope_tiles:
        @pl.when(j < rope_tiles)
        def _():
            acc = _dot(h_scr[...], w_ref[...])
            scale = jnp.where(j == 0, RET_DK ** -0.5, 1.0)
            cos = cos_ref[...] * scale
            sin = sin_ref[...] * scale
            for hb in range(acc.shape[1] // RET_DK):
                cols = slice(hb * RET_DK, (hb + 1) * RET_DK)
                o_ref[:, cols] = _rope(acc[:, cols], cos, sin).astype(BF16)

        @pl.when(j >= rope_tiles)
        def _():
            o_ref[...] = _dot(h_scr[...], w_ref[...]).astype(BF16)
    else:
        o_ref[...] = _dot(h_scr[...], w_ref[...]).astype(BF16)


def _in_proj(x, mod, g, w, col0, ncols, rope=None, tm=1024, tn=1024):
    t, d = x.shape
    tm = min(tm, t)
    off = col0 // tn
    in_specs = [pl.BlockSpec((tm, d), lambda i, j: (i, 0)),
                pl.BlockSpec((8, d), lambda i, j: (0, 0)),
                pl.BlockSpec((1, d), lambda i, j: (0, 0)),
                pl.BlockSpec((None, d, tn), lambda i, j: (0, 0, j + off))]
    args = [x, mod, g.reshape(1, d), w]
    rope_tiles = 0
    if rope is not None:
        cos_t, sin_t, qk_cols = rope
        assert tn == RET_Q and qk_cols == 2 * RET_Q and col0 == 0
        rope_tiles = qk_cols // tn
        in_specs += [pl.BlockSpec((tm, RET_DK), lambda i, j: (i, 0))] * 2
        args += [cos_t, sin_t]
    return pl.pallas_call(
        functools.partial(_inproj_kernel, rope_tiles=rope_tiles),
        grid=(t // tm, ncols // tn),
        in_specs=in_specs,
        out_specs=pl.BlockSpec((tm, tn), lambda i, j: (i, j)),
        out_shape=jax.ShapeDtypeStruct((t, ncols), BF16),
        scratch_shapes=[pltpu.VMEM((tm, d), BF16)],
        compiler_params=_cparams("parallel", "arbitrary"),
        name="in_proj",
    )(*args)


def _rope_tables(t):
    half = RET_DK // 2
    n_rows = t // GRID_W
    inv = ROPE_BASE ** (-jnp.arange(0, half, 2, dtype=F32) / half)
    ang_r = jnp.arange(n_rows, dtype=jnp.int32).astype(F32)[:, None] * inv[None, :]
    ang_c = jnp.arange(GRID_W, dtype=jnp.int32).astype(F32)[:, None] * inv[None, :]
    by_row = lambda v: jnp.repeat(v, GRID_W, axis=0)
    by_col = lambda v: jnp.tile(v, (n_rows, 1))
    cos_r, sin_r, cos_c, sin_c = jnp.cos(ang_r), jnp.sin(ang_r), jnp.cos(ang_c), jnp.sin(ang_c)
    cos_t = jnp.concatenate([by_row(cos_r)] * 2 + [by_col(cos_c)] * 2, axis=-1)
    sin_t = jnp.concatenate([by_row(-sin_r), by_row(sin_r), by_col(-sin_c), by_col(sin_c)], axis=-1)
    return cos_t, sin_t


def _ctx_state_kernel(lg_ref, k_ref, v_ref, o_ref):
    h = pl.program_id(0)
    lc = k_ref.shape[0]
    m = lax.broadcasted_iota(jnp.int32, (lc, 1), 0).astype(F32)
    k = k_ref[...].astype(F32)
    v = v_ref[...]
    w_f = jnp.exp(lg_ref[0, h] * (lc - 1.0 - m))
    w_b = jnp.exp(lg_ref[1, h] * m)
    o_ref[0, 0] = _dot_t0((k * w_f).astype(BF16), v)
    o_ref[1, 0] = _dot_t0((k * w_b).astype(BF16), v)


def _ctx_states(pkv, lg):
    lc = pkv.shape[0]
    return pl.pallas_call(
        _ctx_state_kernel,
        grid=(RET_HEADS,),
        in_specs=[pl.BlockSpec(memory_space=pltpu.SMEM),
                  pl.BlockSpec((lc, RET_DK), lambda h: (0, h)),
                  pl.BlockSpec((lc, RET_DV), lambda h: (0, RET_Q // RET_DV + h))],
        out_specs=pl.BlockSpec((2, 1, RET_DK, RET_DV), lambda h: (0, h, 0, 0)),
        out_shape=jax.ShapeDtypeStruct((2, RET_HEADS, RET_DK, RET_DV), F32),
        compiler_params=_cparams("arbitrary"),
        name="ctx_states",
    )(lg, pkv, pkv)


def _ret_kernel(lg_ref, q_ref, k_ref, v_ref, g_ref, st0_ref, gng_ref, gnb_ref,
                o_ref, sb_all, sf_scr, sb_scr, *, nb, bc):
    c_len = RET_CHUNK
    h = pl.program_id(0)
    i = pl.program_id(1)
    lgf = lg_ref[0, h]
    lgb = lg_ref[1, h]
    pos = lax.broadcasted_iota(jnp.int32, (c_len, 1), 0).astype(F32)
    ones_row = jnp.ones((1, RET_DV), F32)

    @pl.when(i == 0)
    def _():
        sf_scr[...] = st0_ref[0, 0]
        sb_scr[...] = st0_ref[1, 0]

    @pl.when(i < nb)
    def _():
        rb = nb - 1 - i
        zeta_b = jnp.exp(lgb * pos)
        cd_b = jnp.exp((lgb * c_len) * ones_row)

        s_b = sb_scr[...]
        for c in reversed(range(bc)):
            rows = slice(c * c_len, (c + 1) * c_len)
            sb_all[rb * bc + c] = s_b.astype(BF16)
            kz = (k_ref[rows, :].astype(F32) * zeta_b).astype(BF16)
            s_b = s_b * cd_b + _dot_t0(kz, v_ref[rows, :])
        sb_scr[...] = s_b

    @pl.when(i >= nb)
    def _():
        rb = i - nb
        a = lax.broadcasted_iota(jnp.int32, (c_len, c_len), 0)
        b = lax.broadcasted_iota(jnp.int32, (c_len, c_len), 1)
        d = (a - b).astype(F32)
        decay = jnp.where(d > 0, jnp.exp(lgf * jnp.maximum(d, 0.0)),
                          jnp.where(d < 0, jnp.exp(lgb * jnp.maximum(-d, 0.0)), 2.0))
        xi_f = jnp.exp(lgf * (pos + 1.0))
        xi_b = jnp.exp(lgb * (c_len - pos))
        zeta_f = jnp.exp(lgf * (c_len - 1.0 - pos))
        cd_f = jnp.exp((lgf * c_len) * ones_row)

        s_f = sf_scr[...]
        for c in range(bc):
            rows = slice(c * c_len, (c + 1) * c_len)
            q = q_ref[rows, :]
            k = k_ref[rows, :]
            v = v_ref[rows, :]
            s = lax.dot_general(q, k, (((1,), (1,)), ((), ())), preferred_element_type=F32) * decay
            y = (_dot(s.astype(BF16), v)
                 + xi_f * _dot(q, s_f.astype(BF16))
                 + xi_b * _dot(q, sb_all[rb * bc + c]))
            kz = (k.astype(F32) * zeta_f).astype(BF16)
            s_f = s_f * cd_f + _dot_t0(kz, v)
            mu = jnp.mean(y, axis=-1, keepdims=True)
            yc = y - mu
            var = jnp.mean(yc * yc, axis=-1, keepdims=True)
            yn = yc * lax.rsqrt(var + EPS) * gng_ref[...] + gnb_ref[...]
            o_ref[rows, :] = (_silu(g_ref[rows, :].astype(F32)) * yn).astype(BF16)
        sf_scr[...] = s_f


def _retention(p, st0, lg, gn_g, gn_b, q_off, k_off, v_off, g_off, bc=16):
    t = p.shape[0]
    bc = min(bc, t // RET_CHUNK)
    rows = bc * RET_CHUNK
    nb = t // rows

    def kv_rb(i):
        return jnp.where(i < nb, nb - 1 - i, i - nb)

    def q_rb(i):
        return jnp.maximum(i - nb, 0)

    return pl.pallas_call(
        functools.partial(_ret_kernel, nb=nb, bc=bc),
        grid=(RET_HEADS, 2 * nb),
        in_specs=[pl.BlockSpec(memory_space=pltpu.SMEM),
                  pl.BlockSpec((rows, RET_DK), lambda h, i: (q_rb(i), q_off // RET_DK + h)),
                  pl.BlockSpec((rows, RET_DK), lambda h, i: (kv_rb(i), k_off // RET_DK + h)),
                  pl.BlockSpec((rows, RET_DV), lambda h, i: (kv_rb(i), v_off // RET_DV + h)),
                  pl.BlockSpec((rows, RET_DV), lambda h, i: (q_rb(i), g_off // RET_DV + h)),
                  pl.BlockSpec((2, 1, RET_DK, RET_DV), lambda h, i: (0, h, 0, 0)),
                  pl.BlockSpec((1, RET_DV), lambda h, i: (0, h)),
                  pl.BlockSpec((1, RET_DV), lambda h, i: (0, h))],
        out_specs=pl.BlockSpec((rows, RET_DV), lambda h, i: (q_rb(i), h)),
        out_shape=jax.ShapeDtypeStruct((t, RET_V), BF16),
        scratch_shapes=[pltpu.VMEM((t // RET_CHUNK, RET_DK, RET_DV), BF16),
                        pltpu.VMEM((RET_DK, RET_DV), F32),
                        pltpu.VMEM((RET_DK, RET_DV), F32)],
        compiler_params=_cparams("arbitrary", "arbitrary"),
        name="retention",
    )(lg, p, p, p, p, st0, gn_g.reshape(1, RET_V), gn_b.reshape(1, RET_V))


def _hy_pre_kernel(m0, m1, m2, p0, p1, p2, n0, n1, n2, w_ref, b_ref, zx_ref):
    i = pl.program_id(0)
    tm, ch = m0.shape
    row = lax.broadcasted_iota(jnp.int32, (tm, 1), 0)
    has_prev = (i > 0).astype(F32)
    has_next = (i < pl.num_programs(0) - 1).astype(F32)
    halo = p0.shape[0]

    def conv(main, prev, nxt, part):
        cols = slice(part * ch, (part + 1) * ch)
        u = main[...].astype(F32)
        before = jnp.where(row == 0, prev[halo - 1:halo, :].astype(F32) * has_prev, pltpu.roll(u, 1, 0))
        after = jnp.where(row == tm - 1, nxt[0:1, :].astype(F32) * has_next, pltpu.roll(u, tm - 1, 0))
        return (before * w_ref[0:1, cols] + u * w_ref[1:2, cols] + after * w_ref[2:3, cols]
                + b_ref[0:1, cols])

    x0 = conv(m0, p0, n0, 0)
    z = conv(m1, p1, n1, 1) * conv(m2, p2, n2, 2)
    zx_ref[...] = _pack2(z, x0)


def _hy_pre(p, conv_w, conv_b, hy_off, ch, tm=256, halo=16):
    t = p.shape[0]
    tm = min(tm, t)
    nh = tm // halo
    last = t // halo - 1
    cb0 = hy_off // ch

    def main(part):
        return pl.BlockSpec((tm, ch), lambda i: (i, cb0 + part))

    def prev(part):
        return pl.BlockSpec((halo, ch), lambda i: (jnp.maximum(i * nh - 1, 0), cb0 + part))

    def nxt(part):
        return pl.BlockSpec((halo, ch), lambda i: (jnp.minimum((i + 1) * nh, last), cb0 + part))

    return pl.pallas_call(
        _hy_pre_kernel,
        grid=(t // tm,),
        in_specs=[main(0), main(1), main(2), prev(0), prev(1), prev(2), nxt(0), nxt(1), nxt(2),
                  pl.BlockSpec((3, 3 * ch), lambda i: (0, 0)),
                  pl.BlockSpec((1, 3 * ch), lambda i: (0, 0))],
        out_specs=pl.BlockSpec((tm, ch), lambda i: (i, 0)),
        out_shape=jax.ShapeDtypeStruct((t, ch), U32),
        compiler_params=_cparams("parallel"),
        name="hy_conv3",
    )(p, p, p, p, p, p, p, p, p, conv_w, conv_b.reshape(1, 3 * ch))


def _filt_ffn_kernel(f_ref, w1, b1, w2, b2, w3, b3, fr, o_ref):
    hp = lax.Precision.HIGHEST
    freq = fr[:, 0:1]
    h = jnp.sin(freq * (jnp.dot(w1[...], f_ref[...], precision=hp, preferred_element_type=F32) + b1[:, 0:1]))
    h = jnp.sin(freq * (jnp.dot(w2[...], h, precision=hp, preferred_element_type=F32) + b2[:, 0:1]))
    o_ref[...] = jnp.sin(freq * (jnp.dot(w3[...], h, precision=hp, preferred_element_type=F32) + b3[:, 0:1]))


def _filter_ffn(feat_t, w1, b1, w2, b2, w3, b3, freq, tl=2048):
    e, length = feat_t.shape
    ff = w1.shape[1]
    tl = min(tl, length)

    def col(v):
        return jnp.broadcast_to(v.reshape(ff, 1), (ff, 128))

    full = lambda shape: pl.BlockSpec(shape, lambda i: (0, 0))
    return pl.pallas_call(
        _filt_ffn_kernel,
        grid=(length // tl,),
        in_specs=[pl.BlockSpec((e, tl), lambda i: (0, i)),
                  full((ff, e)), full((ff, 128)), full((ff, ff)), full((ff, 128)),
                  full((ff, ff)), full((ff, 128)), full((ff, 128))],
        out_specs=pl.BlockSpec((ff, tl), lambda i: (0, i)),
        out_shape=jax.ShapeDtypeStruct((ff, length), F32),
        compiler_params=_cparams("parallel"),
        name="filter_ffn",
    )(feat_t, w1.T, col(b1), w2.T, col(b2), w3.T, col(b3), col(freq))


def _filt_taps_kernel(hf_ref, hb_ref, tf_ref, tb_ref, wf_ref, wb_ref, dec_ref, hp_ref, ssq_ref):
    i = pl.program_id(0)
    hf = hf_ref[...].astype(BF16)
    hb = hb_ref[...].astype(BF16)
    t_f = tf_ref[:, 0:1]
    t_b = tb_ref[:, 0:1]
    row = lax.broadcasted_iota(jnp.int32, t_b.shape, 0)
    sign_b = jnp.where((row == 0) & (i == 0), 0.0, -1.0)

    @pl.when(i == 0)
    def _():
        ssq_ref[...] = jnp.zeros_like(ssq_ref)

    width = TAPS_COLS
    for c0 in range(0, dec_ref.shape[1], width):
        cols = slice(c0, c0 + width)
        dec = jnp.abs(dec_ref[:, cols])
        fwd = _dot_t0(hf, wf_ref[:, cols]) * jnp.exp(-t_f * dec)
        bwd = _dot_t0(hb, wb_ref[:, cols]) * (jnp.exp(-t_b * dec) * sign_b)
        ssq_ref[:, cols] += jnp.sum(fwd * fwd + bwd * bwd, axis=0, keepdims=True)
        hp_ref[:, cols] = _pack2(fwd, bwd)


def _filter_taps(h_ext, t_ext, w4, decay, tr=512):
    ff, n = h_ext.shape
    length = n // 2
    ch = decay.shape[0]
    tr = min(tr, length)
    nt = length // tr
    return pl.pallas_call(
        _filt_taps_kernel,
        grid=(nt,),
        in_specs=[pl.BlockSpec((ff, tr), lambda i: (0, i)),
                  pl.BlockSpec((ff, tr), lambda i: (0, i + nt)),
                  pl.BlockSpec((tr, 1), lambda i: (i, 0)),
                  pl.BlockSpec((tr, 1), lambda i: (i + nt, 0)),
                  pl.BlockSpec((ff, ch), lambda i: (0, 0)),
                  pl.BlockSpec((ff, ch), lambda i: (0, 1)),
                  pl.BlockSpec((1, ch), lambda i: (0, 0))],
        out_specs=[pl.BlockSpec((tr, ch), lambda i: (i, 0)),
                   pl.BlockSpec((1, ch), lambda i: (0, 0))],
        out_shape=[jax.ShapeDtypeStruct((length, ch), U32), jax.ShapeDtypeStruct((1, ch), F32)],
        compiler_params=_cparams("arbitrary"),
        name="filter_taps",
    )(h_ext, h_ext, t_ext, t_ext, w4, w4, decay.reshape(1, ch))


def _wide_spec(rows):
    return pl.BlockSpec((rows, None, SUBLANES, DFT_WIDTH), lambda g, j: (0, g, 0, j))


def _flat_scratch(rows, arrays=1):
    return [pltpu.VMEM((rows * SUBLANES, LANES), U32)] * (arrays * DFT_WIDTH // LANES)


def _flatten(block, flats):
    for s, f in enumerate(flats):
        f[...] = block[:, :, s * LANES:(s + 1) * LANES].reshape(f.shape)


def _gather_rows(flats, a):
    return jnp.concatenate([f[pl.ds(a, f.shape[0] // SUBLANES, stride=SUBLANES), :] for f in flats], axis=1)


def _lane_groups(flats):
    per = DFT_SLABS
    return [(slice(i * per * LANES, (i + 1) * per * LANES), flats[i * per:(i + 1) * per])
            for i in range(len(flats) // per)]


def _dft_a_kernel(x_ref, e_ref, o_ref, *flats, both_halves):
    _flatten(x_ref, flats)
    for lanes, fl in _lane_groups(flats):
        for a in range(SUBLANES):
            hi, lo = _unpack2(_gather_rows(fl, a))
            x = jnp.concatenate([hi, lo], axis=0) if both_halves else hi
            r = _dot(e_ref[a], x.astype(BF16))
            n = r.shape[0] // 2
            o_ref[a, :, lanes] = _pack2(r[:n], r[n:])


def _dft_stage_a(x, e_tab, both_halves):
    n1, m2, kdim = e_tab.shape
    length, c = x.shape
    k2 = length // n1
    assert kdim == (2 * k2 if both_halves else k2)
    return pl.pallas_call(
        functools.partial(_dft_a_kernel, both_halves=both_halves),
        grid=(n1 // SUBLANES, c // DFT_WIDTH),
        in_specs=[_wide_spec(k2), pl.BlockSpec((SUBLANES, m2, kdim), lambda g, j: (g, 0, 0))],
        out_specs=pl.BlockSpec((SUBLANES, m2 // 2, DFT_WIDTH), lambda g, j: (g, 0, j)),
        out_shape=jax.ShapeDtypeStruct((n1, m2 // 2, c), U32),
        scratch_shapes=_flat_scratch(k2),
        compiler_params=_cparams("parallel", "arbitrary"),
        name="dft_stage_a",
    )(x.reshape(k2, n1 // SUBLANES, SUBLANES, c), e_tab)


def _dft_mid_kernel(z_ref, h_ref, f_ref, finv_ref, o_ref, *flats):
    half = len(flats) // 2
    _flatten(z_ref, flats[:half])
    _flatten(h_ref, flats[half:])
    n = z_ref.shape[0]
    for (lanes, fz), (_, fh) in zip(_lane_groups(flats[:half]), _lane_groups(flats[half:])):
        for a in range(SUBLANES):
            zr, zi = _unpack2(_gather_rows(fz, a))
            hr, hi = _unpack2(_gather_rows(fh, a))
            xs = _dot(f_ref[...], jnp.concatenate([zr, zi], axis=0).astype(BF16))
            ks = _dot(f_ref[...], jnp.concatenate([hr, hi], axis=0).astype(BF16))
            xr, xi, kr, ki = xs[:n], xs[n:], ks[:n], ks[n:]
            y = jnp.concatenate([xr * kr - xi * ki, xr * ki + xi * kr], axis=0).astype(BF16)
            cs = _dot(finv_ref[...], y)
            o_ref[a, :, lanes] = _pack2(cs[:n], cs[n:])


def _dft_mid(bz, bh, f_fwd, f_inv):
    n1, n2h, c = bz.shape
    mat = pl.BlockSpec((2 * n1, 2 * n1), lambda g, j: (0, 0))
    view = lambda b: b.reshape(n1, n2h // SUBLANES, SUBLANES, c)
    return pl.pallas_call(
        _dft_mid_kernel,
        grid=(n2h // SUBLANES, c // DFT_WIDTH),
        in_specs=[_wide_spec(n1), _wide_spec(n1), mat, mat],
        out_specs=pl.BlockSpec((SUBLANES, n1, DFT_WIDTH), lambda g, j: (g, 0, j)),
        out_shape=jax.ShapeDtypeStruct((n2h, n1, c), U32),
        scratch_shapes=_flat_scratch(n1, arrays=2),
        compiler_params=_cparams("parallel", "arbitrary"),
        name="dft_mid",
    )(view(bz), view(bh), f_fwd, f_inv)


def _dft_ainv_kernel(c_ref, zx_ref, e_ref, skip_ref, sc_ref, o_ref, *flats):
    half = len(flats) // 2
    _flatten(c_ref, flats[:half])
    _flatten(zx_ref, flats[half:])
    for (lanes, fc), (_, fzx) in zip(_lane_groups(flats[:half]), _lane_groups(flats[half:])):
        for a in range(SUBLANES):
            c_r, c_i = _unpack2(_gather_rows(fc, a))
            y = _dot_t0(e_ref[a], jnp.concatenate([c_r, c_i], axis=0).astype(BF16)) * sc_ref[:, lanes]
            z, x0 = _unpack2(_gather_rows(fzx, a))
            o_ref[:, a, lanes] = x0 * (y + z * skip_ref[:, lanes])


def _dft_stage_a_inv(cc, e_tab, zx, skip, scale):
    n2h, n1, c = cc.shape
    length = zx.shape[0]
    k2 = length // n1
    assert e_tab.shape == (n1, 2 * n2h, k2)
    row = pl.BlockSpec((1, DFT_WIDTH), lambda g, j: (0, j))
    out = pl.pallas_call(
        _dft_ainv_kernel,
        grid=(n1 // SUBLANES, c // DFT_WIDTH),
        in_specs=[_wide_spec(n2h), _wide_spec(k2),
                  pl.BlockSpec((SUBLANES, 2 * n2h, k2), lambda g, j: (g, 0, 0)), row, row],
        out_specs=_wide_spec(k2),
        out_shape=jax.ShapeDtypeStruct((k2, n1 // SUBLANES, SUBLANES, c), F32),
        scratch_shapes=_flat_scratch(n2h) + _flat_scratch(k2),
        compiler_params=_cparams("parallel", "arbitrary"),
        name="dft_stage_a_inv",
    )(cc.reshape(n2h, n1 // SUBLANES, SUBLANES, c), zx.reshape(k2, n1 // SUBLANES, SUBLANES, c),
      e_tab, skip.reshape(1, c), scale)
    return out.reshape(length, c)


def _dft_tables(n1, n2h):
    n2 = 2 * n2h
    n = n1 * n2
    a = jnp.arange(n1, dtype=jnp.int32)
    b = jnp.arange(n2, dtype=jnp.int32)
    odd = 2 * jnp.arange(n2h, dtype=jnp.int32) + 1
    alpha = ((a[:, None] * odd[None, :]) % (2 * n)).astype(F32) * (math.pi / n)
    beta = ((odd[:, None] * b[None, :]) % (2 * n2)).astype(F32) * (math.pi / n2)
    ar, ai = jnp.cos(alpha)[:, :, None], -jnp.sin(alpha)[:, :, None]
    br, bi = jnp.cos(beta)[None], -jnp.sin(beta)[None]
    er = ar * br - ai * bi
    ei = ar * bi + ai * br
    e_full = jnp.concatenate([er, ei], axis=1).astype(BF16)
    e_half = e_full[:, :, :n2h]
    phi = ((a[:, None] * a[None, :]) % n1).astype(F32) * (2.0 * math.pi / n1)
    fr, fi = jnp.cos(phi), -jnp.sin(phi)
    f_fwd = jnp.concatenate([jnp.concatenate([fr, -fi], axis=1),
                             jnp.concatenate([fi, fr], axis=1)], axis=0).astype(BF16)
    f_inv = jnp.concatenate([jnp.concatenate([fr, fi], axis=1),
                             jnp.concatenate([-fi, fr], axis=1)], axis=0).astype(BF16)
    return e_half, e_full, f_fwd, f_inv


def _hyena_features(length):
    r = jnp.arange(2 * length, dtype=jnp.int32)
    pos = jnp.where(r < length, r, 2 * length - r)
    t = pos.astype(F32) / (length - 1.0)
    w = 2.0 * math.pi * pos.astype(F32)[None, :] / length
    f = jnp.linspace(1e-4, HY_BANDS - 1.0, HY_BANDS, dtype=F32)[:, None]
    return t[:, None], jnp.concatenate([t[None, :], jnp.cos(f * w), -jnp.sin(f * w)], axis=0)


def _merge_kernel(x_ref, yr_ref, hy_ref, gr_ref, gh_ref, gate_ref, wr_ref, wh_ref, wo_ref, o_ref):
    ret_out = _dot(yr_ref[...], wr_ref[...])
    hy_out = _dot(hy_ref[...].astype(BF16), wh_ref[...])
    m = (jax.nn.sigmoid(gr_ref[...].astype(F32)) * ret_out
         + jax.nn.sigmoid(gh_ref[...].astype(F32)) * hy_out)
    o_ref[...] = x_ref[...] + gate_ref[...] * _dot(m.astype(BF16), wo_ref[...])


def _merge(x, y_ret, hy_in, p, gate_off, gate, w_ret_o, w_hy_o, w_out, tm=256):
    t, d = x.shape
    tm = min(tm, t)
    gb = gate_off // d
    tile = lambda: pl.BlockSpec((tm, d), lambda i: (i, 0))
    weight = lambda w: pl.BlockSpec((None,) + w.shape[1:], lambda i: (0, 0, 0), pipeline_mode=pl.Buffered(1))
    return pl.pallas_call(
        _merge_kernel,
        grid=(t // tm,),
        in_specs=[tile(), tile(), tile(),
                  pl.BlockSpec((tm, d), lambda i: (i, gb)),
                  pl.BlockSpec((tm, d), lambda i: (i, gb + 1)),
                  pl.BlockSpec((1, d), lambda i: (0, 0)),
                  weight(w_ret_o), weight(w_hy_o), weight(w_out)],
        out_specs=tile(),
        out_shape=jax.ShapeDtypeStruct((t, d), F32),
        compiler_params=_cparams("parallel"),
        name="merge_out",
    )(x, y_ret, hy_in, p, p, gate.reshape(1, d), w_ret_o, w_hy_o, w_out)


def _mod_rows(mods, first):
    return jnp.pad(mods[first:first + 3], ((0, 5), (0, 0)))


def kernel(x, c, ctx, c_ctx, w_ada, b_ada, norm_g, ffn_up, ffn_down, w_in, ret_log_gamma, ret_gn_g,
           ret_gn_b, w_ret_o, hy_conv_w, hy_conv_b, hy_ff_w1, hy_ff_b1, hy_ff_w2, hy_ff_b2, hy_ff_w3,
           hy_ff_b3, hy_ff_w4, hy_sin_freq, hy_decay, hy_bias, w_hy_o, w_out, final_norm_g):
    batch, t, d = x.shape
    assert batch == 1 and w_ada.shape[0] == 1, "single sample, single layer"
    hy_w = hy_decay.shape[1]
    q_off, k_off = 0, RET_Q
    v_off = k_off + RET_Q
    g_off = v_off + RET_V
    hy_off = g_off + RET_V
    gate_off = hy_off + 3 * hy_w
    assert w_in.shape[2] == gate_off + 2 * d and hy_w == d

    xs, cs = x[0], ctx[0]
    up, down, w_in_b = ffn_up.astype(BF16), ffn_down.astype(BF16), w_in.astype(BF16)
    lg = ret_log_gamma[0].astype(F32)

    cv = jnp.concatenate([c, c_ctx[None, :], jnp.zeros((6, d), F32)], axis=0)
    mods = _adaln(cv, w_ada[0], b_ada[0]).reshape(8, N_MOD, d)
    mx, mc = mods[0], mods[1]

    xs = _ffn_half(xs, _mod_rows(mx, 0), norm_g[0, 0], up, down, 0)
    cs = _ffn_half(cs, _mod_rows(mc, 0), norm_g[0, 0], up, down, 0)

    pkv = _in_proj(cs, _mod_rows(mc, 3), norm_g[0, 1], w_in_b, k_off, g_off - k_off)
    st0 = _ctx_states(pkv, lg)

    cos_t, sin_t = _rope_tables(t)
    p = _in_proj(xs, _mod_rows(mx, 3), norm_g[0, 1], w_in_b, 0, w_in_b.shape[2],
                 rope=(cos_t, sin_t, v_off))

    y_ret = _retention(p, st0, lg, ret_gn_g[0], ret_gn_b[0], q_off, k_off, v_off, g_off)

    zx = _hy_pre(p, hy_conv_w[0], hy_conv_b[0], hy_off, hy_w)
    t_ext, feat_t = _hyena_features(t)
    e_pad = (-feat_t.shape[0]) % 8
    feat_t = jnp.pad(feat_t, ((0, e_pad), (0, 0)))
    w1 = jnp.pad(hy_ff_w1[0], ((0, e_pad), (0, 0)))
    h_ext = _filter_ffn(feat_t, w1, hy_ff_b1[0], hy_ff_w2[0], hy_ff_b2[0], hy_ff_w3[0], hy_ff_b3[0],
                        hy_sin_freq[0])
    h_taps, ssq = _filter_taps(h_ext, t_ext, hy_ff_w4[0].astype(BF16), hy_decay[0])

    n1 = DFT_N1
    e_half, e_full, f_fwd, f_inv = _dft_tables(n1, t // n1)
    bz = _dft_stage_a(zx, e_half, both_halves=False)
    bh = _dft_stage_a(h_taps, e_full, both_halves=True)
    cc = _dft_mid(bz, bh, f_fwd, f_inv)
    scale = lax.rsqrt(ssq + EPS) * (2.0 / (2 * t))
    hy_in = _dft_stage_a_inv(cc, e_half, zx, hy_bias[0], scale)

    xs = _merge(xs, y_ret, hy_in, p, gate_off, mx[5], w_ret_o.astype(BF16), w_hy_o.astype(BF16),
                w_out.astype(BF16))

    out = _ffn_half(xs, _mod_rows(mx, 6), norm_g[0, 2], up, down, 1, final_g=final_norm_g)
    return out[None]
```

```python
import functools
import math

import jax
import jax.numpy as jnp
from jax import lax
from jax.experimental import pallas as pl
from jax.experimental.pallas import tpu as pltpu

F32 = jnp.float32
BF16 = jnp.bfloat16
U32 = jnp.uint32

N_MOD = 9
GRID_W = 64
RET_HEADS = 8
RET_DK = 128
RET_DV = 256
RET_CHUNK = 128
RET_Q = RET_HEADS * RET_DK
RET_V = RET_HEADS * RET_DV
HY_BANDS = 16
ROPE_BASE = 10000.0
EPS = 1e-6

TAPS_COLS = 256
DFT_N1 = 128
SUBLANES = 8
LANES = 128
DFT_SLABS = 4
DFT_WIDTH = 1024
VMEM_BYTES = 64 * 1024 * 1024
VMEM_LIMIT = VMEM_BYTES - 8 * 1024 * 1024


def _cparams(*sem, vmem=VMEM_LIMIT):
    return pltpu.CompilerParams(dimension_semantics=sem, vmem_limit_bytes=vmem)


def _dot(a, b):
    return jnp.dot(a, b, preferred_element_type=F32)


def _dot_t0(a, b):
    return lax.dot_general(a, b, (((0,), (0,)), ((), ())), preferred_element_type=F32)


def _silu(v):
    return v * jax.nn.sigmoid(v)


def _norm_mod(x, g, shift, scale):
    return x * lax.rsqrt(jnp.mean(x * x, axis=-1, keepdims=True) + EPS) * (g * (1.0 + scale)) + shift


def _pack2(hi, lo):
    hb = lax.bitcast_convert_type(hi.astype(BF16).astype(F32), U32)
    lb = lax.bitcast_convert_type(lo.astype(BF16).astype(F32), U32)
    return hb | (lb >> 16)


def _unpack2(w):
    hi = lax.bitcast_convert_type(w & jnp.uint32(0xFFFF0000), F32)
    lo = lax.bitcast_convert_type(w << 16, F32)
    return hi, lo


def _adaln_kernel(c_ref, w_ref, b_ref, o_ref):
    s = _silu(c_ref[...]).astype(BF16)
    o_ref[...] = _dot(s, w_ref[...].astype(BF16)) + b_ref[...]


def _adaln(cv, w, b, tn=1024):
    rows, d = cv.shape
    n = w.shape[1]
    return pl.pallas_call(
        _adaln_kernel,
        grid=(n // tn,),
        in_specs=[pl.BlockSpec((rows, d), lambda j: (0, 0)),
                  pl.BlockSpec((d, tn), lambda j: (0, j)),
                  pl.BlockSpec((1, tn), lambda j: (0, j))],
        out_specs=pl.BlockSpec((rows, tn), lambda j: (0, j)),
        out_shape=jax.ShapeDtypeStruct((rows, n), F32),
        compiler_params=_cparams("arbitrary"),
        name="adaln",
    )(cv, w, b.reshape(1, n))


def _ffn_kernel(x_ref, mod_ref, g_ref, wa_ref, wg_ref, wd_ref, *rest, final):
    if final:
        fg_ref, o_ref, h_scr = rest
    else:
        o_ref, h_scr = rest
    j = pl.program_id(1)

    @pl.when(j == 0)
    def _():
        h = _norm_mod(x_ref[...], g_ref[...], mod_ref[0:1, :], mod_ref[1:2, :])
        h_scr[...] = h.astype(BF16)

    h = h_scr[...]
    a = _dot(h, wa_ref[...])
    g = _dot(h, wg_ref[...])
    act = (_silu(a) * g).astype(BF16)

    @pl.when(j == 0)
    def _():
        o_ref[...] = _dot(act, wd_ref[...])

    @pl.when(j > 0)
    def _():
        o_ref[...] += _dot(act, wd_ref[...])

    @pl.when(j == pl.num_programs(1) - 1)
    def _():
        out = x_ref[...] + (0.5 * mod_ref[2:3, :]) * o_ref[...]
        if final:
            out = out * lax.rsqrt(jnp.mean(out * out, axis=-1, keepdims=True) + EPS) * fg_ref[...]
        o_ref[...] = out


def _ffn_half(x, mod, g, w_up, w_down, layer, final_g=None, tm=512, tf=512):
    t, d = x.shape
    dff = w_down.shape[2]
    tm = min(tm, t)
    nf = dff // tf
    final = final_g is not None
    in_specs = [pl.BlockSpec((tm, d), lambda i, j: (i, 0)),
                pl.BlockSpec((8, d), lambda i, j: (0, 0)),
                pl.BlockSpec((1, d), lambda i, j: (0, 0)),
                pl.BlockSpec((None, None, d, tf), lambda i, j: (0, layer, 0, j)),
                pl.BlockSpec((None, None, d, tf), lambda i, j: (0, layer, 0, nf + j)),
                pl.BlockSpec((None, None, tf, d), lambda i, j: (0, layer, j, 0))]
    args = [x, mod, g.reshape(1, d), w_up, w_up, w_down]
    if final:
        in_specs.append(pl.BlockSpec((1, d), lambda i, j: (0, 0)))
        args.append(final_g.reshape(1, d))
    return pl.pallas_call(
        functools.partial(_ffn_kernel, final=final),
        grid=(t // tm, nf),
        in_specs=in_specs,
        out_specs=pl.BlockSpec((tm, d), lambda i, j: (i, 0)),
        out_shape=jax.ShapeDtypeStruct((t, d), F32),
        scratch_shapes=[pltpu.VMEM((tm, d), BF16)],
        compiler_params=_cparams("parallel", "arbitrary"),
        name="ffn_final" if final else "ffn_half",
    )(*args)


def _rope(x, cos, sin):
    lane = lax.broadcasted_iota(jnp.int32, x.shape, 1)
    partner = jnp.where((lane % 64) < 32, pltpu.roll(x, 96, 1), pltpu.roll(x, 32, 1))
    return x * cos + partner * sin


def _inproj_kernel(x_ref, mod_ref, g_ref, w_ref, *rest, rope_tiles):
    if rope_tiles:
        cos_ref, sin_ref, o_ref, h_scr = rest
    else:
        o_ref, h_scr = rest
    j = pl.program_id(1)

    @pl.when(j == 0)
    def _():
        h = _norm_mod(x_ref[...], g_ref[...], mod_ref[0:1, :], mod_ref[1:2, :])
        h_scr[...] = h.astype(BF16)

    if rope_tiles:
        @pl.when(j < rope_tiles)
        def _():
            acc = _dot(h_scr[...], w_ref[...])
            scale = jnp.where(j == 0, RET_DK ** -0.5, 1.0)
            cos = cos_ref[...] * scale
            sin = sin_ref[...] * scale
            for hb in range(acc.shape[1] // RET_DK):
                cols = slice(hb * RET_DK, (hb + 1) * RET_DK)
                o_ref[:, cols] = _rope(acc[:, cols], cos, sin).astype(BF16)

        @pl.when(j >= rope_tiles)
        def _():
            o_ref[...] = _dot(h_scr[...], w_ref[...]).astype(BF16)
    else:
        o_ref[...] = _dot(h_scr[...], w_ref[...]).astype(BF16)


def _in_proj(x, mod, g, w, col0, ncols, rope=None, tm=1024, tn=1024):
    t, d = x.shape
    tm = min(tm, t)
    off = col0 // tn
    in_specs = [pl.BlockSpec((tm, d), lambda i, j: (i, 0)),
                pl.BlockSpec((8, d), lambda i, j: (0, 0)),
                pl.BlockSpec((1, d), lambda i, j: (0, 0)),
                pl.BlockSpec((None, d, tn), lambda i, j: (0, 0, j + off))]
    args = [x, mod, g.reshape(1, d), w]
    rope_tiles = 0
    if rope is not None:
        cos_t, sin_t, qk_cols = rope
        assert tn == RET_Q and qk_cols == 2 * RET_Q and col0 == 0
        rope_tiles = qk_cols // tn
        in_specs += [pl.BlockSpec((tm, RET_DK), lambda i, j: (i, 0))] * 2
        args += [cos_t, sin_t]
    return pl.pallas_call(
        functools.partial(_inproj_kernel, rope_tiles=rope_tiles),
        grid=(t // tm, ncols // tn),
        in_specs=in_specs,
        out_specs=pl.BlockSpec((tm, tn), lambda i, j: (i, j)),
        out_shape=jax.ShapeDtypeStruct((t, ncols), BF16),
        scratch_shapes=[pltpu.VMEM((tm, d), BF16)],
        compiler_params=_cparams("parallel", "arbitrary"),
        name="in_proj",
    )(*args)


def _rope_tables(t):
    half = RET_DK // 2
    n_rows = t // GRID_W
    inv = ROPE_BASE ** (-jnp.arange(0, half, 2, dtype=F32) / half)
    ang_r = jnp.arange(n_rows, dtype=jnp.int32).astype(F32)[:, None] * inv[None, :]
    ang_c = jnp.arange(GRID_W, dtype=jnp.int32).astype(F32)[:, None] * inv[None, :]
    by_row = lambda v: jnp.repeat(v, GRID_W, axis=0)
    by_col = lambda v: jnp.tile(v, (n_rows, 1))
    cos_r, sin_r, cos_c, sin_c = jnp.cos(ang_r), jnp.sin(ang_r), jnp.cos(ang_c), jnp.sin(ang_c)
    cos_t = jnp.concatenate([by_row(cos_r)] * 2 + [by_col(cos_c)] * 2, axis=-1)
    sin_t = jnp.concatenate([by_row(-sin_r), by_row(sin_r), by_col(-sin_c), by_col(sin_c)], axis=-1)
    return cos_t, sin_t


def _ctx_state_kernel(lg_ref, k_ref, v_ref, o_ref):
    h = pl.program_id(0)
    lc = k_ref.shape[0]
    m = lax.broadcasted_iota(jnp.int32, (lc, 1), 0).astype(F32)
    k = k_ref[...].astype(F32)
    v = v_ref[...]
    w_f = jnp.exp(lg_ref[0, h] * (lc - 1.0 - m))
    w_b = jnp.exp(lg_ref[1, h] * m)
    o_ref[0, 0] = _dot_t0((k * w_f).astype(BF16), v)
    o_ref[1, 0] = _dot_t0((k * w_b).astype(BF16), v)


def _ctx_states(pkv, lg):
    lc = pkv.shape[0]
    return pl.pallas_call(
        _ctx_state_kernel,
        grid=(RET_HEADS,),
        in_specs=[pl.BlockSpec(memory_space=pltpu.SMEM),
                  pl.BlockSpec((lc, RET_DK), lambda h: (0, h)),
                  pl.BlockSpec((lc, RET_DV), lambda h: (0, RET_Q // RET_DV + h))],
        out_specs=pl.BlockSpec((2, 1, RET_DK, RET_DV), lambda h: (0, h, 0, 0)),
        out_shape=jax.ShapeDtypeStruct((2, RET_HEADS, RET_DK, RET_DV), F32),
        compiler_params=_cparams("arbitrary"),
        name="ctx_states",
    )(lg, pkv, pkv)


def _ret_kernel(lg_ref, q_ref, k_ref, v_ref, g_ref, st0_ref, gng_ref, gnb_ref,
                o_ref, sb_all, sf_scr, sb_scr, *, nb, bc):
    c_len = RET_CHUNK
    h = pl.program_id(0)
    i = pl.program_id(1)
    lgf = lg_ref[0, h]
    lgb = lg_ref[1, h]
    pos = lax.broadcasted_iota(jnp.int32, (c_len, 1), 0).astype(F32)
    ones_row = jnp.ones((1, RET_DV), F32)

    @pl.when(i == 0)
    def _():
        sf_scr[...] = st0_ref[0, 0]
        sb_scr[...] = st0_ref[1, 0]

    @pl.when(i < nb)
    def _():
        rb = nb - 1 - i
        zeta_b = jnp.exp(lgb * pos)
        cd_b = jnp.exp((lgb * c_len) * ones_row)

        s_b = sb_scr[...]
        for c in reversed(range(bc)):
            rows = slice(c * c_len, (c + 1) * c_len)
            sb_all[rb * bc + c] = s_b.astype(BF16)
            kz = (k_ref[rows, :].astype(F32) * zeta_b).astype(BF16)
            s_b = s_b * cd_b + _dot_t0(kz, v_ref[rows, :])
        sb_scr[...] = s_b

    @pl.when(i >= nb)
    def _():
        rb = i - nb
        a = lax.broadcasted_iota(jnp.int32, (c_len, c_len), 0)
        b = lax.broadcasted_iota(jnp.int32, (c_len, c_len), 1)
        d = (a - b).astype(F32)
        decay = jnp.where(d > 0, jnp.exp(lgf * jnp.maximum(d, 0.0)),
                          jnp.where(d < 0, jnp.exp(lgb * jnp.maximum(-d, 0.0)), 2.0))
        xi_f = jnp.exp(lgf * (pos + 1.0))
        xi_b = jnp.exp(lgb * (c_len - pos))
        zeta_f = jnp.exp(lgf * (c_len - 1.0 - pos))
        cd_f = jnp.exp((lgf * c_len) * ones_row)

        s_f = sf_scr[...]
        for c in range(bc):
            rows = slice(c * c_len, (c + 1) * c_len)
            q = q_ref[rows, :]
            k = k_ref[rows, :]
            v = v_ref[rows, :]
            s = lax.dot_general(q, k, (((1,), (1,)), ((), ())), preferred_element_type=F32) * decay
            y = (_dot(s.astype(BF16), v)
                 + xi_f * _dot(q, s_f.astype(BF16))
                 + xi_b * _dot(q, sb_all[rb * bc + c]))
            kz = (k.astype(F32) * zeta_f).astype(BF16)
            s_f = s_f * cd_f + _dot_t0(kz, v)
            mu = jnp.mean(y, axis=-1, keepdims=True)
            yc = y - mu
            var = jnp.mean(yc * yc, axis=-1, keepdims=True)
            yn = yc * lax.rsqrt(var + EPS) * gng_ref[...] + gnb_ref[...]
            o_ref[rows, :] = (_silu(g_ref[rows, :].astype(F32)) * yn).astype(BF16)
        sf_scr[...] = s_f


def _retention(p, st0, lg, gn_g, gn_b, q_off, k_off, v_off, g_off, bc=16):
    t = p.shape[0]
    bc = min(bc, t // RET_CHUNK)
    rows = bc * RET_CHUNK
    nb = t // rows

    def kv_rb(i):
        return jnp.where(i < nb, nb - 1 - i, i - nb)

    def q_rb(i):
        return jnp.maximum(i - nb, 0)

    return pl.pallas_call(
        functools.partial(_ret_kernel, nb=nb, bc=bc),
        grid=(RET_HEADS, 2 * nb),
        in_specs=[pl.BlockSpec(memory_space=pltpu.SMEM),
                  pl.BlockSpec((rows, RET_DK), lambda h, i: (q_rb(i), q_off // RET_DK + h)),
                  pl.BlockSpec((rows, RET_DK), lambda h, i: (kv_rb(i), k_off // RET_DK + h)),
                  pl.BlockSpec((rows, RET_DV), lambda h, i: (kv_rb(i), v_off // RET_DV + h)),
                  pl.BlockSpec((rows, RET_DV), lambda h, i: (q_rb(i), g_off // RET_DV + h)),
                  pl.BlockSpec((2, 1, RET_DK, RET_DV), lambda h, i: (0, h, 0, 0)),
                  pl.BlockSpec((1, RET_DV), lambda h, i: (0, h)),
                  pl.BlockSpec((1, RET_DV), lambda h, i: (0, h))],
        out_specs=pl.BlockSpec((rows, RET_DV), lambda h, i: (q_rb(i), h)),
        out_shape=jax.ShapeDtypeStruct((t, RET_V), BF16),
        scratch_shapes=[pltpu.VMEM((t // RET_CHUNK, RET_DK, RET_DV), BF16),
                        pltpu.VMEM((RET_DK, RET_DV), F32),
                        pltpu.VMEM((RET_DK, RET_DV), F32)],
        compiler_params=_cparams("arbitrary", "arbitrary"),
        name="retention",
    )(lg, p, p, p, p, st0, gn_g.reshape(1, RET_V), gn_b.reshape(1, RET_V))


def _hy_pre_kernel(m0, m1, m2, p0, p1, p2, n0, n1, n2, w_ref, b_ref, zx_ref):
    i = pl.program_id(0)
    tm, ch = m0.shape
    row = lax.broadcasted_iota(jnp.int32, (tm, 1), 0)
    has_prev = (i > 0).astype(F32)
    has_next = (i < pl.num_programs(0) - 1).astype(F32)
    halo = p0.shape[0]
    r_idx = lax.broadcasted_iota(jnp.int32, (tm, tm), 0)
    c_idx = lax.broadcasted_iota(jnp.int32, (tm, tm), 1)
    down = jnp.where(r_idx == c_idx + 1, 1.0, 0.0).astype(BF16)
    up = jnp.where(r_idx + 1 == c_idx, 1.0, 0.0).astype(BF16)

    def conv(main, prev, nxt, part):
        cols = slice(part * ch, (part + 1) * ch)
        ub = main[...]
        before = jnp.where(row == 0, prev[halo - 1:halo, :].astype(F32) * has_prev, _dot(down, ub))
        after = jnp.where(row == tm - 1, nxt[0:1, :].astype(F32) * has_next, _dot(up, ub))
        return (before * w_ref[0:1, cols] + ub.astype(F32) * w_ref[1:2, cols] + after * w_ref[2:3, cols]
                + b_ref[0:1, cols])

    x0 = conv(m0, p0, n0, 0)
    z = conv(m1, p1, n1, 1) * conv(m2, p2, n2, 2)
    zx_ref[...] = _pack2(z, x0)


def _hy_pre(p, conv_w, conv_b, hy_off, ch, tm=256, halo=16):
    t = p.shape[0]
    tm = min(tm, t)
    nh = tm // halo
    last = t // halo - 1
    cb0 = hy_off // ch

    def main(part):
        return pl.BlockSpec((tm, ch), lambda i: (i, cb0 + part))

    def prev(part):
        return pl.BlockSpec((halo, ch), lambda i: (jnp.maximum(i * nh - 1, 0), cb0 + part))

    def nxt(part):
        return pl.BlockSpec((halo, ch), lambda i: (jnp.minimum((i + 1) * nh, last), cb0 + part))

    return pl.pallas_call(
        _hy_pre_kernel,
        grid=(t // tm,),
        in_specs=[main(0), main(1), main(2), prev(0), prev(1), prev(2), nxt(0), nxt(1), nxt(2),
                  pl.BlockSpec((3, 3 * ch), lambda i: (0, 0)),
                  pl.BlockSpec((1, 3 * ch), lambda i: (0, 0))],
        out_specs=pl.BlockSpec((tm, ch), lambda i: (i, 0)),
        out_shape=jax.ShapeDtypeStruct((t, ch), U32),
        compiler_params=_cparams("parallel"),
        name="hy_conv3",
    )(p, p, p, p, p, p, p, p, p, conv_w, conv_b.reshape(1, 3 * ch))


def _filt_ffn_kernel(f_ref, w1, b1, w2, b2, w3, b3, fr, o_ref):
    hp = lax.Precision.HIGHEST
    freq = fr[:, 0:1]
    h = jnp.sin(freq * (jnp.dot(w1[...], f_ref[...], precision=hp, preferred_element_type=F32) + b1[:, 0:1]))
    h = jnp.sin(freq * (jnp.dot(w2[...], h, precision=hp, preferred_element_type=F32) + b2[:, 0:1]))
    o_ref[...] = jnp.sin(freq * (jnp.dot(w3[...], h, precision=hp, preferred_element_type=F32) + b3[:, 0:1]))


def _filter_ffn(feat_t, w1, b1, w2, b2, w3, b3, freq, tl=2048):
    e, length = feat_t.shape
    ff = w1.shape[1]
    tl = min(tl, length)

    def col(v):
        return jnp.broadcast_to(v.reshape(ff, 1), (ff, 128))

    full = lambda shape: pl.BlockSpec(shape, lambda i: (0, 0))
    return pl.pallas_call(
        _filt_ffn_kernel,
        grid=(length // tl,),
        in_specs=[pl.BlockSpec((e, tl), lambda i: (0, i)),
                  full((ff, e)), full((ff, 128)), full((ff, ff)), full((ff, 128)),
                  full((ff, ff)), full((ff, 128)), full((ff, 128))],
        out_specs=pl.BlockSpec((ff, tl), lambda i: (0, i)),
        out_shape=jax.ShapeDtypeStruct((ff, length), F32),
        compiler_params=_cparams("parallel"),
        name="filter_ffn",
    )(feat_t, w1.T, col(b1), w2.T, col(b2), w3.T, col(b3), col(freq))


def _filt_taps_kernel(hf_ref, hb_ref, tf_ref, tb_ref, wf_ref, wb_ref, dec_ref, hp_ref, ssq_ref):
    i = pl.program_id(0)
    hf = hf_ref[...].astype(BF16)
    hb = hb_ref[...].astype(BF16)
    t_f = tf_ref[:, 0:1]
    t_b = tb_ref[:, 0:1]
    row = lax.broadcasted_iota(jnp.int32, t_b.shape, 0)
    sign_b = jnp.where((row == 0) & (i == 0), 0.0, -1.0)

    @pl.when(i == 0)
    def _():
        ssq_ref[...] = jnp.zeros_like(ssq_ref)

    width = TAPS_COLS
    for c0 in range(0, dec_ref.shape[1], width):
        cols = slice(c0, c0 + width)
        dec = jnp.abs(dec_ref[:, cols])
        fwd = _dot_t0(hf, wf_ref[:, cols]) * jnp.exp(-t_f * dec)
        bwd = _dot_t0(hb, wb_ref[:, cols]) * (jnp.exp(-t_b * dec) * sign_b)
        ssq_ref[:, cols] += jnp.sum(fwd * fwd + bwd * bwd, axis=0, keepdims=True)
        hp_ref[:, cols] = _pack2(fwd, bwd)


def _filter_taps(h_ext, t_ext, w4, decay, tr=512):
    ff, n = h_ext.shape
    length = n // 2
    ch = decay.shape[0]
    tr = min(tr, length)
    nt = length // tr
    return pl.pallas_call(
        _filt_taps_kernel,
        grid=(nt,),
        in_specs=[pl.BlockSpec((ff, tr), lambda i: (0, i)),
                  pl.BlockSpec((ff, tr), lambda i: (0, i + nt)),
                  pl.BlockSpec((tr, 1), lambda i: (i, 0)),
                  pl.BlockSpec((tr, 1), lambda i: (i + nt, 0)),
                  pl.BlockSpec((ff, ch), lambda i: (0, 0)),
                  pl.BlockSpec((ff, ch), lambda i: (0, 1)),
                  pl.BlockSpec((1, ch), lambda i: (0, 0))],
        out_specs=[pl.BlockSpec((tr, ch), lambda i: (i, 0)),
                   pl.BlockSpec((1, ch), lambda i: (0, 0))],
        out_shape=[jax.ShapeDtypeStruct((length, ch), U32), jax.ShapeDtypeStruct((1, ch), F32)],
        compiler_params=_cparams("arbitrary"),
        name="filter_taps",
    )(h_ext, h_ext, t_ext, t_ext, w4, w4, decay.reshape(1, ch))


def _wide_spec(rows):
    return pl.BlockSpec((rows, None, SUBLANES, DFT_WIDTH), lambda g, j: (0, g, 0, j))


def _flat_scratch(rows, arrays=1):
    return [pltpu.VMEM((rows * SUBLANES, LANES), U32)] * (arrays * DFT_WIDTH // LANES)


def _flatten(block, flats):
    for s, f in enumerate(flats):
        f[...] = block[:, :, s * LANES:(s + 1) * LANES].reshape(f.shape)


def _gather_rows(flats, a):
    return jnp.concatenate([f[pl.ds(a, f.shape[0] // SUBLANES, stride=SUBLANES), :] for f in flats], axis=1)


def _lane_groups(flats):
    per = DFT_SLABS
    return [(slice(i * per * LANES, (i + 1) * per * LANES), flats[i * per:(i + 1) * per])
            for i in range(len(flats) // per)]


def _dft_a_kernel(x_ref, e_ref, o_ref, *flats, both_halves):
    _flatten(x_ref, flats)
    for lanes, fl in _lane_groups(flats):
        for a in range(SUBLANES):
            hi, lo = _unpack2(_gather_rows(fl, a))
            x = jnp.concatenate([hi, lo], axis=0) if both_halves else hi
            r = _dot(e_ref[a], x.astype(BF16))
            n = r.shape[0] // 2
            o_ref[a, :, lanes] = _pack2(r[:n], r[n:])


def _dft_stage_a(x, e_tab, both_halves):
    n1, m2, kdim = e_tab.shape
    length, c = x.shape
    k2 = length // n1
    assert kdim == (2 * k2 if both_halves else k2)
    return pl.pallas_call(
        functools.partial(_dft_a_kernel, both_halves=both_halves),
        grid=(n1 // SUBLANES, c // DFT_WIDTH),
        in_specs=[_wide_spec(k2), pl.BlockSpec((SUBLANES, m2, kdim), lambda g, j: (g, 0, 0))],
        out_specs=pl.BlockSpec((SUBLANES, m2 // 2, DFT_WIDTH), lambda g, j: (g, 0, j)),
        out_shape=jax.ShapeDtypeStruct((n1, m2 // 2, c), U32),
        scratch_shapes=_flat_scratch(k2),
        compiler_params=_cparams("parallel", "arbitrary"),
        name="dft_stage_a",
    )(x.reshape(k2, n1 // SUBLANES, SUBLANES, c), e_tab)


def _dft_mid_kernel(z_ref, h_ref, f_ref, finv_ref, o_ref, *flats):
    half = len(flats) // 2
    _flatten(z_ref, flats[:half])
    _flatten(h_ref, flats[half:])
    n = z_ref.shape[0]
    for (lanes, fz), (_, fh) in zip(_lane_groups(flats[:half]), _lane_groups(flats[half:])):
        for a in range(SUBLANES):
            zr, zi = _unpack2(_gather_rows(fz, a))
            hr, hi = _unpack2(_gather_rows(fh, a))
            xs = _dot(f_ref[...], jnp.concatenate([zr, zi], axis=0).astype(BF16))
            ks = _dot(f_ref[...], jnp.concatenate([hr, hi], axis=0).astype(BF16))
            xr, xi, kr, ki = xs[:n], xs[n:], ks[:n], ks[n:]
            y = jnp.concatenate([xr * kr - xi * ki, xr * ki + xi * kr], axis=0).astype(BF16)
            cs = _dot(finv_ref[...], y)
            o_ref[a, :, lanes] = _pack2(cs[:n], cs[n:])


def _dft_mid(bz, bh, f_fwd, f_inv):
    n1, n2h, c = bz.shape
    mat = pl.BlockSpec((2 * n1, 2 * n1), lambda g, j: (0, 0))
    view = lambda b: b.reshape(n1, n2h // SUBLANES, SUBLANES, c)
    return pl.pallas_call(
        _dft_mid_kernel,
        grid=(n2h // SUBLANES, c // DFT_WIDTH),
        in_specs=[_wide_spec(n1), _wide_spec(n1), mat, mat],
        out_specs=pl.BlockSpec((SUBLANES, n1, DFT_WIDTH), lambda g, j: (g, 0, j)),
        out_shape=jax.ShapeDtypeStruct((n2h, n1, c), U32),
        scratch_shapes=_flat_scratch(n1, arrays=2),
        compiler_params=_cparams("parallel", "arbitrary"),
        name="dft_mid",
    )(view(bz), view(bh), f_fwd, f_inv)


def _dft_ainv_kernel(c_ref, zx_ref, e_ref, skip_ref, sc_ref, o_ref, *flats):
    half = len(flats) // 2
    _flatten(c_ref, flats[:half])
    _flatten(zx_ref, flats[half:])
    groups = list(zip(_lane_groups(flats[:half]), _lane_groups(flats[half:])))
    assert len(groups) == 2, "the two lane groups of a step share one output word"
    for a in range(SUBLANES):
        vals = []
        for (lanes, fc), (_, fzx) in groups:
            c_r, c_i = _unpack2(_gather_rows(fc, a))
            y = _dot_t0(e_ref[a], jnp.concatenate([c_r, c_i], axis=0).astype(BF16)) * sc_ref[:, lanes]
            z, x0 = _unpack2(_gather_rows(fzx, a))
            vals.append(x0 * (y + z * skip_ref[:, lanes]))
        o_ref[:, a, :] = _pack2(vals[0], vals[1])


def _dft_stage_a_inv(cc, e_tab, zx, skip, scale):
    n2h, n1, c = cc.shape
    length = zx.shape[0]
    k2 = length // n1
    assert e_tab.shape == (n1, 2 * n2h, k2)
    row = pl.BlockSpec((1, DFT_WIDTH), lambda g, j: (0, j))
    out = pl.pallas_call(
        _dft_ainv_kernel,
        grid=(n1 // SUBLANES, c // DFT_WIDTH),
        in_specs=[_wide_spec(n2h), _wide_spec(k2),
                  pl.BlockSpec((SUBLANES, 2 * n2h, k2), lambda g, j: (g, 0, 0)), row, row],
        out_specs=pl.BlockSpec((k2, None, SUBLANES, DFT_WIDTH // 2), lambda g, j: (0, g, 0, j)),
        out_shape=jax.ShapeDtypeStruct((k2, n1 // SUBLANES, SUBLANES, c // 2), U32),
        scratch_shapes=_flat_scratch(n2h) + _flat_scratch(k2),
        compiler_params=_cparams("parallel", "arbitrary"),
        name="dft_stage_a_inv",
    )(cc.reshape(n2h, n1 // SUBLANES, SUBLANES, c), zx.reshape(k2, n1 // SUBLANES, SUBLANES, c),
      e_tab, skip.reshape(1, c), scale)
    return out.reshape(length, c // 2)


def _dft_tables(n1, n2h):
    n2 = 2 * n2h
    n = n1 * n2
    a = jnp.arange(n1, dtype=jnp.int32)
    b = jnp.arange(n2, dtype=jnp.int32)
    odd = 2 * jnp.arange(n2h, dtype=jnp.int32) + 1
    alpha = ((a[:, None] * odd[None, :]) % (2 * n)).astype(F32) * (math.pi / n)
    beta = ((odd[:, None] * b[None, :]) % (2 * n2)).astype(F32) * (math.pi / n2)
    ar, ai = jnp.cos(alpha)[:, :, None], -jnp.sin(alpha)[:, :, None]
    br, bi = jnp.cos(beta)[None], -jnp.sin(beta)[None]
    er = ar * br - ai * bi
    ei = ar * bi + ai * br
    e_full = jnp.concatenate([er, ei], axis=1).astype(BF16)
    e_half = e_full[:, :, :n2h]
    phi = ((a[:, None] * a[None, :]) % n1).astype(F32) * (2.0 * math.pi / n1)
    fr, fi = jnp.cos(phi), -jnp.sin(phi)
    f_fwd = jnp.concatenate([jnp.concatenate([fr, -fi], axis=1),
                             jnp.concatenate([fi, fr], axis=1)], axis=0).astype(BF16)
    f_inv = jnp.concatenate([jnp.concatenate([fr, fi], axis=1),
                             jnp.concatenate([-fi, fr], axis=1)], axis=0).astype(BF16)
    return e_half, e_full, f_fwd, f_inv


def _hyena_features(length):
    r = jnp.arange(2 * length, dtype=jnp.int32)
    pos = jnp.where(r < length, r, 2 * length - r)
    t = pos.astype(F32) / (length - 1.0)
    w = 2.0 * math.pi * pos.astype(F32)[None, :] / length
    f = jnp.linspace(1e-4, HY_BANDS - 1.0, HY_BANDS, dtype=F32)[:, None]
    return t[:, None], jnp.concatenate([t[None, :], jnp.cos(f * w), -jnp.sin(f * w)], axis=0)


def _merge_kernel(x_ref, yr_ref, hy_ref, gr_ref, gh_ref, gate_ref, wr_ref, wh_ref, wo_ref, o_ref):
    ret_out = _dot(yr_ref[...], wr_ref[...])
    hi, lo = _unpack2(hy_ref[...])
    half = DFT_WIDTH // 2
    hy = jnp.concatenate([part[:, b:b + half] for b in range(0, hi.shape[1], half) for part in (hi, lo)], axis=1)
    hy_out = _dot(hy.astype(BF16), wh_ref[...])
    m = (jax.nn.sigmoid(gr_ref[...].astype(F32)) * ret_out
         + jax.nn.sigmoid(gh_ref[...].astype(F32)) * hy_out)
    o_ref[...] = x_ref[...] + gate_ref[...] * _dot(m.astype(BF16), wo_ref[...])


def _merge(x, y_ret, hy_in, p, gate_off, gate, w_ret_o, w_hy_o, w_out, tm=256):
    t, d = x.shape
    tm = min(tm, t)
    gb = gate_off // d
    tile = lambda: pl.BlockSpec((tm, d), lambda i: (i, 0))
    weight = lambda w: pl.BlockSpec((None,) + w.shape[1:], lambda i: (0, 0, 0), pipeline_mode=pl.Buffered(1))
    return pl.pallas_call(
        _merge_kernel,
        grid=(t // tm,),
        in_specs=[tile(), tile(), pl.BlockSpec((tm, d // 2), lambda i: (i, 0)),
                  pl.BlockSpec((tm, d), lambda i: (i, gb)),
                  pl.BlockSpec((tm, d), lambda i: (i, gb + 1)),
                  pl.BlockSpec((1, d), lambda i: (0, 0)),
                  weight(w_ret_o), weight(w_hy_o), weight(w_out)],
        out_specs=tile(),
        out_shape=jax.ShapeDtypeStruct((t, d), F32),
        compiler_params=_cparams("parallel"),
        name="merge_out",
    )(x, y_ret, hy_in, p, p, gate.reshape(1, d), w_ret_o, w_hy_o, w_out)


def _mod_rows(mods, first):
    return jnp.pad(mods[first:first + 3], ((0, 5), (0, 0)))


def kernel(x, c, ctx, c_ctx, w_ada, b_ada, norm_g, ffn_up, ffn_down, w_in, ret_log_gamma, ret_gn_g,
           ret_gn_b, w_ret_o, hy_conv_w, hy_conv_b, hy_ff_w1, hy_ff_b1, hy_ff_w2, hy_ff_b2, hy_ff_w3,
           hy_ff_b3, hy_ff_w4, hy_sin_freq, hy_decay, hy_bias, w_hy_o, w_out, final_norm_g):
    batch, t, d = x.shape
    assert batch == 1 and w_ada.shape[0] == 1, "single sample, single layer"
    hy_w = hy_decay.shape[1]
    q_off, k_off = 0, RET_Q
    v_off = k_off + RET_Q
    g_off = v_off + RET_V
    hy_off = g_off + RET_V
    gate_off = hy_off + 3 * hy_w
    assert w_in.shape[2] == gate_off + 2 * d and hy_w == d

    xs, cs = x[0], ctx[0]
    up, down, w_in_b = ffn_up.astype(BF16), ffn_down.astype(BF16), w_in.astype(BF16)
    lg = ret_log_gamma[0].astype(F32)

    cv = jnp.concatenate([c, c_ctx[None, :], jnp.zeros((6, d), F32)], axis=0)
    mods = _adaln(cv, w_ada[0], b_ada[0]).reshape(8, N_MOD, d)
    mx, mc = mods[0], mods[1]

    xs = _ffn_half(xs, _mod_rows(mx, 0), norm_g[0, 0], up, down, 0)
    cs = _ffn_half(cs, _mod_rows(mc, 0), norm_g[0, 0], up, down, 0)

    pkv = _in_proj(cs, _mod_rows(mc, 3), norm_g[0, 1], w_in_b, k_off, g_off - k_off)
    st0 = _ctx_states(pkv, lg)

    cos_t, sin_t = _rope_tables(t)
    p = _in_proj(xs, _mod_rows(mx, 3), norm_g[0, 1], w_in_b, 0, w_in_b.shape[2],
                 rope=(cos_t, sin_t, v_off))

    y_ret = _retention(p, st0, lg, ret_gn_g[0], ret_gn_b[0], q_off, k_off, v_off, g_off)

    zx = _hy_pre(p, hy_conv_w[0], hy_conv_b[0], hy_off, hy_w)
    t_ext, feat_t = _hyena_features(t)
    e_pad = (-feat_t.shape[0]) % 8
    feat_t = jnp.pad(feat_t, ((0, e_pad), (0, 0)))
    w1 = jnp.pad(hy_ff_w1[0], ((0, e_pad), (0, 0)))
    h_ext = _filter_ffn(feat_t, w1, hy_ff_b1[0], hy_ff_w2[0], hy_ff_b2[0], hy_ff_w3[0], hy_ff_b3[0],
                        hy_sin_freq[0])
    h_taps, ssq = _filter_taps(h_ext, t_ext, hy_ff_w4[0].astype(BF16), hy_decay[0])

    n1 = DFT_N1
    e_half, e_full, f_fwd, f_inv = _dft_tables(n1, t // n1)
    bz = _dft_stage_a(zx, e_half, both_halves=False)
    bh = _dft_stage_a(h_taps, e_full, both_halves=True)
    cc = _dft_mid(bz, bh, f_fwd, f_inv)
    scale = lax.rsqrt(ssq + EPS) * (2.0 / (2 * t))
    hy_in = _dft_stage_a_inv(cc, e_half, zx, hy_bias[0], scale)

    xs = _merge(xs, y_ret, hy_in, p, gate_off, mx[5], w_ret_o.astype(BF16), w_hy_o.astype(BF16),
                w_out.astype(BF16))

    out = _ffn_half(xs, _mod_rows(mx, 6), norm_g[0, 2], up, down, 1, final_g=final_norm_g)
    return out[None]
```

```python
import functools
import math

import jax
import jax.numpy as jnp
from jax import lax
from jax.experimental import pallas as pl
from jax.experimental.pallas import tpu as pltpu

F32 = jnp.float32
BF16 = jnp.bfloat16
U32 = jnp.uint32

N_MOD = 9
GRID_W = 64
RET_HEADS = 8
RET_DK = 128
RET_DV = 256
RET_CHUNK = 128
RET_Q = RET_HEADS * RET_DK
RET_V = RET_HEADS * RET_DV
RET_PAIR = 2
HY_BANDS = 16
ROPE_BASE = 10000.0
EPS = 1e-6

TAPS_COLS = 256
DFT_N1 = 128
SUBLANES = 8
LANES = 128
DFT_SLABS = 4
DFT_WIDTH = 1024
VMEM_BYTES = 64 * 1024 * 1024
VMEM_LIMIT = VMEM_BYTES - 8 * 1024 * 1024


def _cparams(*sem, vmem=VMEM_LIMIT):
    return pltpu.CompilerParams(dimension_semantics=sem, vmem_limit_bytes=vmem)


def _dot(a, b):
    return jnp.dot(a, b, preferred_element_type=F32)


def _dot_t0(a, b):
    return lax.dot_general(a, b, (((0,), (0,)), ((), ())), preferred_element_type=F32)


def _silu(v):
    return v * jax.nn.sigmoid(v)


def _norm_mod(x, g, shift, scale):
    return x * lax.rsqrt(jnp.mean(x * x, axis=-1, keepdims=True) + EPS) * (g * (1.0 + scale)) + shift


def _pack2(hi, lo):
    hb = lax.bitcast_convert_type(hi.astype(BF16).astype(F32), U32)
    lb = lax.bitcast_convert_type(lo.astype(BF16).astype(F32), U32)
    return hb | (lb >> 16)


def _unpack2(w):
    hi = lax.bitcast_convert_type(w & jnp.uint32(0xFFFF0000), F32)
    lo = lax.bitcast_convert_type(w << 16, F32)
    return hi, lo


def _adaln_kernel(c_ref, w_ref, b_ref, o_ref):
    s = _silu(c_ref[...]).astype(BF16)
    o_ref[...] = _dot(s, w_ref[...].astype(BF16)) + b_ref[...]


def _adaln(cv, w, b, tn=1024):
    rows, d = cv.shape
    n = w.shape[1]
    return pl.pallas_call(
        _adaln_kernel,
        grid=(n // tn,),
        in_specs=[pl.BlockSpec((rows, d), lambda j: (0, 0)),
                  pl.BlockSpec((d, tn), lambda j: (0, j)),
                  pl.BlockSpec((1, tn), lambda j: (0, j))],
        out_specs=pl.BlockSpec((rows, tn), lambda j: (0, j)),
        out_shape=jax.ShapeDtypeStruct((rows, n), F32),
        compiler_params=_cparams("arbitrary"),
        name="adaln",
    )(cv, w, b.reshape(1, n))


def _ffn_kernel(x_ref, mod_ref, g_ref, wa_ref, wg_ref, wd_ref, *rest, final):
    if final:
        fg_ref, o_ref, h_scr, acc_scr = rest
    else:
        o_ref, h_scr, acc_scr = rest
    j = pl.program_id(1)

    @pl.when(j == 0)
    def _():
        h = _norm_mod(x_ref[...], g_ref[...], mod_ref[0:1, :], mod_ref[1:2, :])
        h_scr[...] = h.astype(BF16)
        acc_scr[...] = jnp.zeros_like(acc_scr)

    h = h_scr[...]
    a = _dot(h, wa_ref[...])
    g = _dot(h, wg_ref[...])
    acc_scr[...] += _dot((_silu(a) * g).astype(BF16), wd_ref[...])

    @pl.when(j == pl.num_programs(1) - 1)
    def _():
        out = x_ref[...] + (0.5 * mod_ref[2:3, :]) * acc_scr[...]
        if final:
            out = out * lax.rsqrt(jnp.mean(out * out, axis=-1, keepdims=True) + EPS) * fg_ref[...]
        o_ref[...] = out


def _ffn_half(x, mod, g, w_up, w_down, layer, final_g=None, tm=512, tf=512):
    t, d = x.shape
    dff = w_down.shape[2]
    tm = min(tm, t)
    nf = dff // tf
    final = final_g is not None
    in_specs = [pl.BlockSpec((tm, d), lambda i, j: (i, 0)),
                pl.BlockSpec((8, d), lambda i, j: (0, 0)),
                pl.BlockSpec((1, d), lambda i, j: (0, 0)),
                pl.BlockSpec((None, None, d, tf), lambda i, j: (0, layer, 0, j)),
                pl.BlockSpec((None, None, d, tf), lambda i, j: (0, layer, 0, nf + j)),
                pl.BlockSpec((None, None, tf, d), lambda i, j: (0, layer, j, 0))]
    args = [x, mod, g.reshape(1, d), w_up, w_up, w_down]
    if final:
        in_specs.append(pl.BlockSpec((1, d), lambda i, j: (0, 0)))
        args.append(final_g.reshape(1, d))
    return pl.pallas_call(
        functools.partial(_ffn_kernel, final=final),
        grid=(t // tm, nf),
        in_specs=in_specs,
        out_specs=pl.BlockSpec((tm, d), lambda i, j: (i, 0)),
        out_shape=jax.ShapeDtypeStruct((t, d), F32),
        scratch_shapes=[pltpu.VMEM((tm, d), BF16), pltpu.VMEM((tm, d), F32)],
        compiler_params=_cparams("parallel", "arbitrary"),
        name="ffn_final" if final else "ffn_half",
    )(*args)


def _rope(x, cos, sin):
    lane = lax.broadcasted_iota(jnp.int32, x.shape, 1)
    partner = jnp.where((lane % 64) < 32, pltpu.roll(x, 96, 1), pltpu.roll(x, 32, 1))
    return x * cos + partner * sin


def _inproj_kernel(x_ref, mod_ref, g_ref, w_ref, *rest, rope_tiles):
    if rope_tiles:
        cos_ref, sin_ref, o_ref, h_scr = rest
    else:
        o_ref, h_scr = rest
    j = pl.program_id(1)

    @pl.when(j == 0)
    def _():
        h = _norm_mod(x_ref[...], g_ref[...], mod_ref[0:1, :], mod_ref[1:2, :])
        h_scr[...] = h.astype(BF16)

    if rope_tiles:
        @pl.when(j < rope_tiles)
        def _():
            acc = _dot(h_scr[...], w_ref[...])
            scale = jnp.where(j == 0, RET_DK ** -0.5, 1.0)
            cos = cos_ref[...] * scale
            sin = sin_ref[...] * scale
            for hb in range(acc.shape[1] // RET_DK):
                cols = slice(hb * RET_DK, (hb + 1) * RET_DK)
                o_ref[:, cols] = _rope(acc[:, cols], cos, sin).astype(BF16)

        @pl.when(j >= rope_tiles)
        def _():
            o_ref[...] = _dot(h_scr[...], w_ref[...]).astype(BF16)
    else:
        o_ref[...] = _dot(h_scr[...], w_ref[...]).astype(BF16)


def _in_proj(x, mod, g, w, col0, ncols, rope=None, tm=1024, tn=1024):
    t, d = x.shape
    tm = min(tm, t)
    off = col0 // tn
    in_specs = [pl.BlockSpec((tm, d), lambda i, j: (i, 0)),
                pl.BlockSpec((8, d), lambda i, j: (0, 0)),
                pl.BlockSpec((1, d), lambda i, j: (0, 0)),
                pl.BlockSpec((None, d, tn), lambda i, j: (0, 0, j + off))]
    args = [x, mod, g.reshape(1, d), w]
    rope_tiles = 0
    if rope is not None:
        cos_t, sin_t, qk_cols = rope
        assert tn == RET_Q and qk_cols == 2 * RET_Q and col0 == 0
        rope_tiles = qk_cols // tn
        in_specs += [pl.BlockSpec((tm, RET_DK), lambda i, j: (i, 0))] * 2
        args += [cos_t, sin_t]
    return pl.pallas_call(
        functools.partial(_inproj_kernel, rope_tiles=rope_tiles),
        grid=(t // tm, ncols // tn),
        in_specs=in_specs,
        out_specs=pl.BlockSpec((tm, tn), lambda i, j: (i, j)),
        out_shape=jax.ShapeDtypeStruct((t, ncols), BF16),
        scratch_shapes=[pltpu.VMEM((tm, d), BF16)],
        compiler_params=_cparams("parallel", "arbitrary"),
        name="in_proj",
    )(*args)


def _rope_tables(t):
    half = RET_DK // 2
    n_rows = t // GRID_W
    inv = ROPE_BASE ** (-jnp.arange(0, half, 2, dtype=F32) / half)
    ang_r = jnp.arange(n_rows, dtype=jnp.int32).astype(F32)[:, None] * inv[None, :]
    ang_c = jnp.arange(GRID_W, dtype=jnp.int32).astype(F32)[:, None] * inv[None, :]
    by_row = lambda v: jnp.repeat(v, GRID_W, axis=0)
    by_col = lambda v: jnp.tile(v, (n_rows, 1))
    cos_r, sin_r, cos_c, sin_c = jnp.cos(ang_r), jnp.sin(ang_r), jnp.cos(ang_c), jnp.sin(ang_c)
    cos_t = jnp.concatenate([by_row(cos_r)] * 2 + [by_col(cos_c)] * 2, axis=-1)
    sin_t = jnp.concatenate([by_row(-sin_r), by_row(sin_r), by_col(-sin_c), by_col(sin_c)], axis=-1)
    return cos_t, sin_t


def _ctx_state_kernel(lg_ref, k_ref, v_ref, o_ref):
    h = pl.program_id(0)
    lc = k_ref.shape[0]
    m = lax.broadcasted_iota(jnp.int32, (lc, 1), 0).astype(F32)
    k = k_ref[...].astype(F32)
    v = v_ref[...]
    w_f = jnp.exp(lg_ref[0, h] * (lc - 1.0 - m))
    w_b = jnp.exp(lg_ref[1, h] * m)
    o_ref[0, 0] = _dot_t0((k * w_f).astype(BF16), v)
    o_ref[1, 0] = _dot_t0((k * w_b).astype(BF16), v)


def _ctx_states(pkv, lg):
    lc = pkv.shape[0]
    return pl.pallas_call(
        _ctx_state_kernel,
        grid=(RET_HEADS,),
        in_specs=[pl.BlockSpec(memory_space=pltpu.SMEM),
                  pl.BlockSpec((lc, RET_DK), lambda h: (0, h)),
                  pl.BlockSpec((lc, RET_DV), lambda h: (0, RET_Q // RET_DV + h))],
        out_specs=pl.BlockSpec((2, 1, RET_DK, RET_DV), lambda h: (0, h, 0, 0)),
        out_shape=jax.ShapeDtypeStruct((2, RET_HEADS, RET_DK, RET_DV), F32),
        compiler_params=_cparams("arbitrary"),
        name="ctx_states",
    )(lg, pkv, pkv)


def _ret_kernel(lg_ref, q_ref, k_ref, v_ref, g_ref, st0_ref, gng_ref, gnb_ref,
                o_ref, sb_all, sf_scr, sb_scr, *, nb, bc):
    c_len = RET_CHUNK
    hp = pl.program_id(0)
    i = pl.program_id(1)
    pair = range(RET_PAIR)
    lgf = [lg_ref[0, hp * RET_PAIR + e] for e in pair]
    lgb = [lg_ref[1, hp * RET_PAIR + e] for e in pair]
    kcols = [slice(e * RET_DK, (e + 1) * RET_DK) for e in pair]
    vcols = [slice(e * RET_DV, (e + 1) * RET_DV) for e in pair]
    pos = lax.broadcasted_iota(jnp.int32, (c_len, 1), 0).astype(F32)
    ones_row = jnp.ones((1, RET_DV), F32)

    @pl.when(i == 0)
    def _():
        for e in pair:
            sf_scr[e] = st0_ref[0, e]
            sb_scr[e] = st0_ref[1, e]

    @pl.when(i < nb)
    def _():
        rb = nb - 1 - i
        zeta_b = [jnp.exp(lgb[e] * pos) for e in pair]
        cd_b = [jnp.exp((lgb[e] * c_len) * ones_row) for e in pair]
        s_b = [sb_scr[e] for e in pair]
        for c in reversed(range(bc)):
            rows = slice(c * c_len, (c + 1) * c_len)
            for e in pair:
                sb_all[e, rb * bc + c] = s_b[e].astype(BF16)
                kz = (k_ref[rows, kcols[e]].astype(F32) * zeta_b[e]).astype(BF16)
                s_b[e] = s_b[e] * cd_b[e] + _dot_t0(kz, v_ref[rows, vcols[e]])
        for e in pair:
            sb_scr[e] = s_b[e]

    @pl.when(i >= nb)
    def _():
        rb = i - nb
        a = lax.broadcasted_iota(jnp.int32, (c_len, c_len), 0)
        b = lax.broadcasted_iota(jnp.int32, (c_len, c_len), 1)
        d = (a - b).astype(F32)
        decay = [jnp.where(d > 0, jnp.exp(lgf[e] * jnp.maximum(d, 0.0)),
                           jnp.where(d < 0, jnp.exp(lgb[e] * jnp.maximum(-d, 0.0)), 2.0)) for e in pair]
        xi_f = [jnp.exp(lgf[e] * (pos + 1.0)) for e in pair]
        xi_b = [jnp.exp(lgb[e] * (c_len - pos)) for e in pair]
        zeta_f = [jnp.exp(lgf[e] * (c_len - 1.0 - pos)) for e in pair]
        cd_f = [jnp.exp((lgf[e] * c_len) * ones_row) for e in pair]
        s_f = [sf_scr[e] for e in pair]
        for c in range(bc):
            rows = slice(c * c_len, (c + 1) * c_len)
            for e in pair:
                q = q_ref[rows, kcols[e]]
                k = k_ref[rows, kcols[e]]
                v = v_ref[rows, vcols[e]]
                s = lax.dot_general(q, k, (((1,), (1,)), ((), ())), preferred_element_type=F32) * decay[e]
                y = (_dot(s.astype(BF16), v)
                     + xi_f[e] * _dot(q, s_f[e].astype(BF16))
                     + xi_b[e] * _dot(q, sb_all[e, rb * bc + c]))
                kz = (k.astype(F32) * zeta_f[e]).astype(BF16)
                s_f[e] = s_f[e] * cd_f[e] + _dot_t0(kz, v)
                mu = jnp.mean(y, axis=-1, keepdims=True)
                yc = y - mu
                var = jnp.mean(yc * yc, axis=-1, keepdims=True)
                yn = yc * lax.rsqrt(var + EPS) * gng_ref[:, vcols[e]] + gnb_ref[:, vcols[e]]
                o_ref[rows, vcols[e]] = (_silu(g_ref[rows, vcols[e]].astype(F32)) * yn).astype(BF16)
        for e in pair:
            sf_scr[e] = s_f[e]


def _retention(p, st0, lg, gn_g, gn_b, q_off, k_off, v_off, g_off, bc=16):
    t = p.shape[0]
    bc = min(bc, t // RET_CHUNK)
    rows = bc * RET_CHUNK
    nb = t // rows
    kw, vw = RET_PAIR * RET_DK, RET_PAIR * RET_DV

    def kv_rb(i):
        return jnp.where(i < nb, nb - 1 - i, i - nb)

    def q_rb(i):
        return jnp.maximum(i - nb, 0)

    return pl.pallas_call(
        functools.partial(_ret_kernel, nb=nb, bc=bc),
        grid=(RET_HEADS // RET_PAIR, 2 * nb),
        in_specs=[pl.BlockSpec(memory_space=pltpu.SMEM),
                  pl.BlockSpec((rows, kw), lambda h, i: (q_rb(i), q_off // kw + h)),
                  pl.BlockSpec((rows, kw), lambda h, i: (kv_rb(i), k_off // kw + h)),
                  pl.BlockSpec((rows, vw), lambda h, i: (kv_rb(i), v_off // vw + h)),
                  pl.BlockSpec((rows, vw), lambda h, i: (q_rb(i), g_off // vw + h)),
                  pl.BlockSpec((2, RET_PAIR, RET_DK, RET_DV), lambda h, i: (0, h, 0, 0)),
                  pl.BlockSpec((1, vw), lambda h, i: (0, h)),
                  pl.BlockSpec((1, vw), lambda h, i: (0, h))],
        out_specs=pl.BlockSpec((rows, vw), lambda h, i: (q_rb(i), h)),
        out_shape=jax.ShapeDtypeStruct((t, RET_V), BF16),
        scratch_shapes=[pltpu.VMEM((RET_PAIR, t // RET_CHUNK, RET_DK, RET_DV), BF16),
                        pltpu.VMEM((RET_PAIR, RET_DK, RET_DV), F32),
                        pltpu.VMEM((RET_PAIR, RET_DK, RET_DV), F32)],
        compiler_params=_cparams("arbitrary", "arbitrary"),
        name="retention",
    )(lg, p, p, p, p, st0, gn_g.reshape(1, RET_V), gn_b.reshape(1, RET_V))


def _hy_pre_kernel(m0, m1, m2, p0, p1, p2, n0, n1, n2, w_ref, b_ref, zx_ref):
    i = pl.program_id(0)
    tm, ch = m0.shape
    row = lax.broadcasted_iota(jnp.int32, (tm, 1), 0)
    has_prev = (i > 0).astype(F32)
    has_next = (i < pl.num_programs(0) - 1).astype(F32)
    halo = p0.shape[0]
    r_idx = lax.broadcasted_iota(jnp.int32, (tm, tm), 0)
    c_idx = lax.broadcasted_iota(jnp.int32, (tm, tm), 1)
    down = jnp.where(r_idx == c_idx + 1, 1.0, 0.0).astype(BF16)
    up = jnp.where(r_idx + 1 == c_idx, 1.0, 0.0).astype(BF16)

    def conv(main, prev, nxt, part):
        cols = slice(part * ch, (part + 1) * ch)
        ub = main[...]
        before = jnp.where(row == 0, prev[halo - 1:halo, :].astype(F32) * has_prev, _dot(down, ub))
        after = jnp.where(row == tm - 1, nxt[0:1, :].astype(F32) * has_next, _dot(up, ub))
        return (before * w_ref[0:1, cols] + ub.astype(F32) * w_ref[1:2, cols] + after * w_ref[2:3, cols]
                + b_ref[0:1, cols])

    x0 = conv(m0, p0, n0, 0)
    z = conv(m1, p1, n1, 1) * conv(m2, p2, n2, 2)
    zx_ref[...] = _pack2(z, x0)


def _hy_pre(p, conv_w, conv_b, hy_off, ch, tm=256, halo=16):
    t = p.shape[0]
    tm = min(tm, t)
    nh = tm // halo
    last = t // halo - 1
    cb0 = hy_off // ch

    def main(part):
        return pl.BlockSpec((tm, ch), lambda i: (i, cb0 + part))

    def prev(part):
        return pl.BlockSpec((halo, ch), lambda i: (jnp.maximum(i * nh - 1, 0), cb0 + part))

    def nxt(part):
        return pl.BlockSpec((halo, ch), lambda i: (jnp.minimum((i + 1) * nh, last), cb0 + part))

    return pl.pallas_call(
        _hy_pre_kernel,
        grid=(t // tm,),
        in_specs=[main(0), main(1), main(2), prev(0), prev(1), prev(2), nxt(0), nxt(1), nxt(2),
                  pl.BlockSpec((3, 3 * ch), lambda i: (0, 0)),
                  pl.BlockSpec((1, 3 * ch), lambda i: (0, 0))],
        out_specs=pl.BlockSpec((tm, ch), lambda i: (i, 0)),
        out_shape=jax.ShapeDtypeStruct((t, ch), U32),
        compiler_params=_cparams("parallel"),
        name="hy_conv3",
    )(p, p, p, p, p, p, p, p, p, conv_w, conv_b.reshape(1, 3 * ch))


def _filt_ffn_kernel(f_ref, w1, b1, w2, b2, w3, b3, fr, o_ref):
    hp = lax.Precision.HIGHEST
    freq = fr[:, 0:1]
    h = jnp.sin(freq * (jnp.dot(w1[...], f_ref[...], precision=hp, preferred_element_type=F32) + b1[:, 0:1]))
    h = jnp.sin(freq * (jnp.dot(w2[...], h, precision=hp, preferred_element_type=F32) + b2[:, 0:1]))
    o_ref[...] = jnp.sin(freq * (jnp.dot(w3[...], h, precision=hp, preferred_element_type=F32) + b3[:, 0:1]))


def _filter_ffn(feat_t, w1, b1, w2, b2, w3, b3, freq, tl=2048):
    e, length = feat_t.shape
    ff = w1.shape[1]
    tl = min(tl, length)

    def col(v):
        return jnp.broadcast_to(v.reshape(ff, 1), (ff, 128))

    full = lambda shape: pl.BlockSpec(shape, lambda i: (0, 0))
    return pl.pallas_call(
        _filt_ffn_kernel,
        grid=(length // tl,),
        in_specs=[pl.BlockSpec((e, tl), lambda i: (0, i)),
                  full((ff, e)), full((ff, 128)), full((ff, ff)), full((ff, 128)),
                  full((ff, ff)), full((ff, 128)), full((ff, 128))],
        out_specs=pl.BlockSpec((ff, tl), lambda i: (0, i)),
        out_shape=jax.ShapeDtypeStruct((ff, length), F32),
        compiler_params=_cparams("parallel"),
        name="filter_ffn",
    )(feat_t, w1.T, col(b1), w2.T, col(b2), w3.T, col(b3), col(freq))


def _filt_taps_kernel(hf_ref, hb_ref, tf_ref, tb_ref, wf_ref, wb_ref, dec_ref, hp_ref, ssq_ref):
    i = pl.program_id(0)
    hf = hf_ref[...].astype(BF16)
    hb = hb_ref[...].astype(BF16)
    t_f = tf_ref[:, 0:1]
    t_b = tb_ref[:, 0:1]
    row = lax.broadcasted_iota(jnp.int32, t_b.shape, 0)
    sign_b = jnp.where((row == 0) & (i == 0), 0.0, -1.0)

    @pl.when(i == 0)
    def _():
        ssq_ref[...] = jnp.zeros_like(ssq_ref)

    width = TAPS_COLS
    for c0 in range(0, dec_ref.shape[1], width):
        cols = slice(c0, c0 + width)
        dec = jnp.abs(dec_ref[:, cols])
        fwd = _dot_t0(hf, wf_ref[:, cols]) * jnp.exp(-t_f * dec)
        bwd = _dot_t0(hb, wb_ref[:, cols]) * (jnp.exp(-t_b * dec) * sign_b)
        ssq_ref[:, cols] += jnp.sum(fwd * fwd + bwd * bwd, axis=0, keepdims=True)
        hp_ref[:, cols] = _pack2(fwd, bwd)


def _filter_taps(h_ext, t_ext, w4, decay, tr=512):
    ff, n = h_ext.shape
    length = n // 2
    ch = decay.shape[0]
    tr = min(tr, length)
    nt = length // tr
    return pl.pallas_call(
        _filt_taps_kernel,
        grid=(nt,),
        in_specs=[pl.BlockSpec((ff, tr), lambda i: (0, i)),
                  pl.BlockSpec((ff, tr), lambda i: (0, i + nt)),
                  pl.BlockSpec((tr, 1), lambda i: (i, 0)),
                  pl.BlockSpec((tr, 1), lambda i: (i + nt, 0)),
                  pl.BlockSpec((ff, ch), lambda i: (0, 0)),
                  pl.BlockSpec((ff, ch), lambda i: (0, 1)),
                  pl.BlockSpec((1, ch), lambda i: (0, 0))],
        out_specs=[pl.BlockSpec((tr, ch), lambda i: (i, 0)),
                   pl.BlockSpec((1, ch), lambda i: (0, 0))],
        out_shape=[jax.ShapeDtypeStruct((length, ch), U32), jax.ShapeDtypeStruct((1, ch), F32)],
        compiler_params=_cparams("arbitrary"),
        name="filter_taps",
    )(h_ext, h_ext, t_ext, t_ext, w4, w4, decay.reshape(1, ch))


def _wide_spec(rows):
    return pl.BlockSpec((rows, None, SUBLANES, DFT_WIDTH), lambda g, j: (0, g, 0, j))


def _flat_scratch(rows, arrays=1):
    return [pltpu.VMEM((rows * SUBLANES, LANES), U32)] * (arrays * DFT_WIDTH // LANES)


def _flatten(block, flats):
    for s, f in enumerate(flats):
        f[...] = block[:, :, s * LANES:(s + 1) * LANES].reshape(f.shape)


def _gather_rows(flats, a):
    return jnp.concatenate([f[pl.ds(a, f.shape[0] // SUBLANES, stride=SUBLANES), :] for f in flats], axis=1)


def _lane_groups(flats, per=DFT_SLABS):
    return [(slice(i * per * LANES, (i + 1) * per * LANES), flats[i * per:(i + 1) * per])
            for i in range(len(flats) // per)]


def _dft_a_kernel(x_ref, e_ref, o_ref, *flats, both_halves):
    _flatten(x_ref, flats)
    for lanes, fl in _lane_groups(flats):
        for a in range(SUBLANES):
            hi, lo = _unpack2(_gather_rows(fl, a))
            x = jnp.concatenate([hi, lo], axis=0) if both_halves else hi
            r = _dot(e_ref[a], x.astype(BF16))
            n = r.shape[0] // 2
            o_ref[a, :, lanes] = _pack2(r[:n], r[n:])


def _dft_stage_a(x, e_tab, both_halves):
    n1, m2, kdim = e_tab.shape
    length, c = x.shape
    k2 = length // n1
    assert kdim == (2 * k2 if both_halves else k2)
    return pl.pallas_call(
        functools.partial(_dft_a_kernel, both_halves=both_halves),
        grid=(n1 // SUBLANES, c // DFT_WIDTH),
        in_specs=[_wide_spec(k2), pl.BlockSpec((SUBLANES, m2, kdim), lambda g, j: (g, 0, 0))],
        out_specs=pl.BlockSpec((SUBLANES, m2 // 2, DFT_WIDTH), lambda g, j: (g, 0, j)),
        out_shape=jax.ShapeDtypeStruct((n1, m2 // 2, c), U32),
        scratch_shapes=_flat_scratch(k2),
        compiler_params=_cparams("parallel", "arbitrary"),
        name="dft_stage_a",
    )(x.reshape(k2, n1 // SUBLANES, SUBLANES, c), e_tab)


def _dft_mid_kernel(z_ref, h_ref, f_ref, finv_ref, o_ref, *flats):
    half = len(flats) // 2
    _flatten(z_ref, flats[:half])
    _flatten(h_ref, flats[half:])
    n = z_ref.shape[0]
    for (lanes, fz), (_, fh) in zip(_lane_groups(flats[:half], half), _lane_groups(flats[half:], half)):
        for a in range(SUBLANES):
            zr, zi = _unpack2(_gather_rows(fz, a))
            hr, hi = _unpack2(_gather_rows(fh, a))
            xs = _dot(f_ref[...], jnp.concatenate([zr, zi], axis=0).astype(BF16))
            ks = _dot(f_ref[...], jnp.concatenate([hr, hi], axis=0).astype(BF16))
            xr, xi, kr, ki = xs[:n], xs[n:], ks[:n], ks[n:]
            y = jnp.concatenate([xr * kr - xi * ki, xr * ki + xi * kr], axis=0).astype(BF16)
            cs = _dot(finv_ref[...], y)
            o_ref[a, :, lanes] = _pack2(cs[:n], cs[n:])


def _dft_mid(bz, bh, f_fwd, f_inv):
    n1, n2h, c = bz.shape
    mat = pl.BlockSpec((2 * n1, 2 * n1), lambda g, j: (0, 0))
    view = lambda b: b.reshape(n1, n2h // SUBLANES, SUBLANES, c)
    return pl.pallas_call(
        _dft_mid_kernel,
        grid=(n2h // SUBLANES, c // DFT_WIDTH),
        in_specs=[_wide_spec(n1), _wide_spec(n1), mat, mat],
        out_specs=pl.BlockSpec((SUBLANES, n1, DFT_WIDTH), lambda g, j: (g, 0, j)),
        out_shape=jax.ShapeDtypeStruct((n2h, n1, c), U32),
        scratch_shapes=_flat_scratch(n1, arrays=2),
        compiler_params=_cparams("parallel", "arbitrary"),
        name="dft_mid",
    )(view(bz), view(bh), f_fwd, f_inv)


def _dft_ainv_kernel(c_ref, zx_ref, e_ref, skip_ref, sc_ref, o_ref, *flats):
    half = len(flats) // 2
    _flatten(c_ref, flats[:half])
    _flatten(zx_ref, flats[half:])
    fc, fzx = flats[:half], flats[half:]
    mid = DFT_WIDTH // 2
    for a in range(SUBLANES):
        c_r, c_i = _unpack2(_gather_rows(fc, a))
        y = _dot_t0(e_ref[a], jnp.concatenate([c_r, c_i], axis=0).astype(BF16)) * sc_ref[...]
        z, x0 = _unpack2(_gather_rows(fzx, a))
        val = x0 * (y + z * skip_ref[...])
        o_ref[:, a, :] = _pack2(val[:, :mid], val[:, mid:])


def _dft_stage_a_inv(cc, e_tab, zx, skip, scale):
    n2h, n1, c = cc.shape
    length = zx.shape[0]
    k2 = length // n1
    assert e_tab.shape == (n1, 2 * n2h, k2)
    row = pl.BlockSpec((1, DFT_WIDTH), lambda g, j: (0, j))
    out = pl.pallas_call(
        _dft_ainv_kernel,
        grid=(n1 // SUBLANES, c // DFT_WIDTH),
        in_specs=[_wide_spec(n2h), _wide_spec(k2),
                  pl.BlockSpec((SUBLANES, 2 * n2h, k2), lambda g, j: (g, 0, 0)), row, row],
        out_specs=pl.BlockSpec((k2, None, SUBLANES, DFT_WIDTH // 2), lambda g, j: (0, g, 0, j)),
        out_shape=jax.ShapeDtypeStruct((k2, n1 // SUBLANES, SUBLANES, c // 2), U32),
        scratch_shapes=_flat_scratch(n2h) + _flat_scratch(k2),
        compiler_params=_cparams("parallel", "arbitrary"),
        name="dft_stage_a_inv",
    )(cc.reshape(n2h, n1 // SUBLANES, SUBLANES, c), zx.reshape(k2, n1 // SUBLANES, SUBLANES, c),
      e_tab, skip.reshape(1, c), scale)
    return out.reshape(length, c // 2)


def _dft_tables(n1, n2h):
    n2 = 2 * n2h
    n = n1 * n2
    a = jnp.arange(n1, dtype=jnp.int32)
    b = jnp.arange(n2, dtype=jnp.int32)
    odd = 2 * jnp.arange(n2h, dtype=jnp.int32) + 1
    alpha = ((a[:, None] * odd[None, :]) % (2 * n)).astype(F32) * (math.pi / n)
    beta = ((odd[:, None] * b[None, :]) % (2 * n2)).astype(F32) * (math.pi / n2)
    ar, ai = jnp.cos(alpha)[:, :, None], -jnp.sin(alpha)[:, :, None]
    br, bi = jnp.cos(beta)[None], -jnp.sin(beta)[None]
    er = ar * br - ai * bi
    ei = ar * bi + ai * br
    e_full = jnp.concatenate([er, ei], axis=1).astype(BF16)
    e_half = e_full[:, :, :n2h]
    phi = ((a[:, None] * a[None, :]) % n1).astype(F32) * (2.0 * math.pi / n1)
    fr, fi = jnp.cos(phi), -jnp.sin(phi)
    f_fwd = jnp.concatenate([jnp.concatenate([fr, -fi], axis=1),
                             jnp.concatenate([fi, fr], axis=1)], axis=0).astype(BF16)
    f_inv = jnp.concatenate([jnp.concatenate([fr, fi], axis=1),
                             jnp.concatenate([-fi, fr], axis=1)], axis=0).astype(BF16)
    return e_half, e_full, f_fwd, f_inv


def _hyena_features(length):
    r = jnp.arange(2 * length, dtype=jnp.int32)
    pos = jnp.where(r < length, r, 2 * length - r)
    t = pos.astype(F32) / (length - 1.0)
    w = 2.0 * math.pi * pos.astype(F32)[None, :] / length
    f = jnp.linspace(1e-4, HY_BANDS - 1.0, HY_BANDS, dtype=F32)[:, None]
    return t[:, None], jnp.concatenate([t[None, :], jnp.cos(f * w), -jnp.sin(f * w)], axis=0)


def _merge_kernel(x_ref, yr_ref, hy_ref, gr_ref, gh_ref, gate_ref, wr_ref, wh_ref, wo_ref, o_ref):
    ret_out = _dot(yr_ref[...], wr_ref[...])
    hi, lo = _unpack2(hy_ref[...])
    half = DFT_WIDTH // 2
    hy = jnp.concatenate([part[:, b:b + half] for b in range(0, hi.shape[1], half) for part in (hi, lo)], axis=1)
    hy_out = _dot(hy.astype(BF16), wh_ref[...])
    m = (jax.nn.sigmoid(gr_ref[...].astype(F32)) * ret_out
         + jax.nn.sigmoid(gh_ref[...].astype(F32)) * hy_out)
    o_ref[...] = x_ref[...] + gate_ref[...] * _dot(m.astype(BF16), wo_ref[...])


def _merge(x, y_ret, hy_in, p, gate_off, gate, w_ret_o, w_hy_o, w_out, tm=256):
    t, d = x.shape
    tm = min(tm, t)
    gb = gate_off // d
    tile = lambda: pl.BlockSpec((tm, d), lambda i: (i, 0))
    weight = lambda w: pl.BlockSpec((None,) + w.shape[1:], lambda i: (0, 0, 0), pipeline_mode=pl.Buffered(1))
    return pl.pallas_call(
        _merge_kernel,
        grid=(t // tm,),
        in_specs=[tile(), tile(), pl.BlockSpec((tm, d // 2), lambda i: (i, 0)),
                  pl.BlockSpec((tm, d), lambda i: (i, gb)),
                  pl.BlockSpec((tm, d), lambda i: (i, gb + 1)),
                  pl.BlockSpec((1, d), lambda i: (0, 0)),
                  weight(w_ret_o), weight(w_hy_o), weight(w_out)],
        out_specs=tile(),
        out_shape=jax.ShapeDtypeStruct((t, d), F32),
        compiler_params=_cparams("parallel"),
        name="merge_out",
    )(x, y_ret, hy_in, p, p, gate.reshape(1, d), w_ret_o, w_hy_o, w_out)


def _mod_rows(mods, first):
    return jnp.pad(mods[first:first + 3], ((0, 5), (0, 0)))


def kernel(x, c, ctx, c_ctx, w_ada, b_ada, norm_g, ffn_up, ffn_down, w_in, ret_log_gamma, ret_gn_g,
           ret_gn_b, w_ret_o, hy_conv_w, hy_conv_b, hy_ff_w1, hy_ff_b1, hy_ff_w2, hy_ff_b2, hy_ff_w3,
           hy_ff_b3, hy_ff_w4, hy_sin_freq, hy_decay, hy_bias, w_hy_o, w_out, final_norm_g):
    batch, t, d = x.shape
    assert batch == 1 and w_ada.shape[0] == 1, "single sample, single layer"
    hy_w = hy_decay.shape[1]
    q_off, k_off = 0, RET_Q
    v_off = k_off + RET_Q
    g_off = v_off + RET_V
    hy_off = g_off + RET_V
    gate_off = hy_off + 3 * hy_w
    assert w_in.shape[2] == gate_off + 2 * d and hy_w == d

    xs, cs = x[0], ctx[0]
    up, down, w_in_b = ffn_up.astype(BF16), ffn_down.astype(BF16), w_in.astype(BF16)
    lg = ret_log_gamma[0].astype(F32)

    cv = jnp.concatenate([c, c_ctx[None, :], jnp.zeros((6, d), F32)], axis=0)
    mods = _adaln(cv, w_ada[0], b_ada[0]).reshape(8, N_MOD, d)
    mx, mc = mods[0], mods[1]

    xs = _ffn_half(xs, _mod_rows(mx, 0), norm_g[0, 0], up, down, 0)
    cs = _ffn_half(cs, _mod_rows(mc, 0), norm_g[0, 0], up, down, 0)

    pkv = _in_proj(cs, _mod_rows(mc, 3), norm_g[0, 1], w_in_b, k_off, g_off - k_off)
    st0 = _ctx_states(pkv, lg)

    cos_t, sin_t = _rope_tables(t)
    p = _in_proj(xs, _mod_rows(mx, 3), norm_g[0, 1], w_in_b, 0, w_in_b.shape[2],
                 rope=(cos_t, sin_t, v_off))

    y_ret = _retention(p, st0, lg, ret_gn_g[0], ret_gn_b[0], q_off, k_off, v_off, g_off)

    zx = _hy_pre(p, hy_conv_w[0], hy_conv_b[0], hy_off, hy_w)
    t_ext, feat_t = _hyena_features(t)
    e_pad = (-feat_t.shape[0]) % 8
    feat_t = jnp.pad(feat_t, ((0, e_pad), (0, 0)))
    w1 = jnp.pad(hy_ff_w1[0], ((0, e_pad), (0, 0)))
    h_ext = _filter_ffn(feat_t, w1, hy_ff_b1[0], hy_ff_w2[0], hy_ff_b2[0], hy_ff_w3[0], hy_ff_b3[0],
                        hy_sin_freq[0])
    h_taps, ssq = _filter_taps(h_ext, t_ext, hy_ff_w4[0].astype(BF16), hy_decay[0])

    n1 = DFT_N1
    e_half, e_full, f_fwd, f_inv = _dft_tables(n1, t // n1)
    bz = _dft_stage_a(zx, e_half, both_halves=False)
    bh = _dft_stage_a(h_taps, e_full, both_halves=True)
    cc = _dft_mid(bz, bh, f_fwd, f_inv)
    scale = lax.rsqrt(ssq + EPS) * (2.0 / (2 * t))
    hy_in = _dft_stage_a_inv(cc, e_half, zx, hy_bias[0], scale)

    xs = _merge(xs, y_ret, hy_in, p, gate_off, mx[5], w_ret_o.astype(BF16), w_hy_o.astype(BF16),
                w_out.astype(BF16))

    out = _ffn_half(xs, _mod_rows(mx, 6), norm_g[0, 2], up, down, 1, final_g=final_norm_g)
    return out[None]
```

```python
import functools
import math

import jax
import jax.numpy as jnp
from jax import lax
from jax.experimental import pallas as pl
from jax.experimental.pallas import tpu as pltpu

F32 = jnp.float32
BF16 = jnp.bfloat16
U32 = jnp.uint32

N_MOD = 9
GRID_W = 64
RET_HEADS = 8
RET_DK = 128
RET_DV = 256
RET_CHUNK = 128
RET_Q = RET_HEADS * RET_DK
RET_V = RET_HEADS * RET_DV
RET_PAIR = 2
HY_BANDS = 16
ROPE_BASE = 10000.0
EPS = 1e-6

TAPS_COLS = 256
DFT_N1 = 128
SUBLANES = 8
LANES = 128
DFT_SLABS = 4
DFT_WIDTH = 1024
VMEM_BYTES = 64 * 1024 * 1024
VMEM_LIMIT = VMEM_BYTES - 8 * 1024 * 1024


def _cparams(*sem, vmem=VMEM_LIMIT):
    return pltpu.CompilerParams(dimension_semantics=sem, vmem_limit_bytes=vmem)


def _dot(a, b):
    return jnp.dot(a, b, preferred_element_type=F32)


def _dot_t0(a, b):
    return lax.dot_general(a, b, (((0,), (0,)), ((), ())), preferred_element_type=F32)


def _silu(v):
    return v * jax.nn.sigmoid(v)


def _norm_mod(x, g, shift, scale):
    return x * lax.rsqrt(jnp.mean(x * x, axis=-1, keepdims=True) + EPS) * (g * (1.0 + scale)) + shift


def _pack2(hi, lo):
    hb = lax.bitcast_convert_type(hi.astype(BF16).astype(F32), U32)
    lb = lax.bitcast_convert_type(lo.astype(BF16).astype(F32), U32)
    return hb | (lb >> 16)


def _unpack2(w):
    hi = lax.bitcast_convert_type(w & jnp.uint32(0xFFFF0000), F32)
    lo = lax.bitcast_convert_type(w << 16, F32)
    return hi, lo


def _adaln_kernel(c_ref, w_ref, b_ref, o_ref):
    s = _silu(c_ref[...]).astype(BF16)
    o_ref[...] = _dot(s, w_ref[...].astype(BF16)) + b_ref[...]


def _adaln(cv, w, b, tn=1024):
    rows, d = cv.shape
    n = w.shape[1]
    return pl.pallas_call(
        _adaln_kernel,
        grid=(n // tn,),
        in_specs=[pl.BlockSpec((rows, d), lambda j: (0, 0)),
                  pl.BlockSpec((d, tn), lambda j: (0, j)),
                  pl.BlockSpec((1, tn), lambda j: (0, j))],
        out_specs=pl.BlockSpec((rows, tn), lambda j: (0, j)),
        out_shape=jax.ShapeDtypeStruct((rows, n), F32),
        compiler_params=_cparams("arbitrary"),
        name="adaln",
    )(cv, w, b.reshape(1, n))


def _ffn_kernel(x_ref, xn_ref, mod_ref, g_ref, wa_ref, wg_ref, wd_ref, *rest, final, chunk):
    if final:
        fg_ref, o_ref, h_scr, acc_scr = rest
    else:
        o_ref, h_scr, acc_scr = rest
    i = pl.program_id(0)
    j = pl.program_id(1)
    cur = i % 2

    def normed(x):
        return _norm_mod(x, g_ref[...], mod_ref[0:1, :], mod_ref[1:2, :]).astype(BF16)

    @pl.when((i == 0) & (j == 0))
    def _():
        h_scr[0] = normed(x_ref[...])

    @pl.when(j == 0)
    def _():
        acc_scr[...] = jnp.zeros_like(acc_scr)

    h = h_scr[cur]
    a = _dot(h, wa_ref[...])
    g = _dot(h, wg_ref[...])

    c = j % (x_ref.shape[0] // chunk)
    rows = pl.ds(pl.multiple_of(c * chunk, chunk), chunk)
    h_scr[1 - cur, rows, :] = normed(xn_ref[rows, :])

    acc_scr[...] += _dot((_silu(a) * g).astype(BF16), wd_ref[...])

    @pl.when(j == pl.num_programs(1) - 1)
    def _():
        out = x_ref[...] + (0.5 * mod_ref[2:3, :]) * acc_scr[...]
        if final:
            out = out * lax.rsqrt(jnp.mean(out * out, axis=-1, keepdims=True) + EPS) * fg_ref[...]
        o_ref[...] = out


def _ffn_half(x, mod, g, w_up, w_down, layer, final_g=None, tm=512, tf=512):
    t, d = x.shape
    dff = w_down.shape[2]
    tm = min(tm, t)
    nf = dff // tf
    n_tiles = t // tm
    chunk = max(tm // min(nf, 8), 16)
    assert tm % chunk == 0 and tm // chunk <= nf
    final = final_g is not None
    in_specs = [pl.BlockSpec((tm, d), lambda i, j: (i, 0)),
                pl.BlockSpec((tm, d), lambda i, j: (jnp.minimum(i + 1, n_tiles - 1), 0)),
                pl.BlockSpec((8, d), lambda i, j: (0, 0)),
                pl.BlockSpec((1, d), lambda i, j: (0, 0)),
                pl.BlockSpec((None, None, d, tf), lambda i, j: (0, layer, 0, j)),
                pl.BlockSpec((None, None, d, tf), lambda i, j: (0, layer, 0, nf + j)),
                pl.BlockSpec((None, None, tf, d), lambda i, j: (0, layer, j, 0))]
    args = [x, x, mod, g.reshape(1, d), w_up, w_up, w_down]
    if final:
        in_specs.append(pl.BlockSpec((1, d), lambda i, j: (0, 0)))
        args.append(final_g.reshape(1, d))
    return pl.pallas_call(
        functools.partial(_ffn_kernel, final=final, chunk=chunk),
        grid=(n_tiles, nf),
        in_specs=in_specs,
        out_specs=pl.BlockSpec((tm, d), lambda i, j: (i, 0)),
        out_shape=jax.ShapeDtypeStruct((t, d), F32),
        scratch_shapes=[pltpu.VMEM((2, tm, d), BF16), pltpu.VMEM((tm, d), F32)],
        compiler_params=_cparams("arbitrary", "arbitrary"),
        name="ffn_final" if final else "ffn_half",
    )(*args)


def _rope(x, cos, sin):
    lane = lax.broadcasted_iota(jnp.int32, x.shape, 1)
    partner = jnp.where((lane % 64) < 32, pltpu.roll(x, 96, 1), pltpu.roll(x, 32, 1))
    return x * cos + partner * sin


def _inproj_kernel(x_ref, mod_ref, g_ref, w_ref, *rest, rope_tiles):
    if rope_tiles:
        cos_ref, sin_ref, o_ref, h_scr = rest
    else:
        o_ref, h_scr = rest
    j = pl.program_id(1)

    @pl.when(j == 0)
    def _():
        h = _norm_mod(x_ref[...], g_ref[...], mod_ref[0:1, :], mod_ref[1:2, :])
        h_scr[...] = h.astype(BF16)

    if rope_tiles:
        @pl.when(j < rope_tiles)
        def _():
            acc = _dot(h_scr[...], w_ref[...])
            scale = jnp.where(j == 0, RET_DK ** -0.5, 1.0)
            cos = cos_ref[...] * scale
            sin = sin_ref[...] * scale
            for hb in range(acc.shape[1] // RET_DK):
                cols = slice(hb * RET_DK, (hb + 1) * RET_DK)
                o_ref[:, cols] = _rope(acc[:, cols], cos, sin).astype(BF16)

        @pl.when(j >= rope_tiles)
        def _():
            o_ref[...] = _dot(h_scr[...], w_ref[...]).astype(BF16)
    else:
        o_ref[...] = _dot(h_scr[...], w_ref[...]).astype(BF16)


def _in_proj(x, mod, g, w, col0, ncols, rope=None, tm=1024, tn=1024):
    t, d = x.shape
    tm = min(tm, t)
    off = col0 // tn
    in_specs = [pl.BlockSpec((tm, d), lambda i, j: (i, 0)),
                pl.BlockSpec((8, d), lambda i, j: (0, 0)),
                pl.BlockSpec((1, d), lambda i, j: (0, 0)),
                pl.BlockSpec((None, d, tn), lambda i, j: (0, 0, j + off))]
    args = [x, mod, g.reshape(1, d), w]
    rope_tiles = 0
    if rope is not None:
        cos_t, sin_t, qk_cols = rope
        assert tn == RET_Q and qk_cols == 2 * RET_Q and col0 == 0
        rope_tiles = qk_cols // tn
        in_specs += [pl.BlockSpec((tm, RET_DK), lambda i, j: (i, 0))] * 2
        args += [cos_t, sin_t]
    return pl.pallas_call(
        functools.partial(_inproj_kernel, rope_tiles=rope_tiles),
        grid=(t // tm, ncols // tn),
        in_specs=in_specs,
        out_specs=pl.BlockSpec((tm, tn), lambda i, j: (i, j)),
        out_shape=jax.ShapeDtypeStruct((t, ncols), BF16),
        scratch_shapes=[pltpu.VMEM((tm, d), BF16)],
        compiler_params=_cparams("parallel", "arbitrary"),
        name="in_proj",
    )(*args)


def _rope_tables(t):
    half = RET_DK // 2
    n_rows = t // GRID_W
    inv = ROPE_BASE ** (-jnp.arange(0, half, 2, dtype=F32) / half)
    ang_r = jnp.arange(n_rows, dtype=jnp.int32).astype(F32)[:, None] * inv[None, :]
    ang_c = jnp.arange(GRID_W, dtype=jnp.int32).astype(F32)[:, None] * inv[None, :]
    by_row = lambda v: jnp.repeat(v, GRID_W, axis=0)
    by_col = lambda v: jnp.tile(v, (n_rows, 1))
    cos_r, sin_r, cos_c, sin_c = jnp.cos(ang_r), jnp.sin(ang_r), jnp.cos(ang_c), jnp.sin(ang_c)
    cos_t = jnp.concatenate([by_row(cos_r)] * 2 + [by_col(cos_c)] * 2, axis=-1)
    sin_t = jnp.concatenate([by_row(-sin_r), by_row(sin_r), by_col(-sin_c), by_col(sin_c)], axis=-1)
    return cos_t, sin_t


def _ctx_state_kernel(lg_ref, k_ref, v_ref, o_ref):
    h = pl.program_id(0)
    lc = k_ref.shape[0]
    m = lax.broadcasted_iota(jnp.int32, (lc, 1), 0).astype(F32)
    k = k_ref[...].astype(F32)
    v = v_ref[...]
    w_f = jnp.exp(lg_ref[0, h] * (lc - 1.0 - m))
    w_b = jnp.exp(lg_ref[1, h] * m)
    o_ref[0, 0] = _dot_t0((k * w_f).astype(BF16), v)
    o_ref[1, 0] = _dot_t0((k * w_b).astype(BF16), v)


def _ctx_states(pkv, lg):
    lc = pkv.shape[0]
    return pl.pallas_call(
        _ctx_state_kernel,
        grid=(RET_HEADS,),
        in_specs=[pl.BlockSpec(memory_space=pltpu.SMEM),
                  pl.BlockSpec((lc, RET_DK), lambda h: (0, h)),
                  pl.BlockSpec((lc, RET_DV), lambda h: (0, RET_Q // RET_DV + h))],
        out_specs=pl.BlockSpec((2, 1, RET_DK, RET_DV), lambda h: (0, h, 0, 0)),
        out_shape=jax.ShapeDtypeStruct((2, RET_HEADS, RET_DK, RET_DV), F32),
        compiler_params=_cparams("arbitrary"),
        name="ctx_states",
    )(lg, pkv, pkv)


def _ret_kernel(lg_ref, q_ref, k_ref, v_ref, g_ref, st0_ref, gng_ref, gnb_ref,
                o_ref, sb_all, sf_scr, sb_scr, *, nb, bc):
    c_len = RET_CHUNK
    hp = pl.program_id(0)
    i = pl.program_id(1)
    pair = range(RET_PAIR)
    lgf = [lg_ref[0, hp * RET_PAIR + e] for e in pair]
    lgb = [lg_ref[1, hp * RET_PAIR + e] for e in pair]
    kcols = [slice(e * RET_DK, (e + 1) * RET_DK) for e in pair]
    vcols = [slice(e * RET_DV, (e + 1) * RET_DV) for e in pair]
    pos = lax.broadcasted_iota(jnp.int32, (c_len, 1), 0).astype(F32)
    ones_row = jnp.ones((1, RET_DV), F32)

    @pl.when(i == 0)
    def _():
        for e in pair:
            sf_scr[e] = st0_ref[0, e]
            sb_scr[e] = st0_ref[1, e]

    @pl.when(i < nb)
    def _():
        rb = nb - 1 - i
        zeta_b = [jnp.exp(lgb[e] * pos) for e in pair]
        cd_b = [jnp.exp((lgb[e] * c_len) * ones_row) for e in pair]
        s_b = [sb_scr[e] for e in pair]
        for c in reversed(range(bc)):
            rows = slice(c * c_len, (c + 1) * c_len)
            for e in pair:
                sb_all[e, rb * bc + c] = s_b[e].astype(BF16)
                kz = (k_ref[rows, kcols[e]].astype(F32) * zeta_b[e]).astype(BF16)
                s_b[e] = s_b[e] * cd_b[e] + _dot_t0(kz, v_ref[rows, vcols[e]])
        for e in pair:
            sb_scr[e] = s_b[e]

    @pl.when(i >= nb)
    def _():
        rb = i - nb
        a = lax.broadcasted_iota(jnp.int32, (c_len, c_len), 0)
        b = lax.broadcasted_iota(jnp.int32, (c_len, c_len), 1)
        d = (a - b).astype(F32)
        decay = [jnp.where(d > 0, jnp.exp(lgf[e] * jnp.maximum(d, 0.0)),
                           jnp.where(d < 0, jnp.exp(lgb[e] * jnp.maximum(-d, 0.0)), 2.0)) for e in pair]
        xi_f = [jnp.exp(lgf[e] * (pos + 1.0)) for e in pair]
        xi_b = [jnp.exp(lgb[e] * (c_len - pos)) for e in pair]
        zeta_f = [jnp.exp(lgf[e] * (c_len - 1.0 - pos)) for e in pair]
        cd_f = [jnp.exp((lgf[e] * c_len) * ones_row) for e in pair]
        s_f = [sf_scr[e] for e in pair]
        for c in range(bc):
            rows = slice(c * c_len, (c + 1) * c_len)
            for e in pair:
                q = q_ref[rows, kcols[e]]
                k = k_ref[rows, kcols[e]]
                v = v_ref[rows, vcols[e]]
                s = lax.dot_general(q, k, (((1,), (1,)), ((), ())), preferred_element_type=F32) * decay[e]
                y = (_dot(s.astype(BF16), v)
                     + xi_f[e] * _dot(q, s_f[e].astype(BF16))
                     + xi_b[e] * _dot(q, sb_all[e, rb * bc + c]))
                kz = (k.astype(F32) * zeta_f[e]).astype(BF16)
                s_f[e] = s_f[e] * cd_f[e] + _dot_t0(kz, v)
                mu = jnp.mean(y, axis=-1, keepdims=True)
                yc = y - mu
                var = jnp.mean(yc * yc, axis=-1, keepdims=True)
                yn = yc * lax.rsqrt(var + EPS) * gng_ref[:, vcols[e]] + gnb_ref[:, vcols[e]]
                o_ref[rows, vcols[e]] = (_silu(g_ref[rows, vcols[e]].astype(F32)) * yn).astype(BF16)
        for e in pair:
            sf_scr[e] = s_f[e]


def _retention(p, st0, lg, gn_g, gn_b, q_off, k_off, v_off, g_off, bc=16):
    t = p.shape[0]
    bc = min(bc, t // RET_CHUNK)
    rows = bc * RET_CHUNK
    nb = t // rows
    kw, vw = RET_PAIR * RET_DK, RET_PAIR * RET_DV

    def kv_rb(i):
        return jnp.where(i < nb, nb - 1 - i, i - nb)

    def q_rb(i):
        return jnp.maximum(i - nb, 0)

    return pl.pallas_call(
        functools.partial(_ret_kernel, nb=nb, bc=bc),
        grid=(RET_HEADS // RET_PAIR, 2 * nb),
        in_specs=[pl.BlockSpec(memory_space=pltpu.SMEM),
                  pl.BlockSpec((rows, kw), lambda h, i: (q_rb(i), q_off // kw + h)),
                  pl.BlockSpec((rows, kw), lambda h, i: (kv_rb(i), k_off // kw + h)),
                  pl.BlockSpec((rows, vw), lambda h, i: (kv_rb(i), v_off // vw + h)),
                  pl.BlockSpec((rows, vw), lambda h, i: (q_rb(i), g_off // vw + h)),
                  pl.BlockSpec((2, RET_PAIR, RET_DK, RET_DV), lambda h, i: (0, h, 0, 0)),
                  pl.BlockSpec((1, vw), lambda h, i: (0, h)),
                  pl.BlockSpec((1, vw), lambda h, i: (0, h))],
        out_specs=pl.BlockSpec((rows, vw), lambda h, i: (q_rb(i), h)),
        out_shape=jax.ShapeDtypeStruct((t, RET_V), BF16),
        scratch_shapes=[pltpu.VMEM((RET_PAIR, t // RET_CHUNK, RET_DK, RET_DV), BF16),
                        pltpu.VMEM((RET_PAIR, RET_DK, RET_DV), F32),
                        pltpu.VMEM((RET_PAIR, RET_DK, RET_DV), F32)],
        compiler_params=_cparams("arbitrary", "arbitrary"),
        name="retention",
    )(lg, p, p, p, p, st0, gn_g.reshape(1, RET_V), gn_b.reshape(1, RET_V))


def _hy_pre_kernel(m0, m1, m2, p0, p1, p2, n0, n1, n2, w_ref, b_ref, zx_ref):
    i = pl.program_id(0)
    tm, ch = m0.shape
    row = lax.broadcasted_iota(jnp.int32, (tm, 1), 0)
    has_prev = (i > 0).astype(F32)
    has_next = (i < pl.num_programs(0) - 1).astype(F32)
    halo = p0.shape[0]
    r_idx = lax.broadcasted_iota(jnp.int32, (tm, tm), 0)
    c_idx = lax.broadcasted_iota(jnp.int32, (tm, tm), 1)
    down = jnp.where(r_idx == c_idx + 1, 1.0, 0.0).astype(BF16)
    up = jnp.where(r_idx + 1 == c_idx, 1.0, 0.0).astype(BF16)

    def conv(main, prev, nxt, part):
        cols = slice(part * ch, (part + 1) * ch)
        ub = main[...]
        before = jnp.where(row == 0, prev[halo - 1:halo, :].astype(F32) * has_prev, _dot(down, ub))
        after = jnp.where(row == tm - 1, nxt[0:1, :].astype(F32) * has_next, _dot(up, ub))
        return (before * w_ref[0:1, cols] + ub.astype(F32) * w_ref[1:2, cols] + after * w_ref[2:3, cols]
                + b_ref[0:1, cols])

    x0 = conv(m0, p0, n0, 0)
    z = conv(m1, p1, n1, 1) * conv(m2, p2, n2, 2)
    zx_ref[...] = _pack2(z, x0)


def _hy_pre(p, conv_w, conv_b, hy_off, ch, tm=256, halo=16):
    t = p.shape[0]
    tm = min(tm, t)
    nh = tm // halo
    last = t // halo - 1
    cb0 = hy_off // ch

    def main(part):
        return pl.BlockSpec((tm, ch), lambda i: (i, cb0 + part))

    def prev(part):
        return pl.BlockSpec((halo, ch), lambda i: (jnp.maximum(i * nh - 1, 0), cb0 + part))

    def nxt(part):
        return pl.BlockSpec((halo, ch), lambda i: (jnp.minimum((i + 1) * nh, last), cb0 + part))

    return pl.pallas_call(
        _hy_pre_kernel,
        grid=(t // tm,),
        in_specs=[main(0), main(1), main(2), prev(0), prev(1), prev(2), nxt(0), nxt(1), nxt(2),
                  pl.BlockSpec((3, 3 * ch), lambda i: (0, 0)),
                  pl.BlockSpec((1, 3 * ch), lambda i: (0, 0))],
        out_specs=pl.BlockSpec((tm, ch), lambda i: (i, 0)),
        out_shape=jax.ShapeDtypeStruct((t, ch), U32),
        compiler_params=_cparams("parallel"),
        name="hy_conv3",
    )(p, p, p, p, p, p, p, p, p, conv_w, conv_b.reshape(1, 3 * ch))


def _filt_ffn_kernel(f_ref, w1, b1, w2, b2, w3, b3, fr, o_ref):
    hp = lax.Precision.HIGHEST
    freq = fr[:, 0:1]
    h = jnp.sin(freq * (jnp.dot(w1[...], f_ref[...], precision=hp, preferred_element_type=F32) + b1[:, 0:1]))
    h = jnp.sin(freq * (jnp.dot(w2[...], h, precision=hp, preferred_element_type=F32) + b2[:, 0:1]))
    o_ref[...] = jnp.sin(freq * (jnp.dot(w3[...], h, precision=hp, preferred_element_type=F32) + b3[:, 0:1]))


def _filter_ffn(feat_t, w1, b1, w2, b2, w3, b3, freq, tl=2048):
    e, length = feat_t.shape
    ff = w1.shape[1]
    tl = min(tl, length)

    def col(v):
        return jnp.broadcast_to(v.reshape(ff, 1), (ff, 128))

    full = lambda shape: pl.BlockSpec(shape, lambda i: (0, 0))
    return pl.pallas_call(
        _filt_ffn_kernel,
        grid=(length // tl,),
        in_specs=[pl.BlockSpec((e, tl), lambda i: (0, i)),
                  full((ff, e)), full((ff, 128)), full((ff, ff)), full((ff, 128)),
                  full((ff, ff)), full((ff, 128)), full((ff, 128))],
        out_specs=pl.BlockSpec((ff, tl), lambda i: (0, i)),
        out_shape=jax.ShapeDtypeStruct((ff, length), F32),
        compiler_params=_cparams("parallel"),
        name="filter_ffn",
    )(feat_t, w1.T, col(b1), w2.T, col(b2), w3.T, col(b3), col(freq))


def _filt_taps_kernel(hf_ref, hb_ref, hn_ref, tf_ref, tb_ref, wf_ref, wb_ref, dec_ref, hp_ref, ssq_ref):
    i = pl.program_id(0)
    tr = hf_ref.shape[1]
    hf = hf_ref[...].astype(BF16)
    src = lax.broadcasted_iota(jnp.int32, (tr, tr), 0)
    dst = lax.broadcasted_iota(jnp.int32, (tr, tr), 1)
    rev_shift = jnp.where(src + dst == tr, 1.0, 0.0).astype(BF16)
    hb = _dot(hb_ref[...].astype(BF16), rev_shift)
    first = hn_ref[:, 0:1].astype(BF16).astype(F32)
    hb = jnp.where(lax.broadcasted_iota(jnp.int32, (1, tr), 1) == 0, first, hb).astype(BF16)
    t_f = tf_ref[:, 0:1]
    t_b = tb_ref[:, 0:1]
    row = lax.broadcasted_iota(jnp.int32, t_b.shape, 0)
    sign_b = jnp.where((row == 0) & (i == 0), 0.0, -1.0)

    @pl.when(i == 0)
    def _():
        ssq_ref[...] = jnp.zeros_like(ssq_ref)

    width = TAPS_COLS
    for c0 in range(0, dec_ref.shape[1], width):
        cols = slice(c0, c0 + width)
        dec = jnp.abs(dec_ref[:, cols])
        fwd = _dot_t0(hf, wf_ref[:, cols]) * jnp.exp(-t_f * dec)
        bwd = _dot_t0(hb, wb_ref[:, cols]) * (jnp.exp(-t_b * dec) * sign_b)
        ssq_ref[:, cols] += jnp.sum(fwd * fwd + bwd * bwd, axis=0, keepdims=True)
        hp_ref[:, cols] = _pack2(fwd, bwd)


def _filter_taps(h_pos, t_ext, w4, decay, tr=512):
    ff, length = h_pos.shape
    ch = decay.shape[0]
    tr = min(tr, length)
    nt = length // tr
    return pl.pallas_call(
        _filt_taps_kernel,
        grid=(nt,),
        in_specs=[pl.BlockSpec((ff, tr), lambda i: (0, i)),
                  pl.BlockSpec((ff, tr), lambda i: (0, nt - 1 - i)),
                  pl.BlockSpec((ff, tr), lambda i: (0, jnp.minimum(nt - i, nt - 1))),
                  pl.BlockSpec((tr, 1), lambda i: (i, 0)),
                  pl.BlockSpec((tr, 1), lambda i: (i + nt, 0)),
                  pl.BlockSpec((ff, ch), lambda i: (0, 0)),
                  pl.BlockSpec((ff, ch), lambda i: (0, 1)),
                  pl.BlockSpec((1, ch), lambda i: (0, 0))],
        out_specs=[pl.BlockSpec((tr, ch), lambda i: (i, 0)),
                   pl.BlockSpec((1, ch), lambda i: (0, 0))],
        out_shape=[jax.ShapeDtypeStruct((length, ch), U32), jax.ShapeDtypeStruct((1, ch), F32)],
        compiler_params=_cparams("arbitrary"),
        name="filter_taps",
    )(h_pos, h_pos, h_pos, t_ext, t_ext, w4, w4, decay.reshape(1, ch))


def _wide_spec(rows):
    return pl.BlockSpec((rows, None, SUBLANES, DFT_WIDTH), lambda g, j: (0, g, 0, j))


def _flat_scratch(rows, arrays=1):
    return [pltpu.VMEM((rows * SUBLANES, LANES), U32)] * (arrays * DFT_WIDTH // LANES)


def _flatten(block, flats):
    for s, f in enumerate(flats):
        f[...] = block[:, :, s * LANES:(s + 1) * LANES].reshape(f.shape)


def _gather_rows(flats, a):
    return jnp.concatenate([f[pl.ds(a, f.shape[0] // SUBLANES, stride=SUBLANES), :] for f in flats], axis=1)


def _lane_groups(flats, per=DFT_SLABS):
    return [(slice(i * per * LANES, (i + 1) * per * LANES), flats[i * per:(i + 1) * per])
            for i in range(len(flats) // per)]


def _dft_a_kernel(x_ref, e_ref, o_ref, *flats, both_halves):
    _flatten(x_ref, flats)
    for lanes, fl in _lane_groups(flats):
        for a in range(SUBLANES):
            hi, lo = _unpack2(_gather_rows(fl, a))
            x = jnp.concatenate([hi, lo], axis=0) if both_halves else hi
            r = _dot(e_ref[a], x.astype(BF16))
            n = r.shape[0] // 2
            o_ref[a, :, lanes] = _pack2(r[:n], r[n:])


def _dft_stage_a(x, e_tab, both_halves):
    n1, m2, kdim = e_tab.shape
    length, c = x.shape
    k2 = length // n1
    assert kdim == (2 * k2 if both_halves else k2)
    return pl.pallas_call(
        functools.partial(_dft_a_kernel, both_halves=both_halves),
        grid=(n1 // SUBLANES, c // DFT_WIDTH),
        in_specs=[_wide_spec(k2), pl.BlockSpec((SUBLANES, m2, kdim), lambda g, j: (g, 0, 0))],
        out_specs=pl.BlockSpec((SUBLANES, m2 // 2, DFT_WIDTH), lambda g, j: (g, 0, j)),
        out_shape=jax.ShapeDtypeStruct((n1, m2 // 2, c), U32),
        scratch_shapes=_flat_scratch(k2),
        compiler_params=_cparams("parallel", "arbitrary"),
        name="dft_stage_a",
    )(x.reshape(k2, n1 // SUBLANES, SUBLANES, c), e_tab)


def _dft_mid_kernel(z_ref, h_ref, f_ref, finv_ref, o_ref, *flats):
    half = len(flats) // 2
    _flatten(z_ref, flats[:half])
    _flatten(h_ref, flats[half:])
    n = z_ref.shape[0]
    for (lanes, fz), (_, fh) in zip(_lane_groups(flats[:half], half), _lane_groups(flats[half:], half)):
        for a in range(SUBLANES):
            zr, zi = _unpack2(_gather_rows(fz, a))
            hr, hi = _unpack2(_gather_rows(fh, a))
            xs = _dot(f_ref[...], jnp.concatenate([zr, zi], axis=0).astype(BF16))
            ks = _dot(f_ref[...], jnp.concatenate([hr, hi], axis=0).astype(BF16))
            xr, xi, kr, ki = xs[:n], xs[n:], ks[:n], ks[n:]
            y = jnp.concatenate([xr * kr - xi * ki, xr * ki + xi * kr], axis=0).astype(BF16)
            cs = _dot(finv_ref[...], y)
            o_ref[a, :, lanes] = _pack2(cs[:n], cs[n:])


def _dft_mid(bz, bh, f_fwd, f_inv):
    n1, n2h, c = bz.shape
    mat = pl.BlockSpec((2 * n1, 2 * n1), lambda g, j: (0, 0))
    view = lambda b: b.reshape(n1, n2h // SUBLANES, SUBLANES, c)
    return pl.pallas_call(
        _dft_mid_kernel,
        grid=(n2h // SUBLANES, c // DFT_WIDTH),
        in_specs=[_wide_spec(n1), _wide_spec(n1), mat, mat],
        out_specs=pl.BlockSpec((SUBLANES, n1, DFT_WIDTH), lambda g, j: (g, 0, j)),
        out_shape=jax.ShapeDtypeStruct((n2h, n1, c), U32),
        scratch_shapes=_flat_scratch(n1, arrays=2),
        compiler_params=_cparams("parallel", "arbitrary"),
        name="dft_mid",
    )(view(bz), view(bh), f_fwd, f_inv)


def _dft_ainv_kernel(c_ref, zx_ref, e_ref, skip_ref, sc_ref, o_ref, *flats):
    half = len(flats) // 2
    _flatten(c_ref, flats[:half])
    _flatten(zx_ref, flats[half:])
    fc, fzx = flats[:half], flats[half:]
    mid = DFT_WIDTH // 2
    for a in range(SUBLANES):
        c_r, c_i = _unpack2(_gather_rows(fc, a))
        y = _dot_t0(e_ref[a], jnp.concatenate([c_r, c_i], axis=0).astype(BF16)) * sc_ref[...]
        z, x0 = _unpack2(_gather_rows(fzx, a))
        val = x0 * (y + z * skip_ref[...])
        o_ref[:, a, :] = _pack2(val[:, :mid], val[:, mid:])


def _dft_stage_a_inv(cc, e_tab, zx, skip, scale):
    n2h, n1, c = cc.shape
    length = zx.shape[0]
    k2 = length // n1
    assert e_tab.shape == (n1, 2 * n2h, k2)
    row = pl.BlockSpec((1, DFT_WIDTH), lambda g, j: (0, j))
    out = pl.pallas_call(
        _dft_ainv_kernel,
        grid=(n1 // SUBLANES, c // DFT_WIDTH),
        in_specs=[_wide_spec(n2h), _wide_spec(k2),
                  pl.BlockSpec((SUBLANES, 2 * n2h, k2), lambda g, j: (g, 0, 0)), row, row],
        out_specs=pl.BlockSpec((k2, None, SUBLANES, DFT_WIDTH // 2), lambda g, j: (0, g, 0, j)),
        out_shape=jax.ShapeDtypeStruct((k2, n1 // SUBLANES, SUBLANES, c // 2), U32),
        scratch_shapes=_flat_scratch(n2h) + _flat_scratch(k2),
        compiler_params=_cparams("parallel", "arbitrary"),
        name="dft_stage_a_inv",
    )(cc.reshape(n2h, n1 // SUBLANES, SUBLANES, c), zx.reshape(k2, n1 // SUBLANES, SUBLANES, c),
      e_tab, skip.reshape(1, c), scale)
    return out.reshape(length, c // 2)


def _dft_tables(n1, n2h):
    n2 = 2 * n2h
    n = n1 * n2
    a = jnp.arange(n1, dtype=jnp.int32)
    b = jnp.arange(n2, dtype=jnp.int32)
    odd = 2 * jnp.arange(n2h, dtype=jnp.int32) + 1
    alpha = ((a[:, None] * odd[None, :]) % (2 * n)).astype(F32) * (math.pi / n)
    beta = ((odd[:, None] * b[None, :]) % (2 * n2)).astype(F32) * (math.pi / n2)
    ar, ai = jnp.cos(alpha)[:, :, None], -jnp.sin(alpha)[:, :, None]
    br, bi = jnp.cos(beta)[None], -jnp.sin(beta)[None]
    ar, ai, br, bi = lax.optimization_barrier((ar, ai, br, bi))
    er = ar * br - ai * bi
    ei = ar * bi + ai * br
    e_full = jnp.concatenate([er, ei], axis=1).astype(BF16)
    e_half = e_full[:, :, :n2h]
    phi = ((a[:, None] * a[None, :]) % n1).astype(F32) * (2.0 * math.pi / n1)
    fr, fi = jnp.cos(phi), -jnp.sin(phi)
    f_fwd = jnp.concatenate([jnp.concatenate([fr, -fi], axis=1),
                             jnp.concatenate([fi, fr], axis=1)], axis=0).astype(BF16)
    f_inv = jnp.concatenate([jnp.concatenate([fr, fi], axis=1),
                             jnp.concatenate([-fi, fr], axis=1)], axis=0).astype(BF16)
    return e_half, e_full, f_fwd, f_inv


def _hyena_features(length):
    pos = jnp.arange(length, dtype=jnp.int32).astype(F32)
    t = pos / (length - 1.0)
    w = 2.0 * math.pi * pos[None, :] / length
    f = jnp.linspace(1e-4, HY_BANDS - 1.0, HY_BANDS, dtype=F32)[:, None]
    feat_t = jnp.concatenate([t[None, :], jnp.cos(f * w), -jnp.sin(f * w)], axis=0)
    r = jnp.arange(2 * length, dtype=jnp.int32)
    t_ext = jnp.where(r < length, r, 2 * length - r).astype(F32) / (length - 1.0)
    return t_ext[:, None], feat_t


def _merge_kernel(x_ref, yr_ref, hy_ref, gr_ref, gh_ref, gate_ref, wr_ref, wh_ref, wo_ref, o_ref):
    ret_out = _dot(yr_ref[...], wr_ref[...])
    hi, lo = _unpack2(hy_ref[...])
    half = DFT_WIDTH // 2
    hy = jnp.concatenate([part[:, b:b + half] for b in range(0, hi.shape[1], half) for part in (hi, lo)], axis=1)
    hy_out = _dot(hy.astype(BF16), wh_ref[...])
    m = (jax.nn.sigmoid(gr_ref[...].astype(F32)) * ret_out
         + jax.nn.sigmoid(gh_ref[...].astype(F32)) * hy_out)
    o_ref[...] = x_ref[...] + gate_ref[...] * _dot(m.astype(BF16), wo_ref[...])


def _merge(x, y_ret, hy_in, p, gate_off, gate, w_ret_o, w_hy_o, w_out, tm=256):
    t, d = x.shape
    tm = min(tm, t)
    gb = gate_off // d
    tile = lambda: pl.BlockSpec((tm, d), lambda i: (i, 0))
    weight = lambda w: pl.BlockSpec((None,) + w.shape[1:], lambda i: (0, 0, 0), pipeline_mode=pl.Buffered(1))
    return pl.pallas_call(
        _merge_kernel,
        grid=(t // tm,),
        in_specs=[tile(), tile(), pl.BlockSpec((tm, d // 2), lambda i: (i, 0)),
                  pl.BlockSpec((tm, d), lambda i: (i, gb)),
                  pl.BlockSpec((tm, d), lambda i: (i, gb + 1)),
                  pl.BlockSpec((1, d), lambda i: (0, 0)),
                  weight(w_ret_o), weight(w_hy_o), weight(w_out)],
        out_specs=tile(),
        out_shape=jax.ShapeDtypeStruct((t, d), F32),
        compiler_params=_cparams("parallel"),
        name="merge_out",
    )(x, y_ret, hy_in, p, p, gate.reshape(1, d), w_ret_o, w_hy_o, w_out)


def _mod_rows(mods, first):
    return jnp.pad(mods[first:first + 3], ((0, 5), (0, 0)))


def kernel(x, c, ctx, c_ctx, w_ada, b_ada, norm_g, ffn_up, ffn_down, w_in, ret_log_gamma, ret_gn_g,
           ret_gn_b, w_ret_o, hy_conv_w, hy_conv_b, hy_ff_w1, hy_ff_b1, hy_ff_w2, hy_ff_b2, hy_ff_w3,
           hy_ff_b3, hy_ff_w4, hy_sin_freq, hy_decay, hy_bias, w_hy_o, w_out, final_norm_g):
    batch, t, d = x.shape
    assert batch == 1 and w_ada.shape[0] == 1, "single sample, single layer"
    hy_w = hy_decay.shape[1]
    q_off, k_off = 0, RET_Q
    v_off = k_off + RET_Q
    g_off = v_off + RET_V
    hy_off = g_off + RET_V
    gate_off = hy_off + 3 * hy_w
    assert w_in.shape[2] == gate_off + 2 * d and hy_w == d

    xs, cs = x[0], ctx[0]
    up, down, w_in_b = ffn_up.astype(BF16), ffn_down.astype(BF16), w_in.astype(BF16)
    lg = ret_log_gamma[0].astype(F32)

    cv = jnp.concatenate([c, c_ctx[None, :], jnp.zeros((6, d), F32)], axis=0)
    mods = _adaln(cv, w_ada[0], b_ada[0]).reshape(8, N_MOD, d)
    mx, mc = mods[0], mods[1]

    xs = _ffn_half(xs, _mod_rows(mx, 0), norm_g[0, 0], up, down, 0)
    cs = _ffn_half(cs, _mod_rows(mc, 0), norm_g[0, 0], up, down, 0)

    pkv = _in_proj(cs, _mod_rows(mc, 3), norm_g[0, 1], w_in_b, k_off, g_off - k_off)
    st0 = _ctx_states(pkv, lg)

    cos_t, sin_t = _rope_tables(t)
    p = _in_proj(xs, _mod_rows(mx, 3), norm_g[0, 1], w_in_b, 0, w_in_b.shape[2],
                 rope=(cos_t, sin_t, v_off))

    y_ret = _retention(p, st0, lg, ret_gn_g[0], ret_gn_b[0], q_off, k_off, v_off, g_off)

    zx = _hy_pre(p, hy_conv_w[0], hy_conv_b[0], hy_off, hy_w)
    t_ext, feat_t = _hyena_features(t)
    e_pad = (-feat_t.shape[0]) % 8
    feat_t = jnp.pad(feat_t, ((0, e_pad), (0, 0)))
    w1 = jnp.pad(hy_ff_w1[0], ((0, e_pad), (0, 0)))
    h_pos = _filter_ffn(feat_t, w1, hy_ff_b1[0], hy_ff_w2[0], hy_ff_b2[0], hy_ff_w3[0], hy_ff_b3[0],
                        hy_sin_freq[0])
    h_taps, ssq = _filter_taps(h_pos, t_ext, hy_ff_w4[0].astype(BF16), hy_decay[0])

    n1 = DFT_N1
    e_half, e_full, f_fwd, f_inv = _dft_tables(n1, t // n1)
    bz = _dft_stage_a(zx, e_half, both_halves=False)
    bh = _dft_stage_a(h_taps, e_full, both_halves=True)
    cc = _dft_mid(bz, bh, f_fwd, f_inv)
    scale = lax.rsqrt(ssq + EPS) * (2.0 / (2 * t))
    hy_in = _dft_stage_a_inv(cc, e_half, zx, hy_bias[0], scale)

    xs = _merge(xs, y_ret, hy_in, p, gate_off, mx[5], w_ret_o.astype(BF16), w_hy_o.astype(BF16),
                w_out.astype(BF16))

    out = _ffn_half(xs, _mod_rows(mx, 6), norm_g[0, 2], up, down, 1, final_g=final_norm_g)
    return out[None]
```

```python
import functools
import math

import jax
import jax.numpy as jnp
from jax import lax
from jax.experimental import pallas as pl
from jax.experimental.pallas import tpu as pltpu

F32 = jnp.float32
BF16 = jnp.bfloat16
U32 = jnp.uint32

N_MOD = 9
GRID_W = 64
RET_HEADS = 8
RET_DK = 128
RET_DV = 256
RET_CHUNK = 128
RET_Q = RET_HEADS * RET_DK
RET_V = RET_HEADS * RET_DV
RET_PAIR = 2
HY_BANDS = 16
ROPE_BASE = 10000.0
EPS = 1e-6

TAPS_COLS = 256
DFT_N1 = 128
SUBLANES = 8
LANES = 128
DFT_SLABS = 4
DFT_WIDTH = 1024
VMEM_BYTES = 64 * 1024 * 1024
VMEM_LIMIT = VMEM_BYTES - 8 * 1024 * 1024


def _cparams(*sem, vmem=VMEM_LIMIT):
    return pltpu.CompilerParams(dimension_semantics=sem, vmem_limit_bytes=vmem)


def _dot(a, b):
    return jnp.dot(a, b, preferred_element_type=F32)


def _dot_t0(a, b):
    return lax.dot_general(a, b, (((0,), (0,)), ((), ())), preferred_element_type=F32)


def _silu(v):
    return v * jax.nn.sigmoid(v)


def _norm_mod(x, g, shift, scale):
    return x * lax.rsqrt(jnp.mean(x * x, axis=-1, keepdims=True) + EPS) * (g * (1.0 + scale)) + shift


def _pack2(hi, lo):
    hb = lax.bitcast_convert_type(hi.astype(BF16).astype(F32), U32)
    lb = lax.bitcast_convert_type(lo.astype(BF16).astype(F32), U32)
    return hb | (lb >> 16)


def _unpack2(w):
    hi = lax.bitcast_convert_type(w & jnp.uint32(0xFFFF0000), F32)
    lo = lax.bitcast_convert_type(w << 16, F32)
    return hi, lo


def _adaln_kernel(c_ref, w_ref, b_ref, o_ref):
    s = _silu(c_ref[...]).astype(BF16)
    o_ref[...] = _dot(s, w_ref[...].astype(BF16)) + b_ref[...]


def _adaln(cv, w, b, tn=1024):
    rows, d = cv.shape
    n = w.shape[1]
    return pl.pallas_call(
        _adaln_kernel,
        grid=(n // tn,),
        in_specs=[pl.BlockSpec((rows, d), lambda j: (0, 0)),
                  pl.BlockSpec((d, tn), lambda j: (0, j)),
                  pl.BlockSpec((1, tn), lambda j: (0, j))],
        out_specs=pl.BlockSpec((rows, tn), lambda j: (0, j)),
        out_shape=jax.ShapeDtypeStruct((rows, n), F32),
        compiler_params=_cparams("arbitrary"),
        name="adaln",
    )(cv, w, b.reshape(1, n))


def _ffn_kernel(x_ref, mod_ref, g_ref, wa_ref, wg_ref, wd_ref, *rest, final):
    if final:
        fg_ref, o_ref, h_scr, acc_scr = rest
    else:
        o_ref, h_scr, acc_scr = rest
    j = pl.program_id(1)

    @pl.when(j == 0)
    def _():
        h = _norm_mod(x_ref[...], g_ref[...], mod_ref[0:1, :], mod_ref[1:2, :])
        h_scr[...] = h.astype(BF16)
        acc_scr[...] = jnp.zeros_like(acc_scr)

    h = h_scr[...]
    a = _dot(h, wa_ref[...])
    g = _dot(h, wg_ref[...])
    acc_scr[...] += _dot((_silu(a) * g).astype(BF16), wd_ref[...])

    @pl.when(j == pl.num_programs(1) - 1)
    def _():
        out = x_ref[...] + (0.5 * mod_ref[2:3, :]) * acc_scr[...]
        if final:
            out = out * lax.rsqrt(jnp.mean(out * out, axis=-1, keepdims=True) + EPS) * fg_ref[...]
        o_ref[...] = out


def _ffn_half(x, mod, g, w_up, w_down, layer, final_g=None, tm=512, tf=512):
    t, d = x.shape
    dff = w_down.shape[2]
    tm = min(tm, t)
    nf = dff // tf
    final = final_g is not None
    in_specs = [pl.BlockSpec((tm, d), lambda i, j: (i, 0)),
                pl.BlockSpec((8, d), lambda i, j: (0, 0)),
                pl.BlockSpec((1, d), lambda i, j: (0, 0)),
                pl.BlockSpec((None, None, d, tf), lambda i, j: (0, layer, 0, j)),
                pl.BlockSpec((None, None, d, tf), lambda i, j: (0, layer, 0, nf + j)),
                pl.BlockSpec((None, None, tf, d), lambda i, j: (0, layer, j, 0))]
    args = [x, mod, g.reshape(1, d), w_up, w_up, w_down]
    if final:
        in_specs.append(pl.BlockSpec((1, d), lambda i, j: (0, 0)))
        args.append(final_g.reshape(1, d))
    return pl.pallas_call(
        functools.partial(_ffn_kernel, final=final),
        grid=(t // tm, nf),
        in_specs=in_specs,
        out_specs=pl.BlockSpec((tm, d), lambda i, j: (i, 0)),
        out_shape=jax.ShapeDtypeStruct((t, d), F32),
        scratch_shapes=[pltpu.VMEM((tm, d), BF16), pltpu.VMEM((tm, d), F32)],
        compiler_params=_cparams("parallel", "arbitrary"),
        name="ffn_final" if final else "ffn_half",
    )(*args)


def _rope(x, cos, sin):
    lane = lax.broadcasted_iota(jnp.int32, x.shape, 1)
    partner = jnp.where((lane % 64) < 32, pltpu.roll(x, 96, 1), pltpu.roll(x, 32, 1))
    return x * cos + partner * sin


def _inproj_kernel(x_ref, mod_ref, g_ref, w_ref, *rest, rope_tiles):
    if rope_tiles:
        cos_ref, sin_ref, o_ref, h_scr = rest
    else:
        o_ref, h_scr = rest
    j = pl.program_id(1)

    @pl.when(j == 0)
    def _():
        h = _norm_mod(x_ref[...], g_ref[...], mod_ref[0:1, :], mod_ref[1:2, :])
        h_scr[...] = h.astype(BF16)

    if rope_tiles:
        @pl.when(j < rope_tiles)
        def _():
            acc = _dot(h_scr[...], w_ref[...])
            scale = jnp.where(j == 0, RET_DK ** -0.5, 1.0)
            cos = cos_ref[...] * scale
            sin = sin_ref[...] * scale
            for hb in range(acc.shape[1] // RET_DK):
                cols = slice(hb * RET_DK, (hb + 1) * RET_DK)
                o_ref[:, cols] = _rope(acc[:, cols], cos, sin).astype(BF16)

        @pl.when(j >= rope_tiles)
        def _():
            o_ref[...] = _dot(h_scr[...], w_ref[...]).astype(BF16)
    else:
        o_ref[...] = _dot(h_scr[...], w_ref[...]).astype(BF16)


def _in_proj(x, mod, g, w, col0, ncols, rope=None, tm=1024, tn=1024):
    t, d = x.shape
    tm = min(tm, t)
    off = col0 // tn
    in_specs = [pl.BlockSpec((tm, d), lambda i, j: (i, 0)),
                pl.BlockSpec((8, d), lambda i, j: (0, 0)),
                pl.BlockSpec((1, d), lambda i, j: (0, 0)),
                pl.BlockSpec((None, d, tn), lambda i, j: (0, 0, j + off))]
    args = [x, mod, g.reshape(1, d), w]
    rope_tiles = 0
    if rope is not None:
        cos_t, sin_t, qk_cols = rope
        assert tn == RET_Q and qk_cols == 2 * RET_Q and col0 == 0
        rope_tiles = qk_cols // tn
        in_specs += [pl.BlockSpec((tm, RET_DK), lambda i, j: (i, 0))] * 2
        args += [cos_t, sin_t]
    return pl.pallas_call(
        functools.partial(_inproj_kernel, rope_tiles=rope_tiles),
        grid=(t // tm, ncols // tn),
        in_specs=in_specs,
        out_specs=pl.BlockSpec((tm, tn), lambda i, j: (i, j)),
        out_shape=jax.ShapeDtypeStruct((t, ncols), BF16),
        scratch_shapes=[pltpu.VMEM((tm, d), BF16)],
        compiler_params=_cparams("parallel", "arbitrary"),
        name="in_proj",
    )(*args)


def _rope_tables(t):
    half = RET_DK // 2
    n_rows = t // GRID_W
    inv = ROPE_BASE ** (-jnp.arange(0, half, 2, dtype=F32) / half)
    ang_r = jnp.arange(n_rows, dtype=jnp.int32).astype(F32)[:, None] * inv[None, :]
    ang_c = jnp.arange(GRID_W, dtype=jnp.int32).astype(F32)[:, None] * inv[None, :]
    by_row = lambda v: jnp.repeat(v, GRID_W, axis=0)
    by_col = lambda v: jnp.tile(v, (n_rows, 1))
    cos_r, sin_r, cos_c, sin_c = jnp.cos(ang_r), jnp.sin(ang_r), jnp.cos(ang_c), jnp.sin(ang_c)
    cos_t = jnp.concatenate([by_row(cos_r)] * 2 + [by_col(cos_c)] * 2, axis=-1)
    sin_t = jnp.concatenate([by_row(-sin_r), by_row(sin_r), by_col(-sin_c), by_col(sin_c)], axis=-1)
    return cos_t, sin_t


def _ctx_state_kernel(lg_ref, k_ref, v_ref, o_ref):
    h = pl.program_id(0)
    lc = k_ref.shape[0]
    m = lax.broadcasted_iota(jnp.int32, (lc, 1), 0).astype(F32)
    k = k_ref[...].astype(F32)
    v = v_ref[...]
    w_f = jnp.exp(lg_ref[0, h] * (lc - 1.0 - m))
    w_b = jnp.exp(lg_ref[1, h] * m)
    o_ref[0, 0] = _dot_t0((k * w_f).astype(BF16), v)
    o_ref[1, 0] = _dot_t0((k * w_b).astype(BF16), v)


def _ctx_states(pkv, lg):
    lc = pkv.shape[0]
    return pl.pallas_call(
        _ctx_state_kernel,
        grid=(RET_HEADS,),
        in_specs=[pl.BlockSpec(memory_space=pltpu.SMEM),
                  pl.BlockSpec((lc, RET_DK), lambda h: (0, h)),
                  pl.BlockSpec((lc, RET_DV), lambda h: (0, RET_Q // RET_DV + h))],
        out_specs=pl.BlockSpec((2, 1, RET_DK, RET_DV), lambda h: (0, h, 0, 0)),
        out_shape=jax.ShapeDtypeStruct((2, RET_HEADS, RET_DK, RET_DV), F32),
        compiler_params=_cparams("arbitrary"),
        name="ctx_states",
    )(lg, pkv, pkv)


def _ret_kernel(lg_ref, q_ref, k_ref, v_ref, g_ref, st0_ref, gng_ref, gnb_ref,
                o_ref, sb_all, sf_scr, sb_scr, *, nb, bc):
    c_len = RET_CHUNK
    hp = pl.program_id(0)
    i = pl.program_id(1)
    pair = range(RET_PAIR)
    lgf = [lg_ref[0, hp * RET_PAIR + e] for e in pair]
    lgb = [lg_ref[1, hp * RET_PAIR + e] for e in pair]
    kcols = [slice(e * RET_DK, (e + 1) * RET_DK) for e in pair]
    vcols = [slice(e * RET_DV, (e + 1) * RET_DV) for e in pair]
    pos = lax.broadcasted_iota(jnp.int32, (c_len, 1), 0).astype(F32)
    ones_row = jnp.ones((1, RET_DV), F32)

    @pl.when(i == 0)
    def _():
        for e in pair:
            sf_scr[e] = st0_ref[0, e]
            sb_scr[e] = st0_ref[1, e]

    @pl.when(i < nb)
    def _():
        rb = nb - 1 - i
        zeta_b = [jnp.exp(lgb[e] * pos) for e in pair]
        cd_b = [jnp.exp((lgb[e] * c_len) * ones_row) for e in pair]
        s_b = [sb_scr[e] for e in pair]
        for c in reversed(range(bc)):
            rows = slice(c * c_len, (c + 1) * c_len)
            for e in pair:
                sb_all[e, rb * bc + c] = s_b[e].astype(BF16)
                kz = (k_ref[rows, kcols[e]].astype(F32) * zeta_b[e]).astype(BF16)
                s_b[e] = s_b[e] * cd_b[e] + _dot_t0(kz, v_ref[rows, vcols[e]])
        for e in pair:
            sb_scr[e] = s_b[e]

    @pl.when(i >= nb)
    def _():
        rb = i - nb
        a = lax.broadcasted_iota(jnp.int32, (c_len, c_len), 0)
        b = lax.broadcasted_iota(jnp.int32, (c_len, c_len), 1)
        d = (a - b).astype(F32)
        decay = [jnp.where(d > 0, jnp.exp(lgf[e] * jnp.maximum(d, 0.0)),
                           jnp.where(d < 0, jnp.exp(lgb[e] * jnp.maximum(-d, 0.0)), 2.0)) for e in pair]
        xi_f = [jnp.exp(lgf[e] * (pos + 1.0)) for e in pair]
        xi_b = [jnp.exp(lgb[e] * (c_len - pos)) for e in pair]
        zeta_f = [jnp.exp(lgf[e] * (c_len - 1.0 - pos)) for e in pair]
        cd_f = [jnp.exp((lgf[e] * c_len) * ones_row) for e in pair]
        s_f = [sf_scr[e] for e in pair]
        for c in range(bc):
            rows = slice(c * c_len, (c + 1) * c_len)
            for e in pair:
                q = q_ref[rows, kcols[e]]
                k = k_ref[rows, kcols[e]]
                v = v_ref[rows, vcols[e]]
                s = lax.dot_general(q, k, (((1,), (1,)), ((), ())), preferred_element_type=F32) * decay[e]
                qf = q.astype(F32)
                lhs = jnp.concatenate([s.astype(BF16), (qf * xi_f[e]).astype(BF16), (qf * xi_b[e]).astype(BF16)],
                                      axis=1)
                rhs = jnp.concatenate([v, s_f[e].astype(BF16), sb_all[e, rb * bc + c]], axis=0)
                y = _dot(lhs, rhs)
                kz = (k.astype(F32) * zeta_f[e]).astype(BF16)
                s_f[e] = s_f[e] * cd_f[e] + _dot_t0(kz, v)
                mu = jnp.mean(y, axis=-1, keepdims=True)
                yc = y - mu
                var = jnp.mean(yc * yc, axis=-1, keepdims=True)
                yn = yc * lax.rsqrt(var + EPS) * gng_ref[:, vcols[e]] + gnb_ref[:, vcols[e]]
                o_ref[rows, vcols[e]] = (_silu(g_ref[rows, vcols[e]].astype(F32)) * yn).astype(BF16)
        for e in pair:
            sf_scr[e] = s_f[e]


def _retention(p, st0, lg, gn_g, gn_b, q_off, k_off, v_off, g_off, bc=16):
    t = p.shape[0]
    bc = min(bc, t // RET_CHUNK)
    rows = bc * RET_CHUNK
    nb = t // rows
    kw, vw = RET_PAIR * RET_DK, RET_PAIR * RET_DV

    def kv_rb(i):
        return jnp.where(i < nb, nb - 1 - i, i - nb)

    def q_rb(i):
        return jnp.maximum(i - nb, 0)

    return pl.pallas_call(
        functools.partial(_ret_kernel, nb=nb, bc=bc),
        grid=(RET_HEADS // RET_PAIR, 2 * nb),
        in_specs=[pl.BlockSpec(memory_space=pltpu.SMEM),
                  pl.BlockSpec((rows, kw), lambda h, i: (q_rb(i), q_off // kw + h)),
                  pl.BlockSpec((rows, kw), lambda h, i: (kv_rb(i), k_off // kw + h)),
                  pl.BlockSpec((rows, vw), lambda h, i: (kv_rb(i), v_off // vw + h)),
                  pl.BlockSpec((rows, vw), lambda h, i: (q_rb(i), g_off // vw + h)),
                  pl.BlockSpec((2, RET_PAIR, RET_DK, RET_DV), lambda h, i: (0, h, 0, 0)),
                  pl.BlockSpec((1, vw), lambda h, i: (0, h)),
                  pl.BlockSpec((1, vw), lambda h, i: (0, h))],
        out_specs=pl.BlockSpec((rows, vw), lambda h, i: (q_rb(i), h)),
        out_shape=jax.ShapeDtypeStruct((t, RET_V), BF16),
        scratch_shapes=[pltpu.VMEM((RET_PAIR, t // RET_CHUNK, RET_DK, RET_DV), BF16),
                        pltpu.VMEM((RET_PAIR, RET_DK, RET_DV), F32),
                        pltpu.VMEM((RET_PAIR, RET_DK, RET_DV), F32)],
        compiler_params=_cparams("arbitrary", "arbitrary"),
        name="retention",
    )(lg, p, p, p, p, st0, gn_g.reshape(1, RET_V), gn_b.reshape(1, RET_V))


def _hy_pre_kernel(m0, m1, m2, p0, p1, p2, n0, n1, n2, w_ref, b_ref, zx_ref):
    i = pl.program_id(0)
    tm, ch = m0.shape
    row = lax.broadcasted_iota(jnp.int32, (tm, 1), 0)
    has_prev = (i > 0).astype(F32)
    has_next = (i < pl.num_programs(0) - 1).astype(F32)
    halo = p0.shape[0]
    r_idx = lax.broadcasted_iota(jnp.int32, (tm, tm), 0)
    c_idx = lax.broadcasted_iota(jnp.int32, (tm, tm), 1)
    down = jnp.where(r_idx == c_idx + 1, 1.0, 0.0).astype(BF16)
    up = jnp.where(r_idx + 1 == c_idx, 1.0, 0.0).astype(BF16)

    def conv(main, prev, nxt, part):
        cols = slice(part * ch, (part + 1) * ch)
        ub = main[...]
        before = jnp.where(row == 0, prev[halo - 1:halo, :].astype(F32) * has_prev, _dot(down, ub))
        after = jnp.where(row == tm - 1, nxt[0:1, :].astype(F32) * has_next, _dot(up, ub))
        return (before * w_ref[0:1, cols] + ub.astype(F32) * w_ref[1:2, cols] + after * w_ref[2:3, cols]
                + b_ref[0:1, cols])

    x0 = conv(m0, p0, n0, 0)
    z = conv(m1, p1, n1, 1) * conv(m2, p2, n2, 2)
    zx_ref[...] = _pack2(z, x0)


def _hy_pre(p, conv_w, conv_b, hy_off, ch, tm=256, halo=16):
    t = p.shape[0]
    tm = min(tm, t)
    nh = tm // halo
    last = t // halo - 1
    cb0 = hy_off // ch

    def main(part):
        return pl.BlockSpec((tm, ch), lambda i: (i, cb0 + part))

    def prev(part):
        return pl.BlockSpec((halo, ch), lambda i: (jnp.maximum(i * nh - 1, 0), cb0 + part))

    def nxt(part):
        return pl.BlockSpec((halo, ch), lambda i: (jnp.minimum((i + 1) * nh, last), cb0 + part))

    return pl.pallas_call(
        _hy_pre_kernel,
        grid=(t // tm,),
        in_specs=[main(0), main(1), main(2), prev(0), prev(1), prev(2), nxt(0), nxt(1), nxt(2),
                  pl.BlockSpec((3, 3 * ch), lambda i: (0, 0)),
                  pl.BlockSpec((1, 3 * ch), lambda i: (0, 0))],
        out_specs=pl.BlockSpec((tm, ch), lambda i: (i, 0)),
        out_shape=jax.ShapeDtypeStruct((t, ch), U32),
        compiler_params=_cparams("parallel"),
        name="hy_conv3",
    )(p, p, p, p, p, p, p, p, p, conv_w, conv_b.reshape(1, 3 * ch))


def _filt_ffn_kernel(f_ref, w1, b1, w2, b2, w3, b3, fr, o_ref):
    hp = lax.Precision.HIGHEST
    freq = fr[:, 0:1]
    h = jnp.sin(freq * (jnp.dot(w1[...], f_ref[...], precision=hp, preferred_element_type=F32) + b1[:, 0:1]))
    h = jnp.sin(freq * (jnp.dot(w2[...], h, precision=hp, preferred_element_type=F32) + b2[:, 0:1]))
    o_ref[...] = jnp.sin(freq * (jnp.dot(w3[...], h, precision=hp, preferred_element_type=F32) + b3[:, 0:1]))


def _filter_ffn(feat_t, w1, b1, w2, b2, w3, b3, freq, tl=2048):
    e, length = feat_t.shape
    ff = w1.shape[1]
    tl = min(tl, length)

    def col(v):
        return jnp.broadcast_to(v.reshape(ff, 1), (ff, 128))

    full = lambda shape: pl.BlockSpec(shape, lambda i: (0, 0))
    return pl.pallas_call(
        _filt_ffn_kernel,
        grid=(length // tl,),
        in_specs=[pl.BlockSpec((e, tl), lambda i: (0, i)),
                  full((ff, e)), full((ff, 128)), full((ff, ff)), full((ff, 128)),
                  full((ff, ff)), full((ff, 128)), full((ff, 128))],
        out_specs=pl.BlockSpec((ff, tl), lambda i: (0, i)),
        out_shape=jax.ShapeDtypeStruct((ff, length), F32),
        compiler_params=_cparams("parallel"),
        name="filter_ffn",
    )(feat_t, w1.T, col(b1), w2.T, col(b2), w3.T, col(b3), col(freq))


def _filt_taps_kernel(hf_ref, hb_ref, hn_ref, tf_ref, tb_ref, wf_ref, wb_ref, dec_ref, hp_ref, ssq_ref):
    i = pl.program_id(0)
    tr = hf_ref.shape[1]
    hf = hf_ref[...].astype(BF16)
    src = lax.broadcasted_iota(jnp.int32, (tr, tr), 0)
    dst = lax.broadcasted_iota(jnp.int32, (tr, tr), 1)
    rev_shift = jnp.where(src + dst == tr, 1.0, 0.0).astype(BF16)
    hb = _dot(hb_ref[...].astype(BF16), rev_shift)
    first = hn_ref[:, 0:1].astype(BF16).astype(F32)
    hb = jnp.where(lax.broadcasted_iota(jnp.int32, (1, tr), 1) == 0, first, hb).astype(BF16)
    t_f = tf_ref[:, 0:1]
    t_b = tb_ref[:, 0:1]
    row = lax.broadcasted_iota(jnp.int32, t_b.shape, 0)
    sign_b = jnp.where((row == 0) & (i == 0), 0.0, -1.0)

    @pl.when(i == 0)
    def _():
        ssq_ref[...] = jnp.zeros_like(ssq_ref)

    width = TAPS_COLS
    for c0 in range(0, dec_ref.shape[1], width):
        cols = slice(c0, c0 + width)
        dec = jnp.abs(dec_ref[:, cols])
        fwd = _dot_t0(hf, wf_ref[:, cols]) * jnp.exp(-t_f * dec)
        bwd = _dot_t0(hb, wb_ref[:, cols]) * (jnp.exp(-t_b * dec) * sign_b)
        ssq_ref[:, cols] += jnp.sum(fwd * fwd + bwd * bwd, axis=0, keepdims=True)
        hp_ref[:, cols] = _pack2(fwd, bwd)


def _filter_taps(h_pos, t_ext, w4, decay, tr=512):
    ff, length = h_pos.shape
    ch = decay.shape[0]
    tr = min(tr, length)
    nt = length // tr
    return pl.pallas_call(
        _filt_taps_kernel,
        grid=(nt,),
        in_specs=[pl.BlockSpec((ff, tr), lambda i: (0, i)),
                  pl.BlockSpec((ff, tr), lambda i: (0, nt - 1 - i)),
                  pl.BlockSpec((ff, tr), lambda i: (0, jnp.minimum(nt - i, nt - 1))),
                  pl.BlockSpec((tr, 1), lambda i: (i, 0)),
                  pl.BlockSpec((tr, 1), lambda i: (i + nt, 0)),
                  pl.BlockSpec((ff, ch), lambda i: (0, 0)),
                  pl.BlockSpec((ff, ch), lambda i: (0, 1)),
                  pl.BlockSpec((1, ch), lambda i: (0, 0))],
        out_specs=[pl.BlockSpec((tr, ch), lambda i: (i, 0)),
                   pl.BlockSpec((1, ch), lambda i: (0, 0))],
        out_shape=[jax.ShapeDtypeStruct((length, ch), U32), jax.ShapeDtypeStruct((1, ch), F32)],
        compiler_params=_cparams("arbitrary"),
        name="filter_taps",
    )(h_pos, h_pos, h_pos, t_ext, t_ext, w4, w4, decay.reshape(1, ch))


def _wide_spec(rows):
    return pl.BlockSpec((rows, None, SUBLANES, DFT_WIDTH), lambda g, j: (0, g, 0, j))


def _flat_scratch(rows, arrays=1):
    return [pltpu.VMEM((rows * SUBLANES, LANES), U32)] * (arrays * DFT_WIDTH // LANES)


def _flatten(block, flats):
    for s, f in enumerate(flats):
        f[...] = block[:, :, s * LANES:(s + 1) * LANES].reshape(f.shape)


def _gather_rows(flats, a):
    return jnp.concatenate([f[pl.ds(a, f.shape[0] // SUBLANES, stride=SUBLANES), :] for f in flats], axis=1)


def _lane_groups(flats, per=DFT_SLABS):
    return [(slice(i * per * LANES, (i + 1) * per * LANES), flats[i * per:(i + 1) * per])
            for i in range(len(flats) // per)]


def _dft_a_kernel(x_ref, e_ref, o_ref, *flats, both_halves):
    _flatten(x_ref, flats)
    for lanes, fl in _lane_groups(flats):
        for a in range(SUBLANES):
            hi, lo = _unpack2(_gather_rows(fl, a))
            r = _dot(e_ref[a], hi.astype(BF16))
            n = r.shape[0] // 2
            re, im = r[:n], r[n:]
            if both_halves:
                r_lo = _dot(e_ref[a], lo.astype(BF16))
                k2 = lax.broadcasted_iota(jnp.int32, (n, 1), 0)
                sign = jnp.where(k2 % 2 == 0, 1.0, -1.0)
                re, im = re + sign * r_lo[n:], im - sign * r_lo[:n]
            o_ref[a, :, lanes] = _pack2(re, im)


def _dft_stage_a(x, e_tab, both_halves):
    n1, m2, kdim = e_tab.shape
    length, c = x.shape
    k2 = length // n1
    assert kdim == k2
    return pl.pallas_call(
        functools.partial(_dft_a_kernel, both_halves=both_halves),
        grid=(n1 // SUBLANES, c // DFT_WIDTH),
        in_specs=[_wide_spec(k2), pl.BlockSpec((SUBLANES, m2, kdim), lambda g, j: (g, 0, 0))],
        out_specs=pl.BlockSpec((SUBLANES, m2 // 2, DFT_WIDTH), lambda g, j: (g, 0, j)),
        out_shape=jax.ShapeDtypeStruct((n1, m2 // 2, c), U32),
        scratch_shapes=_flat_scratch(k2),
        compiler_params=_cparams("parallel", "arbitrary"),
        name="dft_stage_a",
    )(x.reshape(k2, n1 // SUBLANES, SUBLANES, c), e_tab)


def _dft_mid_kernel(z_ref, h_ref, f_ref, finv_ref, o_ref, *flats):
    half = len(flats) // 2
    _flatten(z_ref, flats[:half])
    _flatten(h_ref, flats[half:])
    n = z_ref.shape[0]
    for (lanes, fz), (_, fh) in zip(_lane_groups(flats[:half], half), _lane_groups(flats[half:], half)):
        for a in range(SUBLANES):
            zr, zi = _unpack2(_gather_rows(fz, a))
            hr, hi = _unpack2(_gather_rows(fh, a))
            xs = _dot(f_ref[...], jnp.concatenate([zr, zi], axis=0).astype(BF16))
            ks = _dot(f_ref[...], jnp.concatenate([hr, hi], axis=0).astype(BF16))
            xr, xi, kr, ki = xs[:n], xs[n:], ks[:n], ks[n:]
            y = jnp.concatenate([xr * kr - xi * ki, xr * ki + xi * kr], axis=0).astype(BF16)
            cs = _dot(finv_ref[...], y)
            o_ref[a, :, lanes] = _pack2(cs[:n], cs[n:])


def _dft_mid(bz, bh, f_fwd, f_inv):
    n1, n2h, c = bz.shape
    mat = pl.BlockSpec((2 * n1, 2 * n1), lambda g, j: (0, 0))
    view = lambda b: b.reshape(n1, n2h // SUBLANES, SUBLANES, c)
    return pl.pallas_call(
        _dft_mid_kernel,
        grid=(n2h // SUBLANES, c // DFT_WIDTH),
        in_specs=[_wide_spec(n1), _wide_spec(n1), mat, mat],
        out_specs=pl.BlockSpec((SUBLANES, n1, DFT_WIDTH), lambda g, j: (g, 0, j)),
        out_shape=jax.ShapeDtypeStruct((n2h, n1, c), U32),
        scratch_shapes=_flat_scratch(n1, arrays=2),
        compiler_params=_cparams("parallel", "arbitrary"),
        name="dft_mid",
    )(view(bz), view(bh), f_fwd, f_inv)


def _dft_ainv_kernel(c_ref, zx_ref, e_ref, skip_ref, sc_ref, o_ref, *flats):
    half = len(flats) // 2
    _flatten(c_ref, flats[:half])
    _flatten(zx_ref, flats[half:])
    fc, fzx = flats[:half], flats[half:]
    mid = DFT_WIDTH // 2
    for a in range(SUBLANES):
        c_r, c_i = _unpack2(_gather_rows(fc, a))
        y = _dot_t0(e_ref[a], jnp.concatenate([c_r, c_i], axis=0).astype(BF16)) * sc_ref[...]
        z, x0 = _unpack2(_gather_rows(fzx, a))
        val = x0 * (y + z * skip_ref[...])
        o_ref[:, a, :] = _pack2(val[:, :mid], val[:, mid:])


def _dft_stage_a_inv(cc, e_tab, zx, skip, scale):
    n2h, n1, c = cc.shape
    length = zx.shape[0]
    k2 = length // n1
    assert e_tab.shape == (n1, 2 * n2h, k2)
    row = pl.BlockSpec((1, DFT_WIDTH), lambda g, j: (0, j))
    out = pl.pallas_call(
        _dft_ainv_kernel,
        grid=(n1 // SUBLANES, c // DFT_WIDTH),
        in_specs=[_wide_spec(n2h), _wide_spec(k2),
                  pl.BlockSpec((SUBLANES, 2 * n2h, k2), lambda g, j: (g, 0, 0)), row, row],
        out_specs=pl.BlockSpec((k2, None, SUBLANES, DFT_WIDTH // 2), lambda g, j: (0, g, 0, j)),
        out_shape=jax.ShapeDtypeStruct((k2, n1 // SUBLANES, SUBLANES, c // 2), U32),
        scratch_shapes=_flat_scratch(n2h) + _flat_scratch(k2),
        compiler_params=_cparams("parallel", "arbitrary"),
        name="dft_stage_a_inv",
    )(cc.reshape(n2h, n1 // SUBLANES, SUBLANES, c), zx.reshape(k2, n1 // SUBLANES, SUBLANES, c),
      e_tab, skip.reshape(1, c), scale)
    return out.reshape(length, c // 2)


def _dft_tables(n1, n2h):
    n2 = 2 * n2h
    n = n1 * n2
    a = jnp.arange(n1, dtype=jnp.int32)
    b = jnp.arange(n2h, dtype=jnp.int32)
    odd = 2 * jnp.arange(n2h, dtype=jnp.int32) + 1
    alpha = ((a[:, None] * odd[None, :]) % (2 * n)).astype(F32) * (math.pi / n)
    beta = ((odd[:, None] * b[None, :]) % (2 * n2)).astype(F32) * (math.pi / n2)
    ar, ai = jnp.cos(alpha)[:, :, None], -jnp.sin(alpha)[:, :, None]
    br, bi = jnp.cos(beta)[None], -jnp.sin(beta)[None]
    ar, ai, br, bi = lax.optimization_barrier((ar, ai, br, bi))
    er = ar * br - ai * bi
    ei = ar * bi + ai * br
    e_half = jnp.concatenate([er, ei], axis=1).astype(BF16)
    phi = ((a[:, None] * a[None, :]) % n1).astype(F32) * (2.0 * math.pi / n1)
    fr, fi = jnp.cos(phi), -jnp.sin(phi)
    f_fwd = jnp.concatenate([jnp.concatenate([fr, -fi], axis=1),
                             jnp.concatenate([fi, fr], axis=1)], axis=0).astype(BF16)
    f_inv = jnp.concatenate([jnp.concatenate([fr, fi], axis=1),
                             jnp.concatenate([-fi, fr], axis=1)], axis=0).astype(BF16)
    return e_half, f_fwd, f_inv


def _hyena_features(length):
    pos = jnp.arange(length, dtype=jnp.int32).astype(F32)
    t = pos / (length - 1.0)
    w = 2.0 * math.pi * pos[None, :] / length
    f = jnp.linspace(1e-4, HY_BANDS - 1.0, HY_BANDS, dtype=F32)[:, None]
    feat_t = jnp.concatenate([t[None, :], jnp.cos(f * w), -jnp.sin(f * w)], axis=0)
    r = jnp.arange(2 * length, dtype=jnp.int32)
    t_ext = jnp.where(r < length, r, 2 * length - r).astype(F32) / (length - 1.0)
    return t_ext[:, None], feat_t


def _merge_kernel(x_ref, yr_ref, hy_ref, gr_ref, gh_ref, gate_ref, wr_ref, wh_ref, wo_ref, o_ref):
    ret_out = _dot(yr_ref[...], wr_ref[...])
    hi, lo = _unpack2(hy_ref[...])
    half = DFT_WIDTH // 2
    hy = jnp.concatenate([part[:, b:b + half] for b in range(0, hi.shape[1], half) for part in (hi, lo)], axis=1)
    hy_out = _dot(hy.astype(BF16), wh_ref[...])
    m = (jax.nn.sigmoid(gr_ref[...].astype(F32)) * ret_out
         + jax.nn.sigmoid(gh_ref[...].astype(F32)) * hy_out)
    o_ref[...] = x_ref[...] + gate_ref[...] * _dot(m.astype(BF16), wo_ref[...])


def _merge(x, y_ret, hy_in, p, gate_off, gate, w_ret_o, w_hy_o, w_out, tm=256):
    t, d = x.shape
    tm = min(tm, t)
    gb = gate_off // d
    tile = lambda: pl.BlockSpec((tm, d), lambda i: (i, 0))
    weight = lambda w: pl.BlockSpec((None,) + w.shape[1:], lambda i: (0, 0, 0), pipeline_mode=pl.Buffered(1))
    return pl.pallas_call(
        _merge_kernel,
        grid=(t // tm,),
        in_specs=[tile(), tile(), pl.BlockSpec((tm, d // 2), lambda i: (i, 0)),
                  pl.BlockSpec((tm, d), lambda i: (i, gb)),
                  pl.BlockSpec((tm, d), lambda i: (i, gb + 1)),
                  pl.BlockSpec((1, d), lambda i: (0, 0)),
                  weight(w_ret_o), weight(w_hy_o), weight(w_out)],
        out_specs=tile(),
        out_shape=jax.ShapeDtypeStruct((t, d), F32),
        compiler_params=_cparams("parallel"),
        name="merge_out",
    )(x, y_ret, hy_in, p, p, gate.reshape(1, d), w_ret_o, w_hy_o, w_out)


def _mod_rows(mods, first):
    return jnp.pad(mods[first:first + 3], ((0, 5), (0, 0)))


def kernel(x, c, ctx, c_ctx, w_ada, b_ada, norm_g, ffn_up, ffn_down, w_in, ret_log_gamma, ret_gn_g,
           ret_gn_b, w_ret_o, hy_conv_w, hy_conv_b, hy_ff_w1, hy_ff_b1, hy_ff_w2, hy_ff_b2, hy_ff_w3,
           hy_ff_b3, hy_ff_w4, hy_sin_freq, hy_decay, hy_bias, w_hy_o, w_out, final_norm_g):
    batch, t, d = x.shape
    assert batch == 1 and w_ada.shape[0] == 1, "single sample, single layer"
    hy_w = hy_decay.shape[1]
    q_off, k_off = 0, RET_Q
    v_off = k_off + RET_Q
    g_off = v_off + RET_V
    hy_off = g_off + RET_V
    gate_off = hy_off + 3 * hy_w
    assert w_in.shape[2] == gate_off + 2 * d and hy_w == d

    xs, cs = x[0], ctx[0]
    up, down, w_in_b = ffn_up.astype(BF16), ffn_down.astype(BF16), w_in.astype(BF16)
    lg = ret_log_gamma[0].astype(F32)

    cv = jnp.concatenate([c, c_ctx[None, :], jnp.zeros((6, d), F32)], axis=0)
    mods = _adaln(cv, w_ada[0], b_ada[0]).reshape(8, N_MOD, d)
    mx, mc = mods[0], mods[1]

    xs = _ffn_half(xs, _mod_rows(mx, 0), norm_g[0, 0], up, down, 0)
    cs = _ffn_half(cs, _mod_rows(mc, 0), norm_g[0, 0], up, down, 0)

    pkv = _in_proj(cs, _mod_rows(mc, 3), norm_g[0, 1], w_in_b, k_off, g_off - k_off)
    st0 = _ctx_states(pkv, lg)

    cos_t, sin_t = _rope_tables(t)
    p = _in_proj(xs, _mod_rows(mx, 3), norm_g[0, 1], w_in_b, 0, w_in_b.shape[2],
                 rope=(cos_t, sin_t, v_off))

    y_ret = _retention(p, st0, lg, ret_gn_g[0], ret_gn_b[0], q_off, k_off, v_off, g_off)

    zx = _hy_pre(p, hy_conv_w[0], hy_conv_b[0], hy_off, hy_w)
    t_ext, feat_t = _hyena_features(t)
    e_pad = (-feat_t.shape[0]) % 8
    feat_t = jnp.pad(feat_t, ((0, e_pad), (0, 0)))
    w1 = jnp.pad(hy_ff_w1[0], ((0, e_pad), (0, 0)))
    h_pos = _filter_ffn(feat_t, w1, hy_ff_b1[0], hy_ff_w2[0], hy_ff_b2[0], hy_ff_w3[0], hy_ff_b3[0],
                        hy_sin_freq[0])
    h_taps, ssq = _filter_taps(h_pos, t_ext, hy_ff_w4[0].astype(BF16), hy_decay[0])

    n1 = DFT_N1
    e_half, f_fwd, f_inv = _dft_tables(n1, t // n1)
    bz = _dft_stage_a(zx, e_half, both_halves=False)
    bh = _dft_stage_a(h_taps, e_half, both_halves=True)
    cc = _dft_mid(bz, bh, f_fwd, f_inv)
    scale = lax.rsqrt(ssq + EPS) * (2.0 / (2 * t))
    hy_in = _dft_stage_a_inv(cc, e_half, zx, hy_bias[0], scale)

    xs = _merge(xs, y_ret, hy_in, p, gate_off, mx[5], w_ret_o.astype(BF16), w_hy_o.astype(BF16),
                w_out.astype(BF16))

    out = _ffn_half(xs, _mod_rows(mx, 6), norm_g[0, 2], up, down, 1, final_g=final_norm_g)
    return out[None]
```

```python
import functools
import math

import jax
import jax.numpy as jnp
from jax import lax
from jax.experimental import pallas as pl
from jax.experimental.pallas import tpu as pltpu

F32 = jnp.float32
BF16 = jnp.bfloat16
U32 = jnp.uint32

N_MOD = 9
GRID_W = 64
RET_HEADS = 8
RET_DK = 128
RET_DV = 256
RET_CHUNK = 128
RET_Q = RET_HEADS * RET_DK
RET_V = RET_HEADS * RET_DV
RET_PAIR = 2
HY_BANDS = 16
ROPE_BASE = 10000.0
EPS = 1e-6

TAPS_COLS = 256
CONV_COLS = 256
DFT_N1 = 128
SUBLANES = 8
LANES = 128
DFT_SLABS = 4
DFT_WIDTH = 1024
VMEM_BYTES = 64 * 1024 * 1024
VMEM_LIMIT = VMEM_BYTES - 8 * 1024 * 1024


def _cparams(*sem, vmem=VMEM_LIMIT):
    return pltpu.CompilerParams(dimension_semantics=sem, vmem_limit_bytes=vmem)


def _dot(a, b):
    return jnp.dot(a, b, preferred_element_type=F32)


def _dot_t0(a, b):
    return lax.dot_general(a, b, (((0,), (0,)), ((), ())), preferred_element_type=F32)


def _silu(v):
    return v * jax.nn.sigmoid(v)


def _norm_mod(x, g, shift, scale):
    return x * lax.rsqrt(jnp.mean(x * x, axis=-1, keepdims=True) + EPS) * (g * (1.0 + scale)) + shift


def _pack2(hi, lo):
    hb = lax.bitcast_convert_type(hi.astype(BF16).astype(F32), U32)
    lb = lax.bitcast_convert_type(lo.astype(BF16).astype(F32), U32)
    return hb | (lb >> 16)


def _unpack2(w):
    hi = lax.bitcast_convert_type(w & jnp.uint32(0xFFFF0000), F32)
    lo = lax.bitcast_convert_type(w << 16, F32)
    return hi, lo


def _adaln_kernel(c_ref, w_ref, b_ref, o_ref):
    s = _silu(c_ref[...]).astype(BF16)
    o_ref[...] = _dot(s, w_ref[...].astype(BF16)) + b_ref[...]


def _adaln(cv, w, b, tn=1024):
    rows, d = cv.shape
    n = w.shape[1]
    return pl.pallas_call(
        _adaln_kernel,
        grid=(n // tn,),
        in_specs=[pl.BlockSpec((rows, d), lambda j: (0, 0)),
                  pl.BlockSpec((d, tn), lambda j: (0, j)),
                  pl.BlockSpec((1, tn), lambda j: (0, j))],
        out_specs=pl.BlockSpec((rows, tn), lambda j: (0, j)),
        out_shape=jax.ShapeDtypeStruct((rows, n), F32),
        compiler_params=_cparams("arbitrary"),
        name="adaln",
    )(cv, w, b.reshape(1, n))


def _ffn_kernel(x_ref, mod_ref, g_ref, wa_ref, wg_ref, wd_ref, *rest, final):
    if final:
        fg_ref, o_ref, h_scr, acc_scr = rest
    else:
        o_ref, h_scr, acc_scr = rest
    j = pl.program_id(1)

    @pl.when(j == 0)
    def _():
        h = _norm_mod(x_ref[...], g_ref[...], mod_ref[0:1, :], mod_ref[1:2, :])
        h_scr[...] = h.astype(BF16)
        acc_scr[...] = jnp.zeros_like(acc_scr)

    h = h_scr[...]
    a = _dot(h, wa_ref[...])
    g = _dot(h, wg_ref[...])
    acc_scr[...] += _dot((_silu(a) * g).astype(BF16), wd_ref[...])

    @pl.when(j == pl.num_programs(1) - 1)
    def _():
        out = x_ref[...] + (0.5 * mod_ref[2:3, :]) * acc_scr[...]
        if final:
            out = out * lax.rsqrt(jnp.mean(out * out, axis=-1, keepdims=True) + EPS) * fg_ref[...]
        o_ref[...] = out


def _ffn_half(x, mod, g, w_up, w_down, layer, final_g=None, tm=512, tf=512):
    t, d = x.shape
    dff = w_down.shape[2]
    tm = min(tm, t)
    nf = dff // tf
    final = final_g is not None
    in_specs = [pl.BlockSpec((tm, d), lambda i, j: (i, 0)),
                pl.BlockSpec((8, d), lambda i, j: (0, 0)),
                pl.BlockSpec((1, d), lambda i, j: (0, 0)),
                pl.BlockSpec((None, None, d, tf), lambda i, j: (0, layer, 0, j)),
                pl.BlockSpec((None, None, d, tf), lambda i, j: (0, layer, 0, nf + j)),
                pl.BlockSpec((None, None, tf, d), lambda i, j: (0, layer, j, 0))]
    args = [x, mod, g.reshape(1, d), w_up, w_up, w_down]
    if final:
        in_specs.append(pl.BlockSpec((1, d), lambda i, j: (0, 0)))
        args.append(final_g.reshape(1, d))
    return pl.pallas_call(
        functools.partial(_ffn_kernel, final=final),
        grid=(t // tm, nf),
        in_specs=in_specs,
        out_specs=pl.BlockSpec((tm, d), lambda i, j: (i, 0)),
        out_shape=jax.ShapeDtypeStruct((t, d), F32),
        scratch_shapes=[pltpu.VMEM((tm, d), BF16), pltpu.VMEM((tm, d), F32)],
        compiler_params=_cparams("parallel", "arbitrary"),
        name="ffn_final" if final else "ffn_half",
    )(*args)


def _rope(x, cos, sin):
    lane = lax.broadcasted_iota(jnp.int32, x.shape, 1)
    partner = jnp.where((lane % 64) < 32, pltpu.roll(x, 96, 1), pltpu.roll(x, 32, 1))
    return x * cos + partner * sin


def _inproj_kernel(x_ref, mod_ref, g_ref, w_ref, *rest, rope_tiles):
    if rope_tiles:
        cos_ref, sin_ref, o_ref, h_scr = rest
    else:
        o_ref, h_scr = rest
    j = pl.program_id(1)

    @pl.when(j == 0)
    def _():
        h = _norm_mod(x_ref[...], g_ref[...], mod_ref[0:1, :], mod_ref[1:2, :])
        h_scr[...] = h.astype(BF16)

    if rope_tiles:
        @pl.when(j < rope_tiles)
        def _():
            acc = _dot(h_scr[...], w_ref[...])
            scale = jnp.where(j == 0, RET_DK ** -0.5, 1.0)
            cos = cos_ref[...] * scale
            sin = sin_ref[...] * scale
            for hb in range(acc.shape[1] // RET_DK):
                cols = slice(hb * RET_DK, (hb + 1) * RET_DK)
                o_ref[:, cols] = _rope(acc[:, cols], cos, sin).astype(BF16)

        @pl.when(j >= rope_tiles)
        def _():
            o_ref[...] = _dot(h_scr[...], w_ref[...]).astype(BF16)
    else:
        o_ref[...] = _dot(h_scr[...], w_ref[...]).astype(BF16)


def _in_proj(x, mod, g, w, col0, ncols, rope=None, tm=1024, tn=1024):
    t, d = x.shape
    tm = min(tm, t)
    off = col0 // tn
    in_specs = [pl.BlockSpec((tm, d), lambda i, j: (i, 0)),
                pl.BlockSpec((8, d), lambda i, j: (0, 0)),
                pl.BlockSpec((1, d), lambda i, j: (0, 0)),
                pl.BlockSpec((None, d, tn), lambda i, j: (0, 0, j + off))]
    args = [x, mod, g.reshape(1, d), w]
    rope_tiles = 0
    if rope is not None:
        cos_t, sin_t, qk_cols = rope
        assert tn == RET_Q and qk_cols == 2 * RET_Q and col0 == 0
        rope_tiles = qk_cols // tn
        in_specs += [pl.BlockSpec((tm, RET_DK), lambda i, j: (i, 0))] * 2
        args += [cos_t, sin_t]
    return pl.pallas_call(
        functools.partial(_inproj_kernel, rope_tiles=rope_tiles),
        grid=(t // tm, ncols // tn),
        in_specs=in_specs,
        out_specs=pl.BlockSpec((tm, tn), lambda i, j: (i, j)),
        out_shape=jax.ShapeDtypeStruct((t, ncols), BF16),
        scratch_shapes=[pltpu.VMEM((tm, d), BF16)],
        compiler_params=_cparams("parallel", "arbitrary"),
        name="in_proj",
    )(*args)


def _rope_tables(t):
    half = RET_DK // 2
    n_rows = t // GRID_W
    inv = ROPE_BASE ** (-jnp.arange(0, half, 2, dtype=F32) / half)
    ang_r = jnp.arange(n_rows, dtype=jnp.int32).astype(F32)[:, None] * inv[None, :]
    ang_c = jnp.arange(GRID_W, dtype=jnp.int32).astype(F32)[:, None] * inv[None, :]
    cos_r, sin_r, cos_c, sin_c = jnp.cos(ang_r), jnp.sin(ang_r), jnp.cos(ang_c), jnp.sin(ang_c)
    zr, zc = jnp.zeros((n_rows, half), F32), jnp.zeros((GRID_W, half), F32)

    def table(r1, r2, c1, c2):
        by_row = jnp.concatenate([r1, r2, zr], axis=-1)[:, None, :]
        by_col = jnp.concatenate([zc, c1, c2], axis=-1)[None, :, :]
        return (by_row + by_col).reshape(t, RET_DK)

    return table(cos_r, cos_r, cos_c, cos_c), table(-sin_r, sin_r, -sin_c, sin_c)


def _ctx_state_kernel(lg_ref, k_ref, v_ref, o_ref):
    h = pl.program_id(0)
    lc = k_ref.shape[0]
    m = lax.broadcasted_iota(jnp.int32, (lc, 1), 0).astype(F32)
    k = k_ref[...].astype(F32)
    v = v_ref[...]
    w_f = jnp.exp(lg_ref[0, h] * (lc - 1.0 - m))
    w_b = jnp.exp(lg_ref[1, h] * m)
    o_ref[0, 0] = _dot_t0((k * w_f).astype(BF16), v)
    o_ref[1, 0] = _dot_t0((k * w_b).astype(BF16), v)


def _ctx_states(pkv, lg):
    lc = pkv.shape[0]
    return pl.pallas_call(
        _ctx_state_kernel,
        grid=(RET_HEADS,),
        in_specs=[pl.BlockSpec(memory_space=pltpu.SMEM),
                  pl.BlockSpec((lc, RET_DK), lambda h: (0, h)),
                  pl.BlockSpec((lc, RET_DV), lambda h: (0, RET_Q // RET_DV + h))],
        out_specs=pl.BlockSpec((2, 1, RET_DK, RET_DV), lambda h: (0, h, 0, 0)),
        out_shape=jax.ShapeDtypeStruct((2, RET_HEADS, RET_DK, RET_DV), F32),
        compiler_params=_cparams("arbitrary"),
        name="ctx_states",
    )(lg, pkv, pkv)


def _ret_kernel(lg_ref, q_ref, k_ref, v_ref, g_ref, st0_ref, gng_ref, gnb_ref,
                o_ref, sb_all, sf_scr, sb_scr, *, nb, bc):
    c_len = RET_CHUNK
    hp = pl.program_id(0)
    i = pl.program_id(1)
    pair = range(RET_PAIR)
    lgf = [lg_ref[0, hp * RET_PAIR + e] for e in pair]
    lgb = [lg_ref[1, hp * RET_PAIR + e] for e in pair]
    kcols = [slice(e * RET_DK, (e + 1) * RET_DK) for e in pair]
    vcols = [slice(e * RET_DV, (e + 1) * RET_DV) for e in pair]
    pos = lax.broadcasted_iota(jnp.int32, (c_len, 1), 0).astype(F32)
    ones_row = jnp.ones((1, RET_DV), F32)

    @pl.when(i == 0)
    def _():
        for e in pair:
            sf_scr[e] = st0_ref[0, e]
            sb_scr[e] = st0_ref[1, e]

    @pl.when(i < nb)
    def _():
        rb = nb - 1 - i
        zeta_b = [jnp.exp(lgb[e] * pos) for e in pair]
        cd_b = [jnp.exp((lgb[e] * c_len) * ones_row) for e in pair]
        s_b = [sb_scr[e] for e in pair]
        for c in reversed(range(bc)):
            rows = slice(c * c_len, (c + 1) * c_len)
            for e in pair:
                sb_all[e, rb * bc + c] = s_b[e].astype(BF16)
                kz = (k_ref[rows, kcols[e]].astype(F32) * zeta_b[e]).astype(BF16)
                s_b[e] = s_b[e] * cd_b[e] + _dot_t0(kz, v_ref[rows, vcols[e]])
        for e in pair:
            sb_scr[e] = s_b[e]

    @pl.when(i >= nb)
    def _():
        rb = i - nb
        a = lax.broadcasted_iota(jnp.int32, (c_len, c_len), 0)
        b = lax.broadcasted_iota(jnp.int32, (c_len, c_len), 1)
        d = (a - b).astype(F32)
        decay = [jnp.where(d > 0, jnp.exp(lgf[e] * jnp.maximum(d, 0.0)),
                           jnp.where(d < 0, jnp.exp(lgb[e] * jnp.maximum(-d, 0.0)), 2.0)) for e in pair]
        xi_f = [jnp.exp(lgf[e] * (pos + 1.0)) for e in pair]
        xi_b = [jnp.exp(lgb[e] * (c_len - pos)) for e in pair]
        zeta_f = [jnp.exp(lgf[e] * (c_len - 1.0 - pos)) for e in pair]
        cd_f = [jnp.exp((lgf[e] * c_len) * ones_row) for e in pair]
        s_f = [sf_scr[e] for e in pair]
        for c in range(bc):
            rows = slice(c * c_len, (c + 1) * c_len)
            for e in pair:
                q = q_ref[rows, kcols[e]]
                k = k_ref[rows, kcols[e]]
                v = v_ref[rows, vcols[e]]
                s = lax.dot_general(q, k, (((1,), (1,)), ((), ())), preferred_element_type=F32) * decay[e]
                qf = q.astype(F32)
                lhs = jnp.concatenate([s.astype(BF16), (qf * xi_f[e]).astype(BF16), (qf * xi_b[e]).astype(BF16)],
                                      axis=1)
                rhs = jnp.concatenate([v, s_f[e].astype(BF16), sb_all[e, rb * bc + c]], axis=0)
                y = _dot(lhs, rhs)
                kz = (k.astype(F32) * zeta_f[e]).astype(BF16)
                s_f[e] = s_f[e] * cd_f[e] + _dot_t0(kz, v)
                mu = jnp.mean(y, axis=-1, keepdims=True)
                yc = y - mu
                var = jnp.mean(yc * yc, axis=-1, keepdims=True)
                yn = yc * lax.rsqrt(var + EPS) * gng_ref[:, vcols[e]] + gnb_ref[:, vcols[e]]
                o_ref[rows, vcols[e]] = (_silu(g_ref[rows, vcols[e]].astype(F32)) * yn).astype(BF16)
        for e in pair:
            sf_scr[e] = s_f[e]


def _retention(p, st0, lg, gn_g, gn_b, q_off, k_off, v_off, g_off, bc=16):
    t = p.shape[0]
    bc = min(bc, t // RET_CHUNK)
    rows = bc * RET_CHUNK
    nb = t // rows
    kw, vw = RET_PAIR * RET_DK, RET_PAIR * RET_DV

    def kv_rb(i):
        return jnp.where(i < nb, nb - 1 - i, i - nb)

    def q_rb(i):
        return jnp.maximum(i - nb, 0)

    return pl.pallas_call(
        functools.partial(_ret_kernel, nb=nb, bc=bc),
        grid=(RET_HEADS // RET_PAIR, 2 * nb),
        in_specs=[pl.BlockSpec(memory_space=pltpu.SMEM),
                  pl.BlockSpec((rows, kw), lambda h, i: (q_rb(i), q_off // kw + h)),
                  pl.BlockSpec((rows, kw), lambda h, i: (kv_rb(i), k_off // kw + h)),
                  pl.BlockSpec((rows, vw), lambda h, i: (kv_rb(i), v_off // vw + h)),
                  pl.BlockSpec((rows, vw), lambda h, i: (q_rb(i), g_off // vw + h)),
                  pl.BlockSpec((2, RET_PAIR, RET_DK, RET_DV), lambda h, i: (0, h, 0, 0)),
                  pl.BlockSpec((1, vw), lambda h, i: (0, h)),
                  pl.BlockSpec((1, vw), lambda h, i: (0, h))],
        out_specs=pl.BlockSpec((rows, vw), lambda h, i: (q_rb(i), h)),
        out_shape=jax.ShapeDtypeStruct((t, RET_V), BF16),
        scratch_shapes=[pltpu.VMEM((RET_PAIR, t // RET_CHUNK, RET_DK, RET_DV), BF16),
                        pltpu.VMEM((RET_PAIR, RET_DK, RET_DV), F32),
                        pltpu.VMEM((RET_PAIR, RET_DK, RET_DV), F32)],
        compiler_params=_cparams("arbitrary", "arbitrary"),
        name="retention",
    )(lg, p, p, p, p, st0, gn_g.reshape(1, RET_V), gn_b.reshape(1, RET_V))


def _hy_pre_kernel(m0, m1, m2, p0, p1, p2, n0, n1, n2, w_ref, b_ref, zx_ref):
    i = pl.program_id(0)
    tm, ch = m0.shape
    row = lax.broadcasted_iota(jnp.int32, (tm, 1), 0)
    has_prev = (i > 0).astype(F32)
    has_next = (i < pl.num_programs(0) - 1).astype(F32)
    halo = p0.shape[0]
    r_idx = lax.broadcasted_iota(jnp.int32, (tm, tm), 0)
    c_idx = lax.broadcasted_iota(jnp.int32, (tm, tm), 1)
    down = jnp.where(r_idx == c_idx + 1, 1.0, 0.0).astype(BF16)
    up = jnp.where(r_idx + 1 == c_idx, 1.0, 0.0).astype(BF16)

    def conv(main, prev, nxt, part, cs):
        cols = slice(part * ch + cs.start, part * ch + cs.stop)
        ub = main[:, cs]
        before = jnp.where(row == 0, prev[halo - 1:halo, cs].astype(F32) * has_prev, _dot(down, ub))
        after = jnp.where(row == tm - 1, nxt[0:1, cs].astype(F32) * has_next, _dot(up, ub))
        return (before * w_ref[0:1, cols] + ub.astype(F32) * w_ref[1:2, cols] + after * w_ref[2:3, cols]
                + b_ref[0:1, cols])

    for c0 in range(0, ch, CONV_COLS):
        cs = slice(c0, c0 + CONV_COLS)
        x0 = conv(m0, p0, n0, 0, cs)
        z = conv(m1, p1, n1, 1, cs) * conv(m2, p2, n2, 2, cs)
        zx_ref[:, cs] = _pack2(z, x0)


def _hy_pre(p, conv_w, conv_b, hy_off, ch, tm=256, halo=16):
    t = p.shape[0]
    tm = min(tm, t)
    nh = tm // halo
    last = t // halo - 1
    cb0 = hy_off // ch

    def main(part):
        return pl.BlockSpec((tm, ch), lambda i: (i, cb0 + part))

    def prev(part):
        return pl.BlockSpec((halo, ch), lambda i: (jnp.maximum(i * nh - 1, 0), cb0 + part))

    def nxt(part):
        return pl.BlockSpec((halo, ch), lambda i: (jnp.minimum((i + 1) * nh, last), cb0 + part))

    return pl.pallas_call(
        _hy_pre_kernel,
        grid=(t // tm,),
        in_specs=[main(0), main(1), main(2), prev(0), prev(1), prev(2), nxt(0), nxt(1), nxt(2),
                  pl.BlockSpec((3, 3 * ch), lambda i: (0, 0)),
                  pl.BlockSpec((1, 3 * ch), lambda i: (0, 0))],
        out_specs=pl.BlockSpec((tm, ch), lambda i: (i, 0)),
        out_shape=jax.ShapeDtypeStruct((t, ch), U32),
        compiler_params=_cparams("parallel"),
        name="hy_conv3",
    )(p, p, p, p, p, p, p, p, p, conv_w, conv_b.reshape(1, 3 * ch))


def _filt_ffn_kernel(f_ref, w1, b1, w2, b2, w3, b3, fr, o_ref):
    hp = lax.Precision.HIGHEST
    freq = fr[:, 0:1]
    h = jnp.sin(freq * (jnp.dot(w1[...], f_ref[...], precision=hp, preferred_element_type=F32) + b1[:, 0:1]))
    h = jnp.sin(freq * (jnp.dot(w2[...], h, precision=hp, preferred_element_type=F32) + b2[:, 0:1]))
    o_ref[...] = jnp.sin(freq * (jnp.dot(w3[...], h, precision=hp, preferred_element_type=F32) + b3[:, 0:1]))


def _filter_ffn(feat_t, w1, b1, w2, b2, w3, b3, freq, tl=2048):
    e, length = feat_t.shape
    ff = w1.shape[1]
    tl = min(tl, length)

    def col(v):
        return jnp.broadcast_to(v.reshape(ff, 1), (ff, 128))

    full = lambda shape: pl.BlockSpec(shape, lambda i: (0, 0))
    return pl.pallas_call(
        _filt_ffn_kernel,
        grid=(length // tl,),
        in_specs=[pl.BlockSpec((e, tl), lambda i: (0, i)),
                  full((ff, e)), full((ff, 128)), full((ff, ff)), full((ff, 128)),
                  full((ff, ff)), full((ff, 128)), full((ff, 128))],
        out_specs=pl.BlockSpec((ff, tl), lambda i: (0, i)),
        out_shape=jax.ShapeDtypeStruct((ff, length), F32),
        compiler_params=_cparams("parallel"),
        name="filter_ffn",
    )(feat_t, w1.T, col(b1), w2.T, col(b2), w3.T, col(b3), col(freq))


def _filt_taps_kernel(hf_ref, hb_ref, hn_ref, tf_ref, tb_ref, wf_ref, wb_ref, dec_ref, hp_ref, ssq_ref):
    i = pl.program_id(0)
    tr = hf_ref.shape[1]
    hf = hf_ref[...].astype(BF16)
    src = lax.broadcasted_iota(jnp.int32, (tr, tr), 0)
    dst = lax.broadcasted_iota(jnp.int32, (tr, tr), 1)
    rev_shift = jnp.where(src + dst == tr, 1.0, 0.0).astype(BF16)
    hb = _dot(hb_ref[...].astype(BF16), rev_shift)
    first = hn_ref[:, 0:1].astype(BF16).astype(F32)
    hb = jnp.where(lax.broadcasted_iota(jnp.int32, (1, tr), 1) == 0, first, hb).astype(BF16)
    t_f = tf_ref[:, 0:1]
    t_b = tb_ref[:, 0:1]
    row = lax.broadcasted_iota(jnp.int32, t_b.shape, 0)
    sign_b = jnp.where((row == 0) & (i == 0), 0.0, -1.0)

    @pl.when(i == 0)
    def _():
        ssq_ref[...] = jnp.zeros_like(ssq_ref)

    width = TAPS_COLS
    for c0 in range(0, dec_ref.shape[1], width):
        cols = slice(c0, c0 + width)
        dec = jnp.abs(dec_ref[:, cols])
        fwd = _dot_t0(hf, wf_ref[:, cols]) * jnp.exp(-t_f * dec)
        bwd = _dot_t0(hb, wb_ref[:, cols]) * (jnp.exp(-t_b * dec) * sign_b)
        ssq_ref[:, cols] += jnp.sum(fwd * fwd + bwd * bwd, axis=0, keepdims=True)
        hp_ref[:, cols] = _pack2(fwd, bwd)


def _filter_taps(h_pos, t_ext, w4, decay, tr=512):
    ff, length = h_pos.shape
    ch = decay.shape[0]
    tr = min(tr, length)
    nt = length // tr
    return pl.pallas_call(
        _filt_taps_kernel,
        grid=(nt,),
        in_specs=[pl.BlockSpec((ff, tr), lambda i: (0, i)),
                  pl.BlockSpec((ff, tr), lambda i: (0, nt - 1 - i)),
                  pl.BlockSpec((ff, tr), lambda i: (0, jnp.minimum(nt - i, nt - 1))),
                  pl.BlockSpec((tr, 1), lambda i: (i, 0)),
                  pl.BlockSpec((tr, 1), lambda i: (i + nt, 0)),
                  pl.BlockSpec((ff, ch), lambda i: (0, 0)),
                  pl.BlockSpec((ff, ch), lambda i: (0, 1)),
                  pl.BlockSpec((1, ch), lambda i: (0, 0))],
        out_specs=[pl.BlockSpec((tr, ch), lambda i: (i, 0)),
                   pl.BlockSpec((1, ch), lambda i: (0, 0))],
        out_shape=[jax.ShapeDtypeStruct((length, ch), U32), jax.ShapeDtypeStruct((1, ch), F32)],
        compiler_params=_cparams("arbitrary"),
        name="filter_taps",
    )(h_pos, h_pos, h_pos, t_ext, t_ext, w4, w4, decay.reshape(1, ch))


def _wide_spec(rows):
    return pl.BlockSpec((rows, None, SUBLANES, DFT_WIDTH), lambda g, j: (0, g, 0, j))


def _flat_scratch(rows, arrays=1):
    return [pltpu.VMEM((rows * SUBLANES, LANES), U32)] * (arrays * DFT_WIDTH // LANES)


def _flatten(block, flats):
    for s, f in enumerate(flats):
        f[...] = block[:, :, s * LANES:(s + 1) * LANES].reshape(f.shape)


def _gather_rows(flats, a):
    return jnp.concatenate([f[pl.ds(a, f.shape[0] // SUBLANES, stride=SUBLANES), :] for f in flats], axis=1)


def _lane_groups(flats, per=DFT_SLABS):
    return [(slice(i * per * LANES, (i + 1) * per * LANES), flats[i * per:(i + 1) * per])
            for i in range(len(flats) // per)]


def _dft_a_kernel(x_ref, e_ref, o_ref, *flats, both_halves):
    _flatten(x_ref, flats)
    for lanes, fl in _lane_groups(flats):
        for a in range(SUBLANES):
            hi, lo = _unpack2(_gather_rows(fl, a))
            r = _dot(e_ref[a], hi.astype(BF16))
            n = r.shape[0] // 2
            re, im = r[:n], r[n:]
            if both_halves:
                r_lo = _dot(e_ref[a], lo.astype(BF16))
                k2 = lax.broadcasted_iota(jnp.int32, (n, 1), 0)
                sign = jnp.where(k2 % 2 == 0, 1.0, -1.0)
                re, im = re + sign * r_lo[n:], im - sign * r_lo[:n]
            o_ref[a, :, lanes] = _pack2(re, im)


def _dft_stage_a(x, e_tab, both_halves):
    n1, m2, kdim = e_tab.shape
    length, c = x.shape
    k2 = length // n1
    assert kdim == k2
    return pl.pallas_call(
        functools.partial(_dft_a_kernel, both_halves=both_halves),
        grid=(n1 // SUBLANES, c // DFT_WIDTH),
        in_specs=[_wide_spec(k2), pl.BlockSpec((SUBLANES, m2, kdim), lambda g, j: (g, 0, 0))],
        out_specs=pl.BlockSpec((SUBLANES, m2 // 2, DFT_WIDTH), lambda g, j: (g, 0, j)),
        out_shape=jax.ShapeDtypeStruct((n1, m2 // 2, c), U32),
        scratch_shapes=_flat_scratch(k2),
        compiler_params=_cparams("parallel", "arbitrary"),
        name="dft_stage_a",
    )(x.reshape(k2, n1 // SUBLANES, SUBLANES, c), e_tab)


def _dft_mid_kernel(z_ref, h_ref, f_ref, finv_ref, o_ref, *flats):
    half = len(flats) // 2
    _flatten(z_ref, flats[:half])
    _flatten(h_ref, flats[half:])
    n = z_ref.shape[0]
    for (lanes, fz), (_, fh) in zip(_lane_groups(flats[:half], half), _lane_groups(flats[half:], half)):
        for a in range(SUBLANES):
            zr, zi = _unpack2(_gather_rows(fz, a))
            hr, hi = _unpack2(_gather_rows(fh, a))
            xs = _dot(f_ref[...], jnp.concatenate([zr, zi], axis=0).astype(BF16))
            ks = _dot(f_ref[...], jnp.concatenate([hr, hi], axis=0).astype(BF16))
            xr, xi, kr, ki = xs[:n], xs[n:], ks[:n], ks[n:]
            y = jnp.concatenate([xr * kr - xi * ki, xr * ki + xi * kr], axis=0).astype(BF16)
            cs = _dot(finv_ref[...], y)
            o_ref[a, :, lanes] = _pack2(cs[:n], cs[n:])


def _dft_mid(bz, bh, f_fwd, f_inv):
    n1, n2h, c = bz.shape
    mat = pl.BlockSpec((2 * n1, 2 * n1), lambda g, j: (0, 0))
    view = lambda b: b.reshape(n1, n2h // SUBLANES, SUBLANES, c)
    return pl.pallas_call(
        _dft_mid_kernel,
        grid=(n2h // SUBLANES, c // DFT_WIDTH),
        in_specs=[_wide_spec(n1), _wide_spec(n1), mat, mat],
        out_specs=pl.BlockSpec((SUBLANES, n1, DFT_WIDTH), lambda g, j: (g, 0, j)),
        out_shape=jax.ShapeDtypeStruct((n2h, n1, c), U32),
        scratch_shapes=_flat_scratch(n1, arrays=2),
        compiler_params=_cparams("parallel", "arbitrary"),
        name="dft_mid",
    )(view(bz), view(bh), f_fwd, f_inv)


def _dft_ainv_kernel(c_ref, zx_ref, e_ref, skip_ref, sc_ref, o_ref, *flats):
    half = len(flats) // 2
    _flatten(c_ref, flats[:half])
    _flatten(zx_ref, flats[half:])
    fc, fzx = flats[:half], flats[half:]
    mid = DFT_WIDTH // 2
    for a in range(SUBLANES):
        c_r, c_i = _unpack2(_gather_rows(fc, a))
        y = _dot_t0(e_ref[a], jnp.concatenate([c_r, c_i], axis=0).astype(BF16)) * sc_ref[...]
        z, x0 = _unpack2(_gather_rows(fzx, a))
        val = x0 * (y + z * skip_ref[...])
        o_ref[:, a, :] = _pack2(val[:, :mid], val[:, mid:])


def _dft_stage_a_inv(cc, e_tab, zx, skip, scale):
    n2h, n1, c = cc.shape
    length = zx.shape[0]
    k2 = length // n1
    assert e_tab.shape == (n1, 2 * n2h, k2)
    row = pl.BlockSpec((1, DFT_WIDTH), lambda g, j: (0, j))
    out = pl.pallas_call(
        _dft_ainv_kernel,
        grid=(n1 // SUBLANES, c // DFT_WIDTH),
        in_specs=[_wide_spec(n2h), _wide_spec(k2),
                  pl.BlockSpec((SUBLANES, 2 * n2h, k2), lambda g, j: (g, 0, 0)), row, row],
        out_specs=pl.BlockSpec((k2, None, SUBLANES, DFT_WIDTH // 2), lambda g, j: (0, g, 0, j)),
        out_shape=jax.ShapeDtypeStruct((k2, n1 // SUBLANES, SUBLANES, c // 2), U32),
        scratch_shapes=_flat_scratch(n2h) + _flat_scratch(k2),
        compiler_params=_cparams("parallel", "arbitrary"),
        name="dft_stage_a_inv",
    )(cc.reshape(n2h, n1 // SUBLANES, SUBLANES, c), zx.reshape(k2, n1 // SUBLANES, SUBLANES, c),
      e_tab, skip.reshape(1, c), scale)
    return out.reshape(length, c // 2)


def _dft_tables(n1, n2h):
    n2 = 2 * n2h
    n = n1 * n2
    a = jnp.arange(n1, dtype=jnp.int32)
    b = jnp.arange(n2h, dtype=jnp.int32)
    odd = 2 * jnp.arange(n2h, dtype=jnp.int32) + 1
    alpha = ((a[:, None] * odd[None, :]) % (2 * n)).astype(F32) * (math.pi / n)
    beta = ((odd[:, None] * b[None, :]) % (2 * n2)).astype(F32) * (math.pi / n2)
    ar, ai = jnp.cos(alpha)[:, :, None], -jnp.sin(alpha)[:, :, None]
    br, bi = jnp.cos(beta)[None], -jnp.sin(beta)[None]
    ar, ai, br, bi = lax.optimization_barrier((ar, ai, br, bi))
    er = ar * br - ai * bi
    ei = ar * bi + ai * br
    e_half = jnp.concatenate([er, ei], axis=1).astype(BF16)
    phi = ((a[:, None] * a[None, :]) % n1).astype(F32) * (2.0 * math.pi / n1)
    fr, fi = jnp.cos(phi), -jnp.sin(phi)
    f_fwd = jnp.concatenate([jnp.concatenate([fr, -fi], axis=1),
                             jnp.concatenate([fi, fr], axis=1)], axis=0).astype(BF16)
    f_inv = jnp.concatenate([jnp.concatenate([fr, fi], axis=1),
                             jnp.concatenate([-fi, fr], axis=1)], axis=0).astype(BF16)
    return e_half, f_fwd, f_inv


def _hyena_features(length):
    pos = jnp.arange(length, dtype=jnp.int32).astype(F32)
    t = pos / (length - 1.0)
    w = 2.0 * math.pi * pos[None, :] / length
    f = jnp.linspace(1e-4, HY_BANDS - 1.0, HY_BANDS, dtype=F32)[:, None]
    feat_t = jnp.concatenate([t[None, :], jnp.cos(f * w), -jnp.sin(f * w)], axis=0)
    r = jnp.arange(2 * length, dtype=jnp.int32)
    t_ext = jnp.where(r < length, r, 2 * length - r).astype(F32) / (length - 1.0)
    return t_ext[:, None], feat_t


def _merge_kernel(x_ref, yr_ref, hy_ref, gr_ref, gh_ref, gate_ref, wr_ref, wh_ref, wo_ref, o_ref):
    ret_out = _dot(yr_ref[...], wr_ref[...])
    hi, lo = _unpack2(hy_ref[...])
    half = DFT_WIDTH // 2
    hy = jnp.concatenate([part[:, b:b + half] for b in range(0, hi.shape[1], half) for part in (hi, lo)], axis=1)
    hy_out = _dot(hy.astype(BF16), wh_ref[...])
    m = (jax.nn.sigmoid(gr_ref[...].astype(F32)) * ret_out
         + jax.nn.sigmoid(gh_ref[...].astype(F32)) * hy_out)
    o_ref[...] = x_ref[...] + gate_ref[...] * _dot(m.astype(BF16), wo_ref[...])


def _merge(x, y_ret, hy_in, p, gate_off, gate, w_ret_o, w_hy_o, w_out, tm=256):
    t, d = x.shape
    tm = min(tm, t)
    gb = gate_off // d
    tile = lambda: pl.BlockSpec((tm, d), lambda i: (i, 0))
    weight = lambda w: pl.BlockSpec((None,) + w.shape[1:], lambda i: (0, 0, 0), pipeline_mode=pl.Buffered(1))
    return pl.pallas_call(
        _merge_kernel,
        grid=(t // tm,),
        in_specs=[tile(), tile(), pl.BlockSpec((tm, d // 2), lambda i: (i, 0)),
                  pl.BlockSpec((tm, d), lambda i: (i, gb)),
                  pl.BlockSpec((tm, d), lambda i: (i, gb + 1)),
                  pl.BlockSpec((1, d), lambda i: (0, 0)),
                  weight(w_ret_o), weight(w_hy_o), weight(w_out)],
        out_specs=tile(),
        out_shape=jax.ShapeDtypeStruct((t, d), F32),
        compiler_params=_cparams("parallel"),
        name="merge_out",
    )(x, y_ret, hy_in, p, p, gate.reshape(1, d), w_ret_o, w_hy_o, w_out)


def _mod_rows(mods, first):
    return jnp.pad(mods[first:first + 3], ((0, 5), (0, 0)))


def kernel(x, c, ctx, c_ctx, w_ada, b_ada, norm_g, ffn_up, ffn_down, w_in, ret_log_gamma, ret_gn_g,
           ret_gn_b, w_ret_o, hy_conv_w, hy_conv_b, hy_ff_w1, hy_ff_b1, hy_ff_w2, hy_ff_b2, hy_ff_w3,
           hy_ff_b3, hy_ff_w4, hy_sin_freq, hy_decay, hy_bias, w_hy_o, w_out, final_norm_g):
    batch, t, d = x.shape
    assert batch == 1 and w_ada.shape[0] == 1, "single sample, single layer"
    hy_w = hy_decay.shape[1]
    q_off, k_off = 0, RET_Q
    v_off = k_off + RET_Q
    g_off = v_off + RET_V
    hy_off = g_off + RET_V
    gate_off = hy_off + 3 * hy_w
    assert w_in.shape[2] == gate_off + 2 * d and hy_w == d

    xs, cs = x[0], ctx[0]
    up, down, w_in_b = ffn_up.astype(BF16), ffn_down.astype(BF16), w_in.astype(BF16)
    lg = ret_log_gamma[0].astype(F32)

    cv = jnp.concatenate([c, c_ctx[None, :], jnp.zeros((6, d), F32)], axis=0)
    mods = _adaln(cv, w_ada[0], b_ada[0]).reshape(8, N_MOD, d)
    mx, mc = mods[0], mods[1]

    xs = _ffn_half(xs, _mod_rows(mx, 0), norm_g[0, 0], up, down, 0)
    cs = _ffn_half(cs, _mod_rows(mc, 0), norm_g[0, 0], up, down, 0)

    pkv = _in_proj(cs, _mod_rows(mc, 3), norm_g[0, 1], w_in_b, k_off, g_off - k_off)
    st0 = _ctx_states(pkv, lg)

    cos_t, sin_t = _rope_tables(t)
    p = _in_proj(xs, _mod_rows(mx, 3), norm_g[0, 1], w_in_b, 0, w_in_b.shape[2],
                 rope=(cos_t, sin_t, v_off))

    y_ret = _retention(p, st0, lg, ret_gn_g[0], ret_gn_b[0], q_off, k_off, v_off, g_off)

    zx = _hy_pre(p, hy_conv_w[0], hy_conv_b[0], hy_off, hy_w)
    t_ext, feat_t = _hyena_features(t)
    e_pad = (-feat_t.shape[0]) % 8
    feat_t = jnp.pad(feat_t, ((0, e_pad), (0, 0)))
    w1 = jnp.pad(hy_ff_w1[0], ((0, e_pad), (0, 0)))
    h_pos = _filter_ffn(feat_t, w1, hy_ff_b1[0], hy_ff_w2[0], hy_ff_b2[0], hy_ff_w3[0], hy_ff_b3[0],
                        hy_sin_freq[0])
    h_taps, ssq = _filter_taps(h_pos, t_ext, hy_ff_w4[0].astype(BF16), hy_decay[0])

    n1 = DFT_N1
    e_half, f_fwd, f_inv = _dft_tables(n1, t // n1)
    bz = _dft_stage_a(zx, e_half, both_halves=False)
    bh = _dft_stage_a(h_taps, e_half, both_halves=True)
    cc = _dft_mid(bz, bh, f_fwd, f_inv)
    scale = lax.rsqrt(ssq + EPS) * (2.0 / (2 * t))
    hy_in = _dft_stage_a_inv(cc, e_half, zx, hy_bias[0], scale)

    xs = _merge(xs, y_ret, hy_in, p, gate_off, mx[5], w_ret_o.astype(BF16), w_hy_o.astype(BF16),
                w_out.astype(BF16))

    out = _ffn_half(xs, _mod_rows(mx, 6), norm_g[0, 2], up, down, 1, final_g=final_norm_g)
    return out[None]
```

```python
import functools
import math

import jax
import jax.numpy as jnp
from jax import lax
from jax.experimental import pallas as pl
from jax.experimental.pallas import tpu as pltpu

F32 = jnp.float32
BF16 = jnp.bfloat16
U32 = jnp.uint32

N_MOD = 9
GRID_W = 64
RET_HEADS = 8
RET_DK = 128
RET_DV = 256
RET_CHUNK = 128
RET_Q = RET_HEADS * RET_DK
RET_V = RET_HEADS * RET_DV
RET_PAIR = 2
HY_BANDS = 16
ROPE_BASE = 10000.0
EPS = 1e-6

TAPS_COLS = 256
CONV_COLS = 256
DFT_N1 = 128
SUBLANES = 8
LANES = 128
DFT_SLABS = 4
DFT_WIDTH = 1024
VMEM_BYTES = 64 * 1024 * 1024
VMEM_LIMIT = VMEM_BYTES - 8 * 1024 * 1024


def _cparams(*sem, vmem=VMEM_LIMIT):
    return pltpu.CompilerParams(dimension_semantics=sem, vmem_limit_bytes=vmem)


def _dot(a, b):
    return jnp.dot(a, b, preferred_element_type=F32)


def _dot_t0(a, b):
    return lax.dot_general(a, b, (((0,), (0,)), ((), ())), preferred_element_type=F32)


def _silu(v):
    return v * jax.nn.sigmoid(v)


def _norm_mod(x, g, shift, scale):
    return x * lax.rsqrt(jnp.mean(x * x, axis=-1, keepdims=True) + EPS) * (g * (1.0 + scale)) + shift


def _pack2(hi, lo):
    hb = lax.bitcast_convert_type(hi.astype(BF16).astype(F32), U32)
    lb = lax.bitcast_convert_type(lo.astype(BF16).astype(F32), U32)
    return hb | (lb >> 16)


def _unpack2(w):
    hi = lax.bitcast_convert_type(w & jnp.uint32(0xFFFF0000), F32)
    lo = lax.bitcast_convert_type(w << 16, F32)
    return hi, lo


def _adaln_kernel(c_ref, w_ref, b_ref, o_ref):
    s = _silu(c_ref[...]).astype(BF16)
    o_ref[...] = _dot(s, w_ref[...].astype(BF16)) + b_ref[...]


def _adaln(cv, w, b, tn=1024):
    rows, d = cv.shape
    n = w.shape[1]
    return pl.pallas_call(
        _adaln_kernel,
        grid=(n // tn,),
        in_specs=[pl.BlockSpec((rows, d), lambda j: (0, 0)),
                  pl.BlockSpec((d, tn), lambda j: (0, j)),
                  pl.BlockSpec((1, tn), lambda j: (0, j))],
        out_specs=pl.BlockSpec((rows, tn), lambda j: (0, j)),
        out_shape=jax.ShapeDtypeStruct((rows, n), F32),
        compiler_params=_cparams("arbitrary"),
        name="adaln",
    )(cv, w, b.reshape(1, n))


def _ffn_kernel(x_ref, mod_ref, g_ref, wa_ref, wg_ref, wd_ref, *rest, final):
    if final:
        fg_ref, o_ref, h_scr, acc_scr = rest
    else:
        o_ref, h_scr, acc_scr = rest
    j = pl.program_id(1)

    @pl.when(j == 0)
    def _():
        h = _norm_mod(x_ref[...], g_ref[...], mod_ref[0:1, :], mod_ref[1:2, :])
        h_scr[...] = h.astype(BF16)
        acc_scr[...] = jnp.zeros_like(acc_scr)

    h = h_scr[...]
    a = _dot(h, wa_ref[...])
    g = _dot(h, wg_ref[...])
    acc_scr[...] += _dot((_silu(a) * g).astype(BF16), wd_ref[...])

    @pl.when(j == pl.num_programs(1) - 1)
    def _():
        out = x_ref[...] + (0.5 * mod_ref[2:3, :]) * acc_scr[...]
        if final:
            out = out * lax.rsqrt(jnp.mean(out * out, axis=-1, keepdims=True) + EPS) * fg_ref[...]
        o_ref[...] = out


def _ffn_half(x, mod, g, w_up, w_down, layer, final_g=None, tm=512, tf=512):
    t, d = x.shape
    dff = w_down.shape[2]
    tm = min(tm, t)
    nf = dff // tf
    final = final_g is not None
    in_specs = [pl.BlockSpec((tm, d), lambda i, j: (i, 0)),
                pl.BlockSpec((8, d), lambda i, j: (0, 0)),
                pl.BlockSpec((1, d), lambda i, j: (0, 0)),
                pl.BlockSpec((None, None, d, tf), lambda i, j: (0, layer, 0, j)),
                pl.BlockSpec((None, None, d, tf), lambda i, j: (0, layer, 0, nf + j)),
                pl.BlockSpec((None, None, tf, d), lambda i, j: (0, layer, j, 0))]
    args = [x, mod, g.reshape(1, d), w_up, w_up, w_down]
    if final:
        in_specs.append(pl.BlockSpec((1, d), lambda i, j: (0, 0)))
        args.append(final_g.reshape(1, d))
    return pl.pallas_call(
        functools.partial(_ffn_kernel, final=final),
        grid=(t // tm, nf),
        in_specs=in_specs,
        out_specs=pl.BlockSpec((tm, d), lambda i, j: (i, 0)),
        out_shape=jax.ShapeDtypeStruct((t, d), F32),
        scratch_shapes=[pltpu.VMEM((tm, d), BF16), pltpu.VMEM((tm, d), F32)],
        compiler_params=_cparams("parallel", "arbitrary"),
        name="ffn_final" if final else "ffn_half",
    )(*args)


def _rope(x, cos, sin):
    lane = lax.broadcasted_iota(jnp.int32, x.shape, 1)
    partner = jnp.where((lane % 64) < 32, pltpu.roll(x, 96, 1), pltpu.roll(x, 32, 1))
    return x * cos + partner * sin


def _inproj_kernel(x_ref, mod_ref, g_ref, w_ref, *rest, rope_tiles):
    if rope_tiles:
        cos_ref, sin_ref, o_ref, h_scr = rest
    else:
        o_ref, h_scr = rest
    j = pl.program_id(1)

    @pl.when(j == 0)
    def _():
        h = _norm_mod(x_ref[...], g_ref[...], mod_ref[0:1, :], mod_ref[1:2, :])
        h_scr[...] = h.astype(BF16)

    if rope_tiles:
        @pl.when(j < rope_tiles)
        def _():
            acc = _dot(h_scr[...], w_ref[...])
            scale = jnp.where(j == 0, RET_DK ** -0.5, 1.0)
            cos = cos_ref[...] * scale
            sin = sin_ref[...] * scale
            for hb in range(acc.shape[1] // RET_DK):
                cols = slice(hb * RET_DK, (hb + 1) * RET_DK)
                o_ref[:, cols] = _rope(acc[:, cols], cos, sin).astype(BF16)

        @pl.when(j >= rope_tiles)
        def _():
            o_ref[...] = _dot(h_scr[...], w_ref[...]).astype(BF16)
    else:
        o_ref[...] = _dot(h_scr[...], w_ref[...]).astype(BF16)


def _in_proj(x, mod, g, w, col0, ncols, rope=None, tm=1024, tn=1024):
    t, d = x.shape
    tm = min(tm, t)
    off = col0 // tn
    in_specs = [pl.BlockSpec((tm, d), lambda i, j: (i, 0)),
                pl.BlockSpec((8, d), lambda i, j: (0, 0)),
                pl.BlockSpec((1, d), lambda i, j: (0, 0)),
                pl.BlockSpec((None, d, tn), lambda i, j: (0, 0, j + off))]
    args = [x, mod, g.reshape(1, d), w]
    rope_tiles = 0
    if rope is not None:
        cos_t, sin_t, qk_cols = rope
        assert tn == RET_Q and qk_cols == 2 * RET_Q and col0 == 0
        rope_tiles = qk_cols // tn
        in_specs += [pl.BlockSpec((tm, RET_DK), lambda i, j: (i, 0))] * 2
        args += [cos_t, sin_t]
    return pl.pallas_call(
        functools.partial(_inproj_kernel, rope_tiles=rope_tiles),
        grid=(t // tm, ncols // tn),
        in_specs=in_specs,
        out_specs=pl.BlockSpec((tm, tn), lambda i, j: (i, j)),
        out_shape=jax.ShapeDtypeStruct((t, ncols), BF16),
        scratch_shapes=[pltpu.VMEM((tm, d), BF16)],
        compiler_params=_cparams("parallel", "arbitrary"),
        name="in_proj",
    )(*args)


def _rope_tables(t):
    half = RET_DK // 2
    n_rows = t // GRID_W
    inv = ROPE_BASE ** (-jnp.arange(0, half, 2, dtype=F32) / half)
    ang_r = jnp.arange(n_rows, dtype=jnp.int32).astype(F32)[:, None] * inv[None, :]
    ang_c = jnp.arange(GRID_W, dtype=jnp.int32).astype(F32)[:, None] * inv[None, :]
    cos_r, sin_r, cos_c, sin_c = jnp.cos(ang_r), jnp.sin(ang_r), jnp.cos(ang_c), jnp.sin(ang_c)
    zr, zc = jnp.zeros((n_rows, half), F32), jnp.zeros((GRID_W, half), F32)

    def table(r1, r2, c1, c2):
        by_row = jnp.concatenate([r1, r2, zr], axis=-1)[:, None, :]
        by_col = jnp.concatenate([zc, c1, c2], axis=-1)[None, :, :]
        return (by_row + by_col).reshape(t, RET_DK)

    return table(cos_r, cos_r, cos_c, cos_c), table(-sin_r, sin_r, -sin_c, sin_c)


def _ctx_state_kernel(lg_ref, k_ref, v_ref, o_ref):
    h = pl.program_id(0)
    lc = k_ref.shape[0]
    m = lax.broadcasted_iota(jnp.int32, (lc, 1), 0).astype(F32)
    k = k_ref[...].astype(F32)
    v = v_ref[...]
    w_f = jnp.exp(lg_ref[0, h] * (lc - 1.0 - m))
    w_b = jnp.exp(lg_ref[1, h] * m)
    o_ref[0, 0] = _dot_t0((k * w_f).astype(BF16), v)
    o_ref[1, 0] = _dot_t0((k * w_b).astype(BF16), v)


def _ctx_states(pkv, lg):
    lc = pkv.shape[0]
    return pl.pallas_call(
        _ctx_state_kernel,
        grid=(RET_HEADS,),
        in_specs=[pl.BlockSpec(memory_space=pltpu.SMEM),
                  pl.BlockSpec((lc, RET_DK), lambda h: (0, h)),
                  pl.BlockSpec((lc, RET_DV), lambda h: (0, RET_Q // RET_DV + h))],
        out_specs=pl.BlockSpec((2, 1, RET_DK, RET_DV), lambda h: (0, h, 0, 0)),
        out_shape=jax.ShapeDtypeStruct((2, RET_HEADS, RET_DK, RET_DV), F32),
        compiler_params=_cparams("arbitrary"),
        name="ctx_states",
    )(lg, pkv, pkv)


def _ret_kernel(lg_ref, q_ref, k_ref, v_ref, g_ref, st0_ref, gng_ref, gnb_ref,
                o_ref, sb_all, sf_scr, sb_scr, *, nb, bc):
    c_len = RET_CHUNK
    hp = pl.program_id(0)
    i = pl.program_id(1)
    pair = range(RET_PAIR)
    lgf = [lg_ref[0, hp * RET_PAIR + e] for e in pair]
    lgb = [lg_ref[1, hp * RET_PAIR + e] for e in pair]
    kcols = [slice(e * RET_DK, (e + 1) * RET_DK) for e in pair]
    vcols = [slice(e * RET_DV, (e + 1) * RET_DV) for e in pair]
    pos = lax.broadcasted_iota(jnp.int32, (c_len, 1), 0).astype(F32)
    ones_row = jnp.ones((1, RET_DV), F32)

    @pl.when(i == 0)
    def _():
        for e in pair:
            sf_scr[e] = st0_ref[0, e]
            sb_scr[e] = st0_ref[1, e]

    @pl.when(i < nb)
    def _():
        rb = nb - 1 - i
        zeta_b = [jnp.exp(lgb[e] * pos) for e in pair]
        cd_b = [jnp.exp((lgb[e] * c_len) * ones_row) for e in pair]
        s_b = [sb_scr[e] for e in pair]
        for c in reversed(range(bc)):
            rows = slice(c * c_len, (c + 1) * c_len)
            for e in pair:
                sb_all[e, rb * bc + c] = s_b[e].astype(BF16)
                kz = (k_ref[rows, kcols[e]].astype(F32) * zeta_b[e]).astype(BF16)
                s_b[e] = s_b[e] * cd_b[e] + _dot_t0(kz, v_ref[rows, vcols[e]])
        for e in pair:
            sb_scr[e] = s_b[e]

    @pl.when(i >= nb)
    def _():
        rb = i - nb
        a = lax.broadcasted_iota(jnp.int32, (c_len, c_len), 0)
        b = lax.broadcasted_iota(jnp.int32, (c_len, c_len), 1)
        d = (a - b).astype(F32)
        decay = [jnp.where(d > 0, jnp.exp(lgf[e] * jnp.maximum(d, 0.0)),
                           jnp.where(d < 0, jnp.exp(lgb[e] * jnp.maximum(-d, 0.0)), 2.0)) for e in pair]
        xi_f = [jnp.exp(lgf[e] * (pos + 1.0)) for e in pair]
        xi_b = [jnp.exp(lgb[e] * (c_len - pos)) for e in pair]
        zeta_f = [jnp.exp(lgf[e] * (c_len - 1.0 - pos)) for e in pair]
        cd_f = [jnp.exp((lgf[e] * c_len) * ones_row) for e in pair]
        s_f = [sf_scr[e] for e in pair]
        for c in range(bc):
            rows = slice(c * c_len, (c + 1) * c_len)
            for e in pair:
                q = q_ref[rows, kcols[e]]
                k = k_ref[rows, kcols[e]]
                v = v_ref[rows, vcols[e]]
                s = lax.dot_general(q, k, (((1,), (1,)), ((), ())), preferred_element_type=F32) * decay[e]
                qf = q.astype(F32)
                lhs = jnp.concatenate([s.astype(BF16), (qf * xi_f[e]).astype(BF16), (qf * xi_b[e]).astype(BF16)],
                                      axis=1)
                rhs = jnp.concatenate([v, s_f[e].astype(BF16), sb_all[e, rb * bc + c]], axis=0)
                y = _dot(lhs, rhs)
                kz = (k.astype(F32) * zeta_f[e]).astype(BF16)
                s_f[e] = s_f[e] * cd_f[e] + _dot_t0(kz, v)
                mu = jnp.mean(y, axis=-1, keepdims=True)
                yc = y - mu
                var = jnp.mean(yc * yc, axis=-1, keepdims=True)
                yn = yc * lax.rsqrt(var + EPS) * gng_ref[:, vcols[e]] + gnb_ref[:, vcols[e]]
                o_ref[rows, vcols[e]] = (_silu(g_ref[rows, vcols[e]].astype(F32)) * yn).astype(BF16)
        for e in pair:
            sf_scr[e] = s_f[e]


def _retention(p, st0, lg, gn_g, gn_b, q_off, k_off, v_off, g_off, bc=16):
    t = p.shape[0]
    bc = min(bc, t // RET_CHUNK)
    rows = bc * RET_CHUNK
    nb = t // rows
    kw, vw = RET_PAIR * RET_DK, RET_PAIR * RET_DV

    def kv_rb(i):
        return jnp.where(i < nb, nb - 1 - i, i - nb)

    def q_rb(i):
        return jnp.maximum(i - nb, 0)

    return pl.pallas_call(
        functools.partial(_ret_kernel, nb=nb, bc=bc),
        grid=(RET_HEADS // RET_PAIR, 2 * nb),
        in_specs=[pl.BlockSpec(memory_space=pltpu.SMEM),
                  pl.BlockSpec((rows, kw), lambda h, i: (q_rb(i), q_off // kw + h)),
                  pl.BlockSpec((rows, kw), lambda h, i: (kv_rb(i), k_off // kw + h)),
                  pl.BlockSpec((rows, vw), lambda h, i: (kv_rb(i), v_off // vw + h)),
                  pl.BlockSpec((rows, vw), lambda h, i: (q_rb(i), g_off // vw + h)),
                  pl.BlockSpec((2, RET_PAIR, RET_DK, RET_DV), lambda h, i: (0, h, 0, 0)),
                  pl.BlockSpec((1, vw), lambda h, i: (0, h)),
                  pl.BlockSpec((1, vw), lambda h, i: (0, h))],
        out_specs=pl.BlockSpec((rows, vw), lambda h, i: (q_rb(i), h)),
        out_shape=jax.ShapeDtypeStruct((t, RET_V), BF16),
        scratch_shapes=[pltpu.VMEM((RET_PAIR, t // RET_CHUNK, RET_DK, RET_DV), BF16),
                        pltpu.VMEM((RET_PAIR, RET_DK, RET_DV), F32),
                        pltpu.VMEM((RET_PAIR, RET_DK, RET_DV), F32)],
        compiler_params=_cparams("arbitrary", "arbitrary"),
        name="retention",
    )(lg, p, p, p, p, st0, gn_g.reshape(1, RET_V), gn_b.reshape(1, RET_V))


def _hy_pre_kernel(m0, m1, m2, p0, p1, p2, n0, n1, n2, w_ref, b_ref, zx_ref):
    i = pl.program_id(0)
    tm, ch = m0.shape
    row = lax.broadcasted_iota(jnp.int32, (tm, 1), 0)
    has_prev = (i > 0).astype(F32)
    has_next = (i < pl.num_programs(0) - 1).astype(F32)
    halo = p0.shape[0]
    r_idx = lax.broadcasted_iota(jnp.int32, (tm, tm), 0)
    c_idx = lax.broadcasted_iota(jnp.int32, (tm, tm), 1)
    down = jnp.where(r_idx == c_idx + 1, 1.0, 0.0).astype(BF16)
    up = jnp.where(r_idx + 1 == c_idx, 1.0, 0.0).astype(BF16)

    def conv(main, prev, nxt, part, cs):
        cols = slice(part * ch + cs.start, part * ch + cs.stop)
        ub = main[:, cs]
        before = jnp.where(row == 0, prev[halo - 1:halo, cs].astype(F32) * has_prev, _dot(down, ub))
        after = jnp.where(row == tm - 1, nxt[0:1, cs].astype(F32) * has_next, _dot(up, ub))
        return (before * w_ref[0:1, cols] + ub.astype(F32) * w_ref[1:2, cols] + after * w_ref[2:3, cols]
                + b_ref[0:1, cols])

    for c0 in range(0, ch, CONV_COLS):
        cs = slice(c0, c0 + CONV_COLS)
        x0 = conv(m0, p0, n0, 0, cs)
        z = conv(m1, p1, n1, 1, cs) * conv(m2, p2, n2, 2, cs)
        zx_ref[:, cs] = _pack2(z, x0)


def _hy_pre(p, conv_w, conv_b, hy_off, ch, tm=256, halo=16):
    t = p.shape[0]
    tm = min(tm, t)
    nh = tm // halo
    last = t // halo - 1
    cb0 = hy_off // ch

    def main(part):
        return pl.BlockSpec((tm, ch), lambda i: (i, cb0 + part))

    def prev(part):
        return pl.BlockSpec((halo, ch), lambda i: (jnp.maximum(i * nh - 1, 0), cb0 + part))

    def nxt(part):
        return pl.BlockSpec((halo, ch), lambda i: (jnp.minimum((i + 1) * nh, last), cb0 + part))

    return pl.pallas_call(
        _hy_pre_kernel,
        grid=(t // tm,),
        in_specs=[main(0), main(1), main(2), prev(0), prev(1), prev(2), nxt(0), nxt(1), nxt(2),
                  pl.BlockSpec((3, 3 * ch), lambda i: (0, 0)),
                  pl.BlockSpec((1, 3 * ch), lambda i: (0, 0))],
        out_specs=pl.BlockSpec((tm, ch), lambda i: (i, 0)),
        out_shape=jax.ShapeDtypeStruct((t, ch), U32),
        compiler_params=_cparams("parallel"),
        name="hy_conv3",
    )(p, p, p, p, p, p, p, p, p, conv_w, conv_b.reshape(1, 3 * ch))


def _filt_ffn_kernel(f_ref, w1, b1, w2, b2, w3, b3, fr, o_ref):
    hp = lax.Precision.HIGHEST
    freq = fr[:, 0:1]
    h = jnp.sin(freq * (jnp.dot(w1[...], f_ref[...], precision=hp, preferred_element_type=F32) + b1[:, 0:1]))
    h = jnp.sin(freq * (jnp.dot(w2[...], h, precision=hp, preferred_element_type=F32) + b2[:, 0:1]))
    o_ref[...] = jnp.sin(freq * (jnp.dot(w3[...], h, precision=hp, preferred_element_type=F32) + b3[:, 0:1]))


def _filter_ffn(feat_t, w1, b1, w2, b2, w3, b3, freq, tl=2048):
    e, length = feat_t.shape
    ff = w1.shape[1]
    tl = min(tl, length)

    def col(v):
        return jnp.broadcast_to(v.reshape(ff, 1), (ff, 128))

    full = lambda shape: pl.BlockSpec(shape, lambda i: (0, 0))
    return pl.pallas_call(
        _filt_ffn_kernel,
        grid=(length // tl,),
        in_specs=[pl.BlockSpec((e, tl), lambda i: (0, i)),
                  full((ff, e)), full((ff, 128)), full((ff, ff)), full((ff, 128)),
                  full((ff, ff)), full((ff, 128)), full((ff, 128))],
        out_specs=pl.BlockSpec((ff, tl), lambda i: (0, i)),
        out_shape=jax.ShapeDtypeStruct((ff, length), F32),
        compiler_params=_cparams("parallel"),
        name="filter_ffn",
    )(feat_t, w1.T, col(b1), w2.T, col(b2), w3.T, col(b3), col(freq))


def _filt_taps_kernel(hf_ref, hb_ref, hn_ref, wf_ref, wb_ref, dec_ref, hp_ref, ssq_ref, win_f, win_b, *,
                      length):
    i = pl.program_id(0)
    tr = hf_ref.shape[1]
    inv = 1.0 / (length - 1.0)
    hf = hf_ref[...].astype(BF16)
    src = lax.broadcasted_iota(jnp.int32, (tr, tr), 0)
    dst = lax.broadcasted_iota(jnp.int32, (tr, tr), 1)
    rev_shift = jnp.where(src + dst == tr, 1.0, 0.0).astype(BF16)
    hb = _dot(hb_ref[...].astype(BF16), rev_shift)
    first = hn_ref[:, 0:1].astype(BF16).astype(F32)
    hb = jnp.where(lax.broadcasted_iota(jnp.int32, (1, tr), 1) == 0, first, hb).astype(BF16)
    row = lax.broadcasted_iota(jnp.int32, (tr, 1), 0)
    u = row.astype(F32)

    @pl.when(i == 0)
    def _():
        ssq_ref[...] = jnp.zeros_like(ssq_ref)
        for c0 in range(0, dec_ref.shape[1], TAPS_COLS):
            cols = slice(c0, c0 + TAPS_COLS)
            dec = jnp.abs(dec_ref[:, cols])
            win_f[:, cols] = jnp.exp(-(u * inv) * dec)
            win_b[:, cols] = -jnp.exp(-((tr - 1.0 - u) * inv) * dec)

    t0_f = (i * tr).astype(F32) * inv
    t0_b = (length - (i + 1) * tr + 1).astype(F32) * inv
    keep_b = jnp.where((row == 0) & (i == 0), 0.0, 1.0)
    for c0 in range(0, dec_ref.shape[1], TAPS_COLS):
        cols = slice(c0, c0 + TAPS_COLS)
        dec = jnp.abs(dec_ref[:, cols])
        fwd = _dot_t0(hf, wf_ref[:, cols]) * (win_f[:, cols] * jnp.exp(-t0_f * dec))
        bwd = _dot_t0(hb, wb_ref[:, cols]) * (win_b[:, cols] * (jnp.exp(-t0_b * dec) * keep_b))
        ssq_ref[:, cols] += jnp.sum(fwd * fwd + bwd * bwd, axis=0, keepdims=True)
        hp_ref[:, cols] = _pack2(fwd, bwd)


def _filter_taps(h_pos, w4, decay, tr=512):
    ff, length = h_pos.shape
    ch = decay.shape[0]
    tr = min(tr, length)
    nt = length // tr
    return pl.pallas_call(
        functools.partial(_filt_taps_kernel, length=length),
        grid=(nt,),
        in_specs=[pl.BlockSpec((ff, tr), lambda i: (0, i)),
                  pl.BlockSpec((ff, tr), lambda i: (0, nt - 1 - i)),
                  pl.BlockSpec((ff, tr), lambda i: (0, jnp.minimum(nt - i, nt - 1))),
                  pl.BlockSpec((ff, ch), lambda i: (0, 0)),
                  pl.BlockSpec((ff, ch), lambda i: (0, 1)),
                  pl.BlockSpec((1, ch), lambda i: (0, 0))],
        out_specs=[pl.BlockSpec((tr, ch), lambda i: (i, 0)),
                   pl.BlockSpec((1, ch), lambda i: (0, 0))],
        out_shape=[jax.ShapeDtypeStruct((length, ch), U32), jax.ShapeDtypeStruct((1, ch), F32)],
        scratch_shapes=[pltpu.VMEM((tr, ch), F32), pltpu.VMEM((tr, ch), F32)],
        compiler_params=_cparams("arbitrary"),
        name="filter_taps",
    )(h_pos, h_pos, h_pos, w4, w4, decay.reshape(1, ch))


def _wide_spec(rows):
    return pl.BlockSpec((rows, None, SUBLANES, DFT_WIDTH), lambda g, j: (0, g, 0, j))


def _flat_scratch(rows, arrays=1):
    return [pltpu.VMEM((rows * SUBLANES, LANES), U32)] * (arrays * DFT_WIDTH // LANES)


def _flatten(block, flats):
    for s, f in enumerate(flats):
        f[...] = block[:, :, s * LANES:(s + 1) * LANES].reshape(f.shape)


def _gather_rows(flats, a):
    return jnp.concatenate([f[pl.ds(a, f.shape[0] // SUBLANES, stride=SUBLANES), :] for f in flats], axis=1)


def _lane_groups(flats, per=DFT_SLABS):
    return [(slice(i * per * LANES, (i + 1) * per * LANES), flats[i * per:(i + 1) * per])
            for i in range(len(flats) // per)]


def _dft_a_kernel(x_ref, e_ref, o_ref, *flats, both_halves):
    _flatten(x_ref, flats)
    for lanes, fl in _lane_groups(flats):
        for a in range(SUBLANES):
            hi, lo = _unpack2(_gather_rows(fl, a))
            r = _dot(e_ref[a], hi.astype(BF16))
            n = r.shape[0] // 2
            re, im = r[:n], r[n:]
            if both_halves:
                r_lo = _dot(e_ref[a], lo.astype(BF16))
                k2 = lax.broadcasted_iota(jnp.int32, (n, 1), 0)
                sign = jnp.where(k2 % 2 == 0, 1.0, -1.0)
                re, im = re + sign * r_lo[n:], im - sign * r_lo[:n]
            o_ref[a, :, lanes] = _pack2(re, im)


def _dft_stage_a(x, e_tab, both_halves):
    n1, m2, kdim = e_tab.shape
    length, c = x.shape
    k2 = length // n1
    assert kdim == k2
    return pl.pallas_call(
        functools.partial(_dft_a_kernel, both_halves=both_halves),
        grid=(n1 // SUBLANES, c // DFT_WIDTH),
        in_specs=[_wide_spec(k2), pl.BlockSpec((SUBLANES, m2, kdim), lambda g, j: (g, 0, 0))],
        out_specs=pl.BlockSpec((SUBLANES, m2 // 2, DFT_WIDTH), lambda g, j: (g, 0, j)),
        out_shape=jax.ShapeDtypeStruct((n1, m2 // 2, c), U32),
        scratch_shapes=_flat_scratch(k2),
        compiler_params=_cparams("parallel", "arbitrary"),
        name="dft_stage_a",
    )(x.reshape(k2, n1 // SUBLANES, SUBLANES, c), e_tab)


def _dft_mid_kernel(z_ref, h_ref, f_ref, finv_ref, o_ref, *flats):
    half = len(flats) // 2
    _flatten(z_ref, flats[:half])
    _flatten(h_ref, flats[half:])
    n = z_ref.shape[0]
    for (lanes, fz), (_, fh) in zip(_lane_groups(flats[:half], half), _lane_groups(flats[half:], half)):
        for a in range(SUBLANES):
            zr, zi = _unpack2(_gather_rows(fz, a))
            hr, hi = _unpack2(_gather_rows(fh, a))
            xs = _dot(f_ref[...], jnp.concatenate([zr, zi], axis=0).astype(BF16))
            ks = _dot(f_ref[...], jnp.concatenate([hr, hi], axis=0).astype(BF16))
            xr, xi, kr, ki = xs[:n], xs[n:], ks[:n], ks[n:]
            y = jnp.concatenate([xr * kr - xi * ki, xr * ki + xi * kr], axis=0).astype(BF16)
            cs = _dot(finv_ref[...], y)
            o_ref[a, :, lanes] = _pack2(cs[:n], cs[n:])


def _dft_mid(bz, bh, f_fwd, f_inv):
    n1, n2h, c = bz.shape
    mat = pl.BlockSpec((2 * n1, 2 * n1), lambda g, j: (0, 0))
    view = lambda b: b.reshape(n1, n2h // SUBLANES, SUBLANES, c)
    return pl.pallas_call(
        _dft_mid_kernel,
        grid=(n2h // SUBLANES, c // DFT_WIDTH),
        in_specs=[_wide_spec(n1), _wide_spec(n1), mat, mat],
        out_specs=pl.BlockSpec((SUBLANES, n1, DFT_WIDTH), lambda g, j: (g, 0, j)),
        out_shape=jax.ShapeDtypeStruct((n2h, n1, c), U32),
        scratch_shapes=_flat_scratch(n1, arrays=2),
        compiler_params=_cparams("parallel", "arbitrary"),
        name="dft_mid",
    )(view(bz), view(bh), f_fwd, f_inv)


def _dft_ainv_kernel(c_ref, zx_ref, e_ref, skip_ref, sc_ref, o_ref, *flats):
    half = len(flats) // 2
    _flatten(c_ref, flats[:half])
    _flatten(zx_ref, flats[half:])
    fc, fzx = flats[:half], flats[half:]
    mid = DFT_WIDTH // 2
    for a in range(SUBLANES):
        c_r, c_i = _unpack2(_gather_rows(fc, a))
        y = _dot_t0(e_ref[a], jnp.concatenate([c_r, c_i], axis=0).astype(BF16)) * sc_ref[...]
        z, x0 = _unpack2(_gather_rows(fzx, a))
        val = x0 * (y + z * skip_ref[...])
        o_ref[:, a, :] = _pack2(val[:, :mid], val[:, mid:])


def _dft_stage_a_inv(cc, e_tab, zx, skip, scale):
    n2h, n1, c = cc.shape
    length = zx.shape[0]
    k2 = length // n1
    assert e_tab.shape == (n1, 2 * n2h, k2)
    row = pl.BlockSpec((1, DFT_WIDTH), lambda g, j: (0, j))
    out = pl.pallas_call(
        _dft_ainv_kernel,
        grid=(n1 // SUBLANES, c // DFT_WIDTH),
        in_specs=[_wide_spec(n2h), _wide_spec(k2),
                  pl.BlockSpec((SUBLANES, 2 * n2h, k2), lambda g, j: (g, 0, 0)), row, row],
        out_specs=pl.BlockSpec((k2, None, SUBLANES, DFT_WIDTH // 2), lambda g, j: (0, g, 0, j)),
        out_shape=jax.ShapeDtypeStruct((k2, n1 // SUBLANES, SUBLANES, c // 2), U32),
        scratch_shapes=_flat_scratch(n2h) + _flat_scratch(k2),
        compiler_params=_cparams("parallel", "arbitrary"),
        name="dft_stage_a_inv",
    )(cc.reshape(n2h, n1 // SUBLANES, SUBLANES, c), zx.reshape(k2, n1 // SUBLANES, SUBLANES, c),
      e_tab, skip.reshape(1, c), scale)
    return out.reshape(length, c // 2)


def _dft_tables(n1, n2h):
    n2 = 2 * n2h
    n = n1 * n2
    a = jnp.arange(n1, dtype=jnp.int32)
    b = jnp.arange(n2h, dtype=jnp.int32)
    odd = 2 * jnp.arange(n2h, dtype=jnp.int32) + 1
    alpha = ((a[:, None] * odd[None, :]) % (2 * n)).astype(F32) * (math.pi / n)
    beta = ((odd[:, None] * b[None, :]) % (2 * n2)).astype(F32) * (math.pi / n2)
    ar, ai = jnp.cos(alpha)[:, :, None], -jnp.sin(alpha)[:, :, None]
    br, bi = jnp.cos(beta)[None], -jnp.sin(beta)[None]
    ar, ai, br, bi = lax.optimization_barrier((ar, ai, br, bi))
    er = ar * br - ai * bi
    ei = ar * bi + ai * br
    e_half = jnp.concatenate([er, ei], axis=1).astype(BF16)
    phi = ((a[:, None] * a[None, :]) % n1).astype(F32) * (2.0 * math.pi / n1)
    fr, fi = jnp.cos(phi), -jnp.sin(phi)
    f_fwd = jnp.concatenate([jnp.concatenate([fr, -fi], axis=1),
                             jnp.concatenate([fi, fr], axis=1)], axis=0).astype(BF16)
    f_inv = jnp.concatenate([jnp.concatenate([fr, fi], axis=1),
                             jnp.concatenate([-fi, fr], axis=1)], axis=0).astype(BF16)
    return e_half, f_fwd, f_inv


def _hyena_features(length):
    pos = jnp.arange(length, dtype=jnp.int32).astype(F32)
    t = pos / (length - 1.0)
    w = 2.0 * math.pi * pos[None, :] / length
    f = jnp.linspace(1e-4, HY_BANDS - 1.0, HY_BANDS, dtype=F32)[:, None]
    return jnp.concatenate([t[None, :], jnp.cos(f * w), -jnp.sin(f * w)], axis=0)


def _merge_kernel(x_ref, yr_ref, hy_ref, gr_ref, gh_ref, gate_ref, wr_ref, wh_ref, wo_ref, o_ref):
    ret_out = _dot(yr_ref[...], wr_ref[...])
    hi, lo = _unpack2(hy_ref[...])
    half = DFT_WIDTH // 2
    hy = jnp.concatenate([part[:, b:b + half] for b in range(0, hi.shape[1], half) for part in (hi, lo)], axis=1)
    hy_out = _dot(hy.astype(BF16), wh_ref[...])
    m = (jax.nn.sigmoid(gr_ref[...].astype(F32)) * ret_out
         + jax.nn.sigmoid(gh_ref[...].astype(F32)) * hy_out)
    o_ref[...] = x_ref[...] + gate_ref[...] * _dot(m.astype(BF16), wo_ref[...])


def _merge(x, y_ret, hy_in, p, gate_off, gate, w_ret_o, w_hy_o, w_out, tm=256):
    t, d = x.shape
    tm = min(tm, t)
    gb = gate_off // d
    tile = lambda: pl.BlockSpec((tm, d), lambda i: (i, 0))
    weight = lambda w: pl.BlockSpec((None,) + w.shape[1:], lambda i: (0, 0, 0), pipeline_mode=pl.Buffered(1))
    return pl.pallas_call(
        _merge_kernel,
        grid=(t // tm,),
        in_specs=[tile(), tile(), pl.BlockSpec((tm, d // 2), lambda i: (i, 0)),
                  pl.BlockSpec((tm, d), lambda i: (i, gb)),
                  pl.BlockSpec((tm, d), lambda i: (i, gb + 1)),
                  pl.BlockSpec((1, d), lambda i: (0, 0)),
                  weight(w_ret_o), weight(w_hy_o), weight(w_out)],
        out_specs=tile(),
        out_shape=jax.ShapeDtypeStruct((t, d), F32),
        compiler_params=_cparams("parallel"),
        name="merge_out",
    )(x, y_ret, hy_in, p, p, gate.reshape(1, d), w_ret_o, w_hy_o, w_out)


def _mod_rows(mods, first):
    return jnp.pad(mods[first:first + 3], ((0, 5), (0, 0)))


def kernel(x, c, ctx, c_ctx, w_ada, b_ada, norm_g, ffn_up, ffn_down, w_in, ret_log_gamma, ret_gn_g,
           ret_gn_b, w_ret_o, hy_conv_w, hy_conv_b, hy_ff_w1, hy_ff_b1, hy_ff_w2, hy_ff_b2, hy_ff_w3,
           hy_ff_b3, hy_ff_w4, hy_sin_freq, hy_decay, hy_bias, w_hy_o, w_out, final_norm_g):
    batch, t, d = x.shape
    assert batch == 1 and w_ada.shape[0] == 1, "single sample, single layer"
    hy_w = hy_decay.shape[1]
    q_off, k_off = 0, RET_Q
    v_off = k_off + RET_Q
    g_off = v_off + RET_V
    hy_off = g_off + RET_V
    gate_off = hy_off + 3 * hy_w
    assert w_in.shape[2] == gate_off + 2 * d and hy_w == d

    xs, cs = x[0], ctx[0]
    up, down, w_in_b = ffn_up.astype(BF16), ffn_down.astype(BF16), w_in.astype(BF16)
    lg = ret_log_gamma[0].astype(F32)

    cv = jnp.concatenate([c, c_ctx[None, :], jnp.zeros((6, d), F32)], axis=0)
    mods = _adaln(cv, w_ada[0], b_ada[0]).reshape(8, N_MOD, d)
    mx, mc = mods[0], mods[1]

    xs = _ffn_half(xs, _mod_rows(mx, 0), norm_g[0, 0], up, down, 0)
    cs = _ffn_half(cs, _mod_rows(mc, 0), norm_g[0, 0], up, down, 0)

    pkv = _in_proj(cs, _mod_rows(mc, 3), norm_g[0, 1], w_in_b, k_off, g_off - k_off)
    st0 = _ctx_states(pkv, lg)

    cos_t, sin_t = _rope_tables(t)
    p = _in_proj(xs, _mod_rows(mx, 3), norm_g[0, 1], w_in_b, 0, w_in_b.shape[2],
                 rope=(cos_t, sin_t, v_off))

    y_ret = _retention(p, st0, lg, ret_gn_g[0], ret_gn_b[0], q_off, k_off, v_off, g_off)

    zx = _hy_pre(p, hy_conv_w[0], hy_conv_b[0], hy_off, hy_w)
    feat_t = _hyena_features(t)
    e_pad = (-feat_t.shape[0]) % 8
    feat_t = jnp.pad(feat_t, ((0, e_pad), (0, 0)))
    w1 = jnp.pad(hy_ff_w1[0], ((0, e_pad), (0, 0)))
    h_pos = _filter_ffn(feat_t, w1, hy_ff_b1[0], hy_ff_w2[0], hy_ff_b2[0], hy_ff_w3[0], hy_ff_b3[0],
                        hy_sin_freq[0])
    h_taps, ssq = _filter_taps(h_pos, hy_ff_w4[0].astype(BF16), hy_decay[0])

    n1 = DFT_N1
    e_half, f_fwd, f_inv = _dft_tables(n1, t // n1)
    bz = _dft_stage_a(zx, e_half, both_halves=False)
    bh = _dft_stage_a(h_taps, e_half, both_halves=True)
    cc = _dft_mid(bz, bh, f_fwd, f_inv)
    scale = lax.rsqrt(ssq + EPS) * (2.0 / (2 * t))
    hy_in = _dft_stage_a_inv(cc, e_half, zx, hy_bias[0], scale)

    xs = _merge(xs, y_ret, hy_in, p, gate_off, mx[5], w_ret_o.astype(BF16), w_hy_o.astype(BF16),
                w_out.astype(BF16))

    out = _ffn_half(xs, _mod_rows(mx, 6), norm_g[0, 2], up, down, 1, final_g=final_norm_g)
    return out[None]
```

```python
import functools
import math

import jax
import jax.numpy as jnp
from jax import lax
from jax.experimental import pallas as pl
from jax.experimental.pallas import tpu as pltpu

F32 = jnp.float32
BF16 = jnp.bfloat16
U32 = jnp.uint32

N_MOD = 9
GRID_W = 64
RET_HEADS = 8
RET_DK = 128
RET_DV = 256
RET_CHUNK = 128
RET_Q = RET_HEADS * RET_DK
RET_V = RET_HEADS * RET_DV
RET_PAIR = 2
HY_BANDS = 16
ROPE_BASE = 10000.0
EPS = 1e-6

TAPS_COLS = 256
CONV_COLS = 256
DFT_N1 = 128
SUBLANES = 8
LANES = 128
DFT_SLABS = 4
DFT_WIDTH = 1024
VMEM_BYTES = 64 * 1024 * 1024
VMEM_LIMIT = VMEM_BYTES - 8 * 1024 * 1024


def _cparams(*sem, vmem=VMEM_LIMIT):
    return pltpu.CompilerParams(dimension_semantics=sem, vmem_limit_bytes=vmem)


def _dot(a, b):
    return jnp.dot(a, b, preferred_element_type=F32)


def _dot_t0(a, b):
    return lax.dot_general(a, b, (((0,), (0,)), ((), ())), preferred_element_type=F32)


def _silu(v):
    return v * jax.nn.sigmoid(v)


def _norm_mod(x, g, shift, scale):
    return x * lax.rsqrt(jnp.mean(x * x, axis=-1, keepdims=True) + EPS) * (g * (1.0 + scale)) + shift


def _pack2(hi, lo):
    hb = lax.bitcast_convert_type(hi.astype(BF16).astype(F32), U32)
    lb = lax.bitcast_convert_type(lo.astype(BF16).astype(F32), U32)
    return hb | (lb >> 16)


def _unpack2(w):
    hi = lax.bitcast_convert_type(w & jnp.uint32(0xFFFF0000), F32)
    lo = lax.bitcast_convert_type(w << 16, F32)
    return hi, lo


def _adaln_kernel(c_ref, w_ref, b_ref, o_ref):
    s = _silu(c_ref[...]).astype(BF16)
    o_ref[...] = _dot(s, w_ref[...].astype(BF16)) + b_ref[...]


def _adaln(cv, w, b, tn=1024):
    rows, d = cv.shape
    n = w.shape[1]
    return pl.pallas_call(
        _adaln_kernel,
        grid=(n // tn,),
        in_specs=[pl.BlockSpec((rows, d), lambda j: (0, 0)),
                  pl.BlockSpec((d, tn), lambda j: (0, j)),
                  pl.BlockSpec((1, tn), lambda j: (0, j))],
        out_specs=pl.BlockSpec((rows, tn), lambda j: (0, j)),
        out_shape=jax.ShapeDtypeStruct((rows, n), F32),
        compiler_params=_cparams("arbitrary"),
        name="adaln",
    )(cv, w, b.reshape(1, n))


def _ffn_kernel(x_ref, xn_ref, mod_ref, g_ref, wa_ref, wg_ref, wd_ref, *rest, final, chunk):
    if final:
        fg_ref, o_ref, h_even, h_odd, acc_scr = rest
    else:
        o_ref, h_even, h_odd, acc_scr = rest
    i = pl.program_id(0)
    j = pl.program_id(1)

    def normed(x):
        return _norm_mod(x, g_ref[...], mod_ref[0:1, :], mod_ref[1:2, :]).astype(BF16)

    @pl.when((i == 0) & (j == 0))
    def _():
        h_even[...] = normed(x_ref[...])

    @pl.when(j == 0)
    def _():
        acc_scr[...] = jnp.zeros_like(acc_scr)

    c = j % (x_ref.shape[0] // chunk)
    rows = pl.ds(pl.multiple_of(c * chunk, chunk), chunk)

    def step(h_cur, h_next):
        h = h_cur[...]
        a = _dot(h, wa_ref[...])
        g = _dot(h, wg_ref[...])
        h_next[rows, :] = normed(xn_ref[rows, :])
        acc_scr[...] += _dot((_silu(a) * g).astype(BF16), wd_ref[...])

    @pl.when(i % 2 == 0)
    def _():
        step(h_even, h_odd)

    @pl.when(i % 2 == 1)
    def _():
        step(h_odd, h_even)

    @pl.when(j == pl.num_programs(1) - 1)
    def _():
        out = x_ref[...] + (0.5 * mod_ref[2:3, :]) * acc_scr[...]
        if final:
            out = out * lax.rsqrt(jnp.mean(out * out, axis=-1, keepdims=True) + EPS) * fg_ref[...]
        o_ref[...] = out


def _ffn_half(x, mod, g, w_up, w_down, layer, final_g=None, tm=512, tf=512):
    t, d = x.shape
    dff = w_down.shape[2]
    tm = min(tm, t)
    nf = dff // tf
    n_tiles = t // tm
    chunk = max(tm // min(nf, 8), 16)
    assert tm % chunk == 0 and tm // chunk <= nf
    final = final_g is not None
    in_specs = [pl.BlockSpec((tm, d), lambda i, j: (i, 0)),
                pl.BlockSpec((tm, d), lambda i, j: (jnp.minimum(i + 1, n_tiles - 1), 0)),
                pl.BlockSpec((8, d), lambda i, j: (0, 0)),
                pl.BlockSpec((1, d), lambda i, j: (0, 0)),
                pl.BlockSpec((None, None, d, tf), lambda i, j: (0, layer, 0, j)),
                pl.BlockSpec((None, None, d, tf), lambda i, j: (0, layer, 0, nf + j)),
                pl.BlockSpec((None, None, tf, d), lambda i, j: (0, layer, j, 0))]
    args = [x, x, mod, g.reshape(1, d), w_up, w_up, w_down]
    if final:
        in_specs.append(pl.BlockSpec((1, d), lambda i, j: (0, 0)))
        args.append(final_g.reshape(1, d))
    return pl.pallas_call(
        functools.partial(_ffn_kernel, final=final, chunk=chunk),
        grid=(n_tiles, nf),
        in_specs=in_specs,
        out_specs=pl.BlockSpec((tm, d), lambda i, j: (i, 0)),
        out_shape=jax.ShapeDtypeStruct((t, d), F32),
        scratch_shapes=[pltpu.VMEM((tm, d), BF16), pltpu.VMEM((tm, d), BF16), pltpu.VMEM((tm, d), F32)],
        compiler_params=_cparams("arbitrary", "arbitrary"),
        name="ffn_final" if final else "ffn_half",
    )(*args)


def _rope(x, cos, sin):
    lane = lax.broadcasted_iota(jnp.int32, x.shape, 1)
    partner = jnp.where((lane % 64) < 32, pltpu.roll(x, 96, 1), pltpu.roll(x, 32, 1))
    return x * cos + partner * sin


def _inproj_kernel(x_ref, mod_ref, g_ref, w_ref, *rest, rope_tiles):
    if rope_tiles:
        cos_ref, sin_ref, o_ref, h_scr = rest
    else:
        o_ref, h_scr = rest
    j = pl.program_id(1)

    @pl.when(j == 0)
    def _():
        h = _norm_mod(x_ref[...], g_ref[...], mod_ref[0:1, :], mod_ref[1:2, :])
        h_scr[...] = h.astype(BF16)

    if rope_tiles:
        @pl.when(j < rope_tiles)
        def _():
            acc = _dot(h_scr[...], w_ref[...])
            scale = jnp.where(j == 0, RET_DK ** -0.5, 1.0)
            cos = cos_ref[...] * scale
            sin = sin_ref[...] * scale
            for hb in range(acc.shape[1] // RET_DK):
                cols = slice(hb * RET_DK, (hb + 1) * RET_DK)
                o_ref[:, cols] = _rope(acc[:, cols], cos, sin).astype(BF16)

        @pl.when(j >= rope_tiles)
        def _():
            o_ref[...] = _dot(h_scr[...], w_ref[...]).astype(BF16)
    else:
        o_ref[...] = _dot(h_scr[...], w_ref[...]).astype(BF16)


def _in_proj(x, mod, g, w, col0, ncols, rope=None, tm=1024, tn=1024):
    t, d = x.shape
    tm = min(tm, t)
    off = col0 // tn
    in_specs = [pl.BlockSpec((tm, d), lambda i, j: (i, 0)),
                pl.BlockSpec((8, d), lambda i, j: (0, 0)),
                pl.BlockSpec((1, d), lambda i, j: (0, 0)),
                pl.BlockSpec((None, d, tn), lambda i, j: (0, 0, j + off))]
    args = [x, mod, g.reshape(1, d), w]
    rope_tiles = 0
    if rope is not None:
        cos_t, sin_t, qk_cols = rope
        assert tn == RET_Q and qk_cols == 2 * RET_Q and col0 == 0
        rope_tiles = qk_cols // tn
        in_specs += [pl.BlockSpec((tm, RET_DK), lambda i, j: (i, 0))] * 2
        args += [cos_t, sin_t]
    return pl.pallas_call(
        functools.partial(_inproj_kernel, rope_tiles=rope_tiles),
        grid=(t // tm, ncols // tn),
        in_specs=in_specs,
        out_specs=pl.BlockSpec((tm, tn), lambda i, j: (i, j)),
        out_shape=jax.ShapeDtypeStruct((t, ncols), BF16),
        scratch_shapes=[pltpu.VMEM((tm, d), BF16)],
        compiler_params=_cparams("parallel", "arbitrary"),
        name="in_proj",
    )(*args)


def _rope_tables(t):
    half = RET_DK // 2
    n_rows = t // GRID_W
    inv = ROPE_BASE ** (-jnp.arange(0, half, 2, dtype=F32) / half)
    ang_r = jnp.arange(n_rows, dtype=jnp.int32).astype(F32)[:, None] * inv[None, :]
    ang_c = jnp.arange(GRID_W, dtype=jnp.int32).astype(F32)[:, None] * inv[None, :]
    cos_r, sin_r, cos_c, sin_c = jnp.cos(ang_r), jnp.sin(ang_r), jnp.cos(ang_c), jnp.sin(ang_c)
    zr, zc = jnp.zeros((n_rows, half), F32), jnp.zeros((GRID_W, half), F32)

    def table(r1, r2, c1, c2):
        by_row = jnp.concatenate([r1, r2, zr], axis=-1)[:, None, :]
        by_col = jnp.concatenate([zc, c1, c2], axis=-1)[None, :, :]
        return (by_row + by_col).reshape(t, RET_DK)

    return table(cos_r, cos_r, cos_c, cos_c), table(-sin_r, sin_r, -sin_c, sin_c)


def _ctx_state_kernel(lg_ref, k_ref, v_ref, o_ref):
    h = pl.program_id(0)
    lc = k_ref.shape[0]
    m = lax.broadcasted_iota(jnp.int32, (lc, 1), 0).astype(F32)
    k = k_ref[...].astype(F32)
    v = v_ref[...]
    w_f = jnp.exp(lg_ref[0, h] * (lc - 1.0 - m))
    w_b = jnp.exp(lg_ref[1, h] * m)
    o_ref[0, 0] = _dot_t0((k * w_f).astype(BF16), v)
    o_ref[1, 0] = _dot_t0((k * w_b).astype(BF16), v)


def _ctx_states(pkv, lg):
    lc = pkv.shape[0]
    return pl.pallas_call(
        _ctx_state_kernel,
        grid=(RET_HEADS,),
        in_specs=[pl.BlockSpec(memory_space=pltpu.SMEM),
                  pl.BlockSpec((lc, RET_DK), lambda h: (0, h)),
                  pl.BlockSpec((lc, RET_DV), lambda h: (0, RET_Q // RET_DV + h))],
        out_specs=pl.BlockSpec((2, 1, RET_DK, RET_DV), lambda h: (0, h, 0, 0)),
        out_shape=jax.ShapeDtypeStruct((2, RET_HEADS, RET_DK, RET_DV), F32),
        compiler_params=_cparams("arbitrary"),
        name="ctx_states",
    )(lg, pkv, pkv)


def _ret_kernel(lg_ref, q_ref, k_ref, v_ref, g_ref, st0_ref, gng_ref, gnb_ref,
                o_ref, sb_all, sf_scr, sb_scr, *, nb, bc):
    c_len = RET_CHUNK
    hp = pl.program_id(0)
    i = pl.program_id(1)
    pair = range(RET_PAIR)
    lgf = [lg_ref[0, hp * RET_PAIR + e] for e in pair]
    lgb = [lg_ref[1, hp * RET_PAIR + e] for e in pair]
    kcols = [slice(e * RET_DK, (e + 1) * RET_DK) for e in pair]
    vcols = [slice(e * RET_DV, (e + 1) * RET_DV) for e in pair]
    pos = lax.broadcasted_iota(jnp.int32, (c_len, 1), 0).astype(F32)
    ones_row = jnp.ones((1, RET_DV), F32)

    @pl.when(i == 0)
    def _():
        for e in pair:
            sf_scr[e] = st0_ref[0, e]
            sb_scr[e] = st0_ref[1, e]

    @pl.when(i < nb)
    def _():
        rb = nb - 1 - i
        zeta_b = [jnp.exp(lgb[e] * pos) for e in pair]
        cd_b = [jnp.exp((lgb[e] * c_len) * ones_row) for e in pair]
        s_b = [sb_scr[e] for e in pair]
        for c in reversed(range(bc)):
            rows = slice(c * c_len, (c + 1) * c_len)
            for e in pair:
                sb_all[e, rb * bc + c] = s_b[e].astype(BF16)
                kz = (k_ref[rows, kcols[e]].astype(F32) * zeta_b[e]).astype(BF16)
                s_b[e] = s_b[e] * cd_b[e] + _dot_t0(kz, v_ref[rows, vcols[e]])
        for e in pair:
            sb_scr[e] = s_b[e]

    @pl.when(i >= nb)
    def _():
        rb = i - nb
        a = lax.broadcasted_iota(jnp.int32, (c_len, c_len), 0)
        b = lax.broadcasted_iota(jnp.int32, (c_len, c_len), 1)
        d = (a - b).astype(F32)
        decay = [jnp.where(d > 0, jnp.exp(lgf[e] * jnp.maximum(d, 0.0)),
                           jnp.where(d < 0, jnp.exp(lgb[e] * jnp.maximum(-d, 0.0)), 2.0)) for e in pair]
        xi_f = [jnp.exp(lgf[e] * (pos + 1.0)) for e in pair]
        xi_b = [jnp.exp(lgb[e] * (c_len - pos)) for e in pair]
        zeta_f = [jnp.exp(lgf[e] * (c_len - 1.0 - pos)) for e in pair]
        cd_f = [jnp.exp((lgf[e] * c_len) * ones_row) for e in pair]
        s_f = [sf_scr[e] for e in pair]
        for c in range(bc):
            rows = slice(c * c_len, (c + 1) * c_len)
            for e in pair:
                q = q_ref[rows, kcols[e]]
                k = k_ref[rows, kcols[e]]
                v = v_ref[rows, vcols[e]]
                s = lax.dot_general(q, k, (((1,), (1,)), ((), ())), preferred_element_type=F32) * decay[e]
                qf = q.astype(F32)
                lhs = jnp.concatenate([s.astype(BF16), (qf * xi_f[e]).astype(BF16), (qf * xi_b[e]).astype(BF16)],
                                      axis=1)
                rhs = jnp.concatenate([v, s_f[e].astype(BF16), sb_all[e, rb * bc + c]], axis=0)
                y = _dot(lhs, rhs)
                kz = (k.astype(F32) * zeta_f[e]).astype(BF16)
                s_f[e] = s_f[e] * cd_f[e] + _dot_t0(kz, v)
                mu = jnp.mean(y, axis=-1, keepdims=True)
                yc = y - mu
                var = jnp.mean(yc * yc, axis=-1, keepdims=True)
                yn = yc * lax.rsqrt(var + EPS) * gng_ref[:, vcols[e]] + gnb_ref[:, vcols[e]]
                o_ref[rows, vcols[e]] = (_silu(g_ref[rows, vcols[e]].astype(F32)) * yn).astype(BF16)
        for e in pair:
            sf_scr[e] = s_f[e]


def _retention(p, st0, lg, gn_g, gn_b, q_off, k_off, v_off, g_off, bc=16):
    t = p.shape[0]
    bc = min(bc, t // RET_CHUNK)
    rows = bc * RET_CHUNK
    nb = t // rows
    kw, vw = RET_PAIR * RET_DK, RET_PAIR * RET_DV

    def kv_rb(i):
        return jnp.where(i < nb, nb - 1 - i, i - nb)

    def q_rb(i):
        return jnp.maximum(i - nb, 0)

    return pl.pallas_call(
        functools.partial(_ret_kernel, nb=nb, bc=bc),
        grid=(RET_HEADS // RET_PAIR, 2 * nb),
        in_specs=[pl.BlockSpec(memory_space=pltpu.SMEM),
                  pl.BlockSpec((rows, kw), lambda h, i: (q_rb(i), q_off // kw + h)),
                  pl.BlockSpec((rows, kw), lambda h, i: (kv_rb(i), k_off // kw + h)),
                  pl.BlockSpec((rows, vw), lambda h, i: (kv_rb(i), v_off // vw + h)),
                  pl.BlockSpec((rows, vw), lambda h, i: (q_rb(i), g_off // vw + h)),
                  pl.BlockSpec((2, RET_PAIR, RET_DK, RET_DV), lambda h, i: (0, h, 0, 0)),
                  pl.BlockSpec((1, vw), lambda h, i: (0, h)),
                  pl.BlockSpec((1, vw), lambda h, i: (0, h))],
        out_specs=pl.BlockSpec((rows, vw), lambda h, i: (q_rb(i), h)),
        out_shape=jax.ShapeDtypeStruct((t, RET_V), BF16),
        scratch_shapes=[pltpu.VMEM((RET_PAIR, t // RET_CHUNK, RET_DK, RET_DV), BF16),
                        pltpu.VMEM((RET_PAIR, RET_DK, RET_DV), F32),
                        pltpu.VMEM((RET_PAIR, RET_DK, RET_DV), F32)],
        compiler_params=_cparams("arbitrary", "arbitrary"),
        name="retention",
    )(lg, p, p, p, p, st0, gn_g.reshape(1, RET_V), gn_b.reshape(1, RET_V))


def _hy_pre_kernel(m0, m1, m2, p0, p1, p2, n0, n1, n2, w_ref, b_ref, zx_ref):
    i = pl.program_id(0)
    tm, ch = m0.shape
    row = lax.broadcasted_iota(jnp.int32, (tm, 1), 0)
    has_prev = (i > 0).astype(F32)
    has_next = (i < pl.num_programs(0) - 1).astype(F32)
    halo = p0.shape[0]
    r_idx = lax.broadcasted_iota(jnp.int32, (tm, tm), 0)
    c_idx = lax.broadcasted_iota(jnp.int32, (tm, tm), 1)
    down = jnp.where(r_idx == c_idx + 1, 1.0, 0.0).astype(BF16)
    up = jnp.where(r_idx + 1 == c_idx, 1.0, 0.0).astype(BF16)

    def conv(main, prev, nxt, part, cs):
        cols = slice(part * ch + cs.start, part * ch + cs.stop)
        ub = main[:, cs]
        before = jnp.where(row == 0, prev[halo - 1:halo, cs].astype(F32) * has_prev, _dot(down, ub))
        after = jnp.where(row == tm - 1, nxt[0:1, cs].astype(F32) * has_next, _dot(up, ub))
        return (before * w_ref[0:1, cols] + ub.astype(F32) * w_ref[1:2, cols] + after * w_ref[2:3, cols]
                + b_ref[0:1, cols])

    for c0 in range(0, ch, CONV_COLS):
        cs = slice(c0, c0 + CONV_COLS)
        x0 = conv(m0, p0, n0, 0, cs)
        z = conv(m1, p1, n1, 1, cs) * conv(m2, p2, n2, 2, cs)
        zx_ref[:, cs] = _pack2(z, x0)


def _hy_pre(p, conv_w, conv_b, hy_off, ch, tm=256, halo=16):
    t = p.shape[0]
    tm = min(tm, t)
    nh = tm // halo
    last = t // halo - 1
    cb0 = hy_off // ch

    def main(part):
        return pl.BlockSpec((tm, ch), lambda i: (i, cb0 + part))

    def prev(part):
        return pl.BlockSpec((halo, ch), lambda i: (jnp.maximum(i * nh - 1, 0), cb0 + part))

    def nxt(part):
        return pl.BlockSpec((halo, ch), lambda i: (jnp.minimum((i + 1) * nh, last), cb0 + part))

    return pl.pallas_call(
        _hy_pre_kernel,
        grid=(t // tm,),
        in_specs=[main(0), main(1), main(2), prev(0), prev(1), prev(2), nxt(0), nxt(1), nxt(2),
                  pl.BlockSpec((3, 3 * ch), lambda i: (0, 0)),
                  pl.BlockSpec((1, 3 * ch), lambda i: (0, 0))],
        out_specs=pl.BlockSpec((tm, ch), lambda i: (i, 0)),
        out_shape=jax.ShapeDtypeStruct((t, ch), U32),
        compiler_params=_cparams("parallel"),
        name="hy_conv3",
    )(p, p, p, p, p, p, p, p, p, conv_w, conv_b.reshape(1, 3 * ch))


def _filt_ffn_kernel(f_ref, w1, b1, w2, b2, w3, b3, fr, o_ref):
    hp = lax.Precision.HIGHEST
    freq = fr[:, 0:1]
    h = jnp.sin(freq * (jnp.dot(w1[...], f_ref[...], precision=hp, preferred_element_type=F32) + b1[:, 0:1]))
    h = jnp.sin(freq * (jnp.dot(w2[...], h, precision=hp, preferred_element_type=F32) + b2[:, 0:1]))
    o_ref[...] = jnp.sin(freq * (jnp.dot(w3[...], h, precision=hp, preferred_element_type=F32) + b3[:, 0:1]))


def _filter_ffn(feat_t, w1, b1, w2, b2, w3, b3, freq, tl=2048):
    e, length = feat_t.shape
    ff = w1.shape[1]
    tl = min(tl, length)

    def col(v):
        return jnp.broadcast_to(v.reshape(ff, 1), (ff, 128))

    full = lambda shape: pl.BlockSpec(shape, lambda i: (0, 0))
    return pl.pallas_call(
        _filt_ffn_kernel,
        grid=(length // tl,),
        in_specs=[pl.BlockSpec((e, tl), lambda i: (0, i)),
                  full((ff, e)), full((ff, 128)), full((ff, ff)), full((ff, 128)),
                  full((ff, ff)), full((ff, 128)), full((ff, 128))],
        out_specs=pl.BlockSpec((ff, tl), lambda i: (0, i)),
        out_shape=jax.ShapeDtypeStruct((ff, length), F32),
        compiler_params=_cparams("parallel"),
        name="filter_ffn",
    )(feat_t, w1.T, col(b1), w2.T, col(b2), w3.T, col(b3), col(freq))


def _filt_taps_kernel(hf_ref, hb_ref, hn_ref, wf_ref, wb_ref, dec_ref, hp_ref, ssq_ref, win_f, win_b, *,
                      length):
    i = pl.program_id(0)
    tr = hf_ref.shape[1]
    inv = 1.0 / (length - 1.0)
    hf = hf_ref[...].astype(BF16)
    src = lax.broadcasted_iota(jnp.int32, (tr, tr), 0)
    dst = lax.broadcasted_iota(jnp.int32, (tr, tr), 1)
    rev_shift = jnp.where(src + dst == tr, 1.0, 0.0).astype(BF16)
    hb = _dot(hb_ref[...].astype(BF16), rev_shift)
    first = hn_ref[:, 0:1].astype(BF16).astype(F32)
    hb = jnp.where(lax.broadcasted_iota(jnp.int32, (1, tr), 1) == 0, first, hb).astype(BF16)
    row = lax.broadcasted_iota(jnp.int32, (tr, 1), 0)
    u = row.astype(F32)

    @pl.when(i == 0)
    def _():
        ssq_ref[...] = jnp.zeros_like(ssq_ref)
        for c0 in range(0, dec_ref.shape[1], TAPS_COLS):
            cols = slice(c0, c0 + TAPS_COLS)
            dec = jnp.abs(dec_ref[:, cols])
            win_f[:, cols] = jnp.exp(-(u * inv) * dec)
            win_b[:, cols] = -jnp.exp(-((tr - 1.0 - u) * inv) * dec)

    t0_f = (i * tr).astype(F32) * inv
    t0_b = (length - (i + 1) * tr + 1).astype(F32) * inv
    keep_b = jnp.where((row == 0) & (i == 0), 0.0, 1.0)
    for c0 in range(0, dec_ref.shape[1], TAPS_COLS):
        cols = slice(c0, c0 + TAPS_COLS)
        dec = jnp.abs(dec_ref[:, cols])
        fwd = _dot_t0(hf, wf_ref[:, cols]) * (win_f[:, cols] * jnp.exp(-t0_f * dec))
        bwd = _dot_t0(hb, wb_ref[:, cols]) * (win_b[:, cols] * (jnp.exp(-t0_b * dec) * keep_b))
        ssq_ref[:, cols] += jnp.sum(fwd * fwd + bwd * bwd, axis=0, keepdims=True)
        hp_ref[:, cols] = _pack2(fwd, bwd)


def _filter_taps(h_pos, w4, decay, tr=512):
    ff, length = h_pos.shape
    ch = decay.shape[0]
    tr = min(tr, length)
    nt = length // tr
    return pl.pallas_call(
        functools.partial(_filt_taps_kernel, length=length),
        grid=(nt,),
        in_specs=[pl.BlockSpec((ff, tr), lambda i: (0, i)),
                  pl.BlockSpec((ff, tr), lambda i: (0, nt - 1 - i)),
                  pl.BlockSpec((ff, tr), lambda i: (0, jnp.minimum(nt - i, nt - 1))),
                  pl.BlockSpec((ff, ch), lambda i: (0, 0)),
                  pl.BlockSpec((ff, ch), lambda i: (0, 1)),
                  pl.BlockSpec((1, ch), lambda i: (0, 0))],
        out_specs=[pl.BlockSpec((tr, ch), lambda i: (i, 0)),
                   pl.BlockSpec((1, ch), lambda i: (0, 0))],
        out_shape=[jax.ShapeDtypeStruct((length, ch), U32), jax.ShapeDtypeStruct((1, ch), F32)],
        scratch_shapes=[pltpu.VMEM((tr, ch), F32), pltpu.VMEM((tr, ch), F32)],
        compiler_params=_cparams("arbitrary"),
        name="filter_taps",
    )(h_pos, h_pos, h_pos, w4, w4, decay.reshape(1, ch))


def _wide_spec(rows):
    return pl.BlockSpec((rows, None, SUBLANES, DFT_WIDTH), lambda g, j: (0, g, 0, j))


def _flat_scratch(rows, arrays=1):
    return [pltpu.VMEM((rows * SUBLANES, LANES), U32)] * (arrays * DFT_WIDTH // LANES)


def _flatten(block, flats):
    for s, f in enumerate(flats):
        f[...] = block[:, :, s * LANES:(s + 1) * LANES].reshape(f.shape)


def _gather_rows(flats, a):
    return jnp.concatenate([f[pl.ds(a, f.shape[0] // SUBLANES, stride=SUBLANES), :] for f in flats], axis=1)


def _lane_groups(flats, per=DFT_SLABS):
    return [(slice(i * per * LANES, (i + 1) * per * LANES), flats[i * per:(i + 1) * per])
            for i in range(len(flats) // per)]


def _dft_a_kernel(x_ref, e_ref, o_ref, *flats, both_halves):
    _flatten(x_ref, flats)
    for lanes, fl in _lane_groups(flats):
        for a in range(SUBLANES):
            hi, lo = _unpack2(_gather_rows(fl, a))
            r = _dot(e_ref[a], hi.astype(BF16))
            n = r.shape[0] // 2
            re, im = r[:n], r[n:]
            if both_halves:
                r_lo = _dot(e_ref[a], lo.astype(BF16))
                k2 = lax.broadcasted_iota(jnp.int32, (n, 1), 0)
                sign = jnp.where(k2 % 2 == 0, 1.0, -1.0)
                re, im = re + sign * r_lo[n:], im - sign * r_lo[:n]
            o_ref[a, :, lanes] = _pack2(re, im)


def _dft_stage_a(x, e_tab, both_halves):
    n1, m2, kdim = e_tab.shape
    length, c = x.shape
    k2 = length // n1
    assert kdim == k2
    return pl.pallas_call(
        functools.partial(_dft_a_kernel, both_halves=both_halves),
        grid=(n1 // SUBLANES, c // DFT_WIDTH),
        in_specs=[_wide_spec(k2), pl.BlockSpec((SUBLANES, m2, kdim), lambda g, j: (g, 0, 0))],
        out_specs=pl.BlockSpec((SUBLANES, m2 // 2, DFT_WIDTH), lambda g, j: (g, 0, j)),
        out_shape=jax.ShapeDtypeStruct((n1, m2 // 2, c), U32),
        scratch_shapes=_flat_scratch(k2),
        compiler_params=_cparams("parallel", "arbitrary"),
        name="dft_stage_a",
    )(x.reshape(k2, n1 // SUBLANES, SUBLANES, c), e_tab)


def _dft_mid_kernel(z_ref, h_ref, f_ref, finv_ref, o_ref, *flats):
    half = len(flats) // 2
    _flatten(z_ref, flats[:half])
    _flatten(h_ref, flats[half:])
    n = z_ref.shape[0]
    for (lanes, fz), (_, fh) in zip(_lane_groups(flats[:half], half), _lane_groups(flats[half:], half)):
        for a in range(SUBLANES):
            zr, zi = _unpack2(_gather_rows(fz, a))
            hr, hi = _unpack2(_gather_rows(fh, a))
            xs = _dot(f_ref[...], jnp.concatenate([zr, zi], axis=0).astype(BF16))
            ks = _dot(f_ref[...], jnp.concatenate([hr, hi], axis=0).astype(BF16))
            xr, xi, kr, ki = xs[:n], xs[n:], ks[:n], ks[n:]
            y = jnp.concatenate([xr * kr - xi * ki, xr * ki + xi * kr], axis=0).astype(BF16)
            cs = _dot(finv_ref[...], y)
            o_ref[a, :, lanes] = _pack2(cs[:n], cs[n:])


def _dft_mid(bz, bh, f_fwd, f_inv):
    n1, n2h, c = bz.shape
    mat = pl.BlockSpec((2 * n1, 2 * n1), lambda g, j: (0, 0))
    view = lambda b: b.reshape(n1, n2h // SUBLANES, SUBLANES, c)
    return pl.pallas_call(
        _dft_mid_kernel,
        grid=(n2h // SUBLANES, c // DFT_WIDTH),
        in_specs=[_wide_spec(n1), _wide_spec(n1), mat, mat],
        out_specs=pl.BlockSpec((SUBLANES, n1, DFT_WIDTH), lambda g, j: (g, 0, j)),
        out_shape=jax.ShapeDtypeStruct((n2h, n1, c), U32),
        scratch_shapes=_flat_scratch(n1, arrays=2),
        compiler_params=_cparams("parallel", "arbitrary"),
        name="dft_mid",
    )(view(bz), view(bh), f_fwd, f_inv)


def _dft_ainv_kernel(c_ref, zx_ref, e_ref, skip_ref, sc_ref, o_ref, *flats):
    half = len(flats) // 2
    _flatten(c_ref, flats[:half])
    _flatten(zx_ref, flats[half:])
    fc, fzx = flats[:half], flats[half:]
    mid = DFT_WIDTH // 2
    for a in range(SUBLANES):
        c_r, c_i = _unpack2(_gather_rows(fc, a))
        y = _dot_t0(e_ref[a], jnp.concatenate([c_r, c_i], axis=0).astype(BF16)) * sc_ref[...]
        z, x0 = _unpack2(_gather_rows(fzx, a))
        val = x0 * (y + z * skip_ref[...])
        o_ref[:, a, :] = _pack2(val[:, :mid], val[:, mid:])


def _dft_stage_a_inv(cc, e_tab, zx, skip, scale):
    n2h, n1, c = cc.shape
    length = zx.shape[0]
    k2 = length // n1
    assert e_tab.shape == (n1, 2 * n2h, k2)
    row = pl.BlockSpec((1, DFT_WIDTH), lambda g, j: (0, j))
    out = pl.pallas_call(
        _dft_ainv_kernel,
        grid=(n1 // SUBLANES, c // DFT_WIDTH),
        in_specs=[_wide_spec(n2h), _wide_spec(k2),
                  pl.BlockSpec((SUBLANES, 2 * n2h, k2), lambda g, j: (g, 0, 0)), row, row],
        out_specs=pl.BlockSpec((k2, None, SUBLANES, DFT_WIDTH // 2), lambda g, j: (0, g, 0, j)),
        out_shape=jax.ShapeDtypeStruct((k2, n1 // SUBLANES, SUBLANES, c // 2), U32),
        scratch_shapes=_flat_scratch(n2h) + _flat_scratch(k2),
        compiler_params=_cparams("parallel", "arbitrary"),
        name="dft_stage_a_inv",
    )(cc.reshape(n2h, n1 // SUBLANES, SUBLANES, c), zx.reshape(k2, n1 // SUBLANES, SUBLANES, c),
      e_tab, skip.reshape(1, c), scale)
    return out.reshape(length, c // 2)


def _dft_tables(n1, n2h):
    n2 = 2 * n2h
    n = n1 * n2
    a = jnp.arange(n1, dtype=jnp.int32)
    b = jnp.arange(n2h, dtype=jnp.int32)
    odd = 2 * jnp.arange(n2h, dtype=jnp.int32) + 1
    alpha = ((a[:, None] * odd[None, :]) % (2 * n)).astype(F32) * (math.pi / n)
    beta = ((odd[:, None] * b[None, :]) % (2 * n2)).astype(F32) * (math.pi / n2)
    ar, ai = jnp.cos(alpha)[:, :, None], -jnp.sin(alpha)[:, :, None]
    br, bi = jnp.cos(beta)[None], -jnp.sin(beta)[None]
    ar, ai, br, bi = lax.optimization_barrier((ar, ai, br, bi))
    er = ar * br - ai * bi
    ei = ar * bi + ai * br
    e_half = jnp.concatenate([er, ei], axis=1).astype(BF16)
    phi = ((a[:, None] * a[None, :]) % n1).astype(F32) * (2.0 * math.pi / n1)
    fr, fi = jnp.cos(phi), -jnp.sin(phi)
    f_fwd = jnp.concatenate([jnp.concatenate([fr, -fi], axis=1),
                             jnp.concatenate([fi, fr], axis=1)], axis=0).astype(BF16)
    f_inv = jnp.concatenate([jnp.concatenate([fr, fi], axis=1),
                             jnp.concatenate([-fi, fr], axis=1)], axis=0).astype(BF16)
    return e_half, f_fwd, f_inv


def _hyena_features(length):
    pos = jnp.arange(length, dtype=jnp.int32).astype(F32)
    t = pos / (length - 1.0)
    w = 2.0 * math.pi * pos[None, :] / length
    f = jnp.linspace(1e-4, HY_BANDS - 1.0, HY_BANDS, dtype=F32)[:, None]
    return jnp.concatenate([t[None, :], jnp.cos(f * w), -jnp.sin(f * w)], axis=0)


def _merge_kernel(x_ref, yr_ref, hy_ref, gr_ref, gh_ref, gate_ref, wr_ref, wh_ref, wo_ref, o_ref):
    ret_out = _dot(yr_ref[...], wr_ref[...])
    hi, lo = _unpack2(hy_ref[...])
    half = DFT_WIDTH // 2
    hy = jnp.concatenate([part[:, b:b + half] for b in range(0, hi.shape[1], half) for part in (hi, lo)], axis=1)
    hy_out = _dot(hy.astype(BF16), wh_ref[...])
    m = (jax.nn.sigmoid(gr_ref[...].astype(F32)) * ret_out
         + jax.nn.sigmoid(gh_ref[...].astype(F32)) * hy_out)
    o_ref[...] = x_ref[...] + gate_ref[...] * _dot(m.astype(BF16), wo_ref[...])


def _merge(x, y_ret, hy_in, p, gate_off, gate, w_ret_o, w_hy_o, w_out, tm=256):
    t, d = x.shape
    tm = min(tm, t)
    gb = gate_off // d
    tile = lambda: pl.BlockSpec((tm, d), lambda i: (i, 0))
    weight = lambda w: pl.BlockSpec((None,) + w.shape[1:], lambda i: (0, 0, 0), pipeline_mode=pl.Buffered(1))
    return pl.pallas_call(
        _merge_kernel,
        grid=(t // tm,),
        in_specs=[tile(), tile(), pl.BlockSpec((tm, d // 2), lambda i: (i, 0)),
                  pl.BlockSpec((tm, d), lambda i: (i, gb)),
                  pl.BlockSpec((tm, d), lambda i: (i, gb + 1)),
                  pl.BlockSpec((1, d), lambda i: (0, 0)),
                  weight(w_ret_o), weight(w_hy_o), weight(w_out)],
        out_specs=tile(),
        out_shape=jax.ShapeDtypeStruct((t, d), F32),
        compiler_params=_cparams("parallel"),
        name="merge_out",
    )(x, y_ret, hy_in, p, p, gate.reshape(1, d), w_ret_o, w_hy_o, w_out)


def _mod_rows(mods, first):
    return jnp.pad(mods[first:first + 3], ((0, 5), (0, 0)))


def kernel(x, c, ctx, c_ctx, w_ada, b_ada, norm_g, ffn_up, ffn_down, w_in, ret_log_gamma, ret_gn_g,
           ret_gn_b, w_ret_o, hy_conv_w, hy_conv_b, hy_ff_w1, hy_ff_b1, hy_ff_w2, hy_ff_b2, hy_ff_w3,
           hy_ff_b3, hy_ff_w4, hy_sin_freq, hy_decay, hy_bias, w_hy_o, w_out, final_norm_g):
    batch, t, d = x.shape
    assert batch == 1 and w_ada.shape[0] == 1, "single sample, single layer"
    hy_w = hy_decay.shape[1]
    q_off, k_off = 0, RET_Q
    v_off = k_off + RET_Q
    g_off = v_off + RET_V
    hy_off = g_off + RET_V
    gate_off = hy_off + 3 * hy_w
    assert w_in.shape[2] == gate_off + 2 * d and hy_w == d

    xs, cs = x[0], ctx[0]
    up, down, w_in_b = ffn_up.astype(BF16), ffn_down.astype(BF16), w_in.astype(BF16)
    lg = ret_log_gamma[0].astype(F32)

    cv = jnp.concatenate([c, c_ctx[None, :], jnp.zeros((6, d), F32)], axis=0)
    mods = _adaln(cv, w_ada[0], b_ada[0]).reshape(8, N_MOD, d)
    mx, mc = mods[0], mods[1]

    xs = _ffn_half(xs, _mod_rows(mx, 0), norm_g[0, 0], up, down, 0)
    cs = _ffn_half(cs, _mod_rows(mc, 0), norm_g[0, 0], up, down, 0)

    pkv = _in_proj(cs, _mod_rows(mc, 3), norm_g[0, 1], w_in_b, k_off, g_off - k_off)
    st0 = _ctx_states(pkv, lg)

    cos_t, sin_t = _rope_tables(t)
    p = _in_proj(xs, _mod_rows(mx, 3), norm_g[0, 1], w_in_b, 0, w_in_b.shape[2],
                 rope=(cos_t, sin_t, v_off))

    y_ret = _retention(p, st0, lg, ret_gn_g[0], ret_gn_b[0], q_off, k_off, v_off, g_off)

    zx = _hy_pre(p, hy_conv_w[0], hy_conv_b[0], hy_off, hy_w)
    feat_t = _hyena_features(t)
    e_pad = (-feat_t.shape[0]) % 8
    feat_t = jnp.pad(feat_t, ((0, e_pad), (0, 0)))
    w1 = jnp.pad(hy_ff_w1[0], ((0, e_pad), (0, 0)))
    h_pos = _filter_ffn(feat_t, w1, hy_ff_b1[0], hy_ff_w2[0], hy_ff_b2[0], hy_ff_w3[0], hy_ff_b3[0],
                        hy_sin_freq[0])
    h_taps, ssq = _filter_taps(h_pos, hy_ff_w4[0].astype(BF16), hy_decay[0])

    n1 = DFT_N1
    e_half, f_fwd, f_inv = _dft_tables(n1, t // n1)
    bz = _dft_stage_a(zx, e_half, both_halves=False)
    bh = _dft_stage_a(h_taps, e_half, both_halves=True)
    cc = _dft_mid(bz, bh, f_fwd, f_inv)
    scale = lax.rsqrt(ssq + EPS) * (2.0 / (2 * t))
    hy_in = _dft_stage_a_inv(cc, e_half, zx, hy_bias[0], scale)

    xs = _merge(xs, y_ret, hy_in, p, gate_off, mx[5], w_ret_o.astype(BF16), w_hy_o.astype(BF16),
                w_out.astype(BF16))

    out = _ffn_half(xs, _mod_rows(mx, 6), norm_g[0, 2], up, down, 1, final_g=final_norm_g)
    return out[None]
```
